```python
import jax
import jax.numpy as jnp
from jax import lax
import numpy as np


D_MODEL = 1024
BATCH = 8
SEQ = 8192
DEPTH = 1

CTX_LEN = 256
GRID_W = 64
EPS = 1e-6

POOL_WINDOWS = (2, 4, 8, 16)
POOL_GROUPS = len(POOL_WINDOWS)
POOL_WIDTH = D_MODEL // 2
POOL_GC = POOL_WIDTH // POOL_GROUPS

M_HEADS = 4
M_HEAD_DIM = D_MODEL // 8
M_WIDTH = M_HEADS * M_HEAD_DIM
CONV_W = 3
CHUNK = 128
F_BIAS_INIT = 3.0

OFF_POOL = 0
OFF_Q = OFF_POOL + POOL_WIDTH
OFF_K = OFF_Q + M_WIDTH
OFF_V = OFF_K + M_WIDTH
OFF_O = OFF_V + M_WIDTH
OFF_GATE = OFF_O + M_WIDTH
N_GATE = 4 * M_HEADS
OFF_MERGE = OFF_GATE + N_GATE
IN_WIDTH = OFF_MERGE + 2 * D_MODEL

N_EXPERTS = 32
TOP_K = 4
D_FF = D_MODEL
SWIGLU_LIMIT = 7.0
SWIGLU_ALPHA = 1.702
MOE_BLOCK = 512

kernel_name = 'hybrid_pool_mlstm_moe_dit'


def rmsnorm(x, w):
    xf = x.astype(jnp.float32)
    y = xf * lax.rsqrt(jnp.mean(xf * xf, axis=-1, keepdims=True) + EPS)
    return (y * w.astype(jnp.float32)).astype(x.dtype)


def modulate(h, shift, scale):
    return h * (1.0 + scale) + shift


def centred_pool(u):
    W = u.shape[2]
    pos = np.arange(W)
    outs = []
    for g, win in enumerate(POOL_WINDOWS):
        ug = u[..., g * POOL_GC:(g + 1) * POOL_GC].astype(jnp.float32)
        cs = jnp.pad(jnp.cumsum(ug, axis=2), ((0, 0), (0, 0), (1, 0), (0, 0)))
        lo = np.clip(pos - win // 2, 0, W)
        hi = np.clip(pos + win // 2, 0, W)
        cnt = (hi - lo).astype(np.float32)[:, None]
        mean = (jnp.take(cs, hi, axis=2) - jnp.take(cs, lo, axis=2)) / cnt
        outs.append(mean - ug)
    return jnp.concatenate(outs, axis=-1).astype(u.dtype)


def short_conv(u, w, b):
    L = u.shape[1]
    pad = CONV_W // 2
    up = jnp.pad(u, ((0, 0), (pad, pad), (0, 0)))
    acc = b
    for j in range(CONV_W):
        acc = acc + up[:, j:j + L] * w[j]
    return jax.nn.silu(acc)


def to_heads(a):
    B, L, _ = a.shape
    return a.reshape(B, L, M_HEADS, M_HEAD_DIM).transpose(0, 2, 1, 3).astype(jnp.float32)


def mlstm_inputs(p, conv_w, conv_b, gate_b):
    qk = short_conv(p[..., OFF_Q:OFF_V], conv_w, conv_b)
    q = to_heads(qk[..., :M_WIDTH]) * (M_HEAD_DIM ** -0.5)
    k = to_heads(qk[..., M_WIDTH:])
    v = to_heads(p[..., OFF_V:OFF_O])
    B, L, _ = p.shape
    g = (p[..., OFF_GATE:OFF_MERGE] + gate_b).astype(jnp.float32)
    g = g.reshape(B, L, 4, M_HEADS).transpose(2, 0, 3, 1)
    return (q, k, v, g[0], jax.nn.log_sigmoid(g[1]), g[2], jax.nn.log_sigmoid(g[3]))


def zero_state(B):
    return (jnp.zeros((B, M_HEADS, M_HEAD_DIM, M_HEAD_DIM), jnp.float32),
            jnp.zeros((B, M_HEADS, M_HEAD_DIM), jnp.float32),
            jnp.zeros((B, M_HEADS), jnp.float32))


def mlstm_scan(q, k, v, ig, lf, state):
    B, H, L, Dh = q.shape
    N = L // CHUNK

    def chunks(a):
        return jnp.moveaxis(a.reshape(a.shape[:2] + (N, CHUNK) + a.shape[3:]), 2, 0)

    mask = jnp.asarray(np.tril(np.ones((CHUNK, CHUNK), dtype=bool)))

    def step(carry, xs):
        C, n, m = carry
        qc, kc, vc, ic, fc = xs
        b = jnp.cumsum(fc, axis=-1)
        logw = jnp.where(mask, b[..., :, None] - b[..., None, :] + ic[..., None, :], -jnp.inf)
        m_inter = b + m[..., None]
        m_t = jnp.maximum(jnp.max(logw, axis=-1), m_inter)
        s = jnp.einsum('bhtd,bhsd->bhts', qc, kc) * jnp.exp(logw - m_t[..., None])
        decay = jnp.exp(m_inter - m_t)
        num = jnp.einsum('bhts,bhsd->bhtd', s, vc) + decay[..., None] * jnp.einsum('bhtk,bhkv->bhtv', qc, C)
        den = jnp.sum(s, axis=-1) + decay * jnp.einsum('bhtk,bhk->bht', qc, n)
        h = num / jnp.maximum(jnp.abs(den), jnp.exp(-m_t))[..., None]
        btot = b[..., -1]
        log_a = btot[..., None] - b + ic
        m_new = jnp.maximum(btot + m, jnp.max(log_a, axis=-1))
        a = jnp.exp(log_a - m_new[..., None])
        cd = jnp.exp(btot + m - m_new)
        C_new = cd[..., None, None] * C + jnp.einsum('bhs,bhsk,bhsv->bhkv', a, kc, vc)
        n_new = cd[..., None] * n + jnp.einsum('bhs,bhsk->bhk', a, kc)
        return (C_new, n_new, m_new), h

    state, h = lax.scan(step, state, (chunks(q), chunks(k), chunks(v), chunks(ig), chunks(lf)))
    return jnp.moveaxis(h, 0, 2).reshape(B, H, L, Dh), state


def mlstm_bidir(q, k, v, i_f, lf_f, i_b, lf_b, init_f, init_b):
    h_f, st_f = mlstm_scan(q, k, v, i_f, lf_f, init_f)
    fl = lambda a: jnp.flip(a, axis=2)
    h_b, st_b = mlstm_scan(fl(q), fl(k), fl(v), fl(i_b), fl(lf_b), init_b)
    return h_f + fl(h_b), st_f, st_b


def mixer_merge(p, pooled, h_m, w_pool, pool_scale, hnorm_w, w_bp, w_bm, w_out):
    B, L, _ = p.shape
    pg = pooled.reshape(B, L, POOL_GROUPS, POOL_GC)
    ya = jnp.einsum('blgc,gcd->blgd', pg, w_pool).reshape(B, L, POOL_WIDTH) * pool_scale
    hn = h_m * lax.rsqrt(jnp.mean(h_m * h_m, axis=-1, keepdims=True) + EPS)
    hn = hn.transpose(0, 2, 1, 3).reshape(B, L, M_WIDTH).astype(p.dtype) * hnorm_w
    yb = hn * jax.nn.sigmoid(p[..., OFF_O:OFF_GATE])
    ga = jax.nn.sigmoid(p[..., OFF_MERGE:OFF_MERGE + D_MODEL])
    gb = jax.nn.sigmoid(p[..., OFF_MERGE + D_MODEL:])
    y = ga * (ya @ w_bp) + gb * (yb @ w_bm)
    return y @ w_out


def moe(h, w_router, b_router, w1, b1, w2, b2):
    shape = h.shape
    x = h.reshape(-1, D_MODEL)
    T = x.shape[0]
    logits = (x @ w_router + b_router).astype(jnp.float32)
    top_v, top_i = lax.top_k(logits, TOP_K)
    top_w = jax.nn.softmax(top_v, axis=-1)
    TK = T * TOP_K
    flat_e = top_i.reshape(-1)
    flat_t = jnp.arange(TK, dtype=jnp.int32) // TOP_K
    flat_w = top_w.reshape(-1)
    order = jnp.argsort(flat_e)
    se = flat_e[order]
    counts = jnp.bincount(flat_e, length=N_EXPERTS)
    starts = jnp.cumsum(counts) - counts
    pcounts = (counts + MOE_BLOCK - 1) // MOE_BLOCK * MOE_BLOCK
    pends = jnp.cumsum(pcounts)
    pstarts = pends - pcounts
    dest = pstarts[se] + jnp.arange(TK, dtype=jnp.int32) - starts[se]
    n_blocks = (TK + MOE_BLOCK - 1) // MOE_BLOCK + N_EXPERTS
    row_tok = jnp.full((n_blocks * MOE_BLOCK,), T, jnp.int32).at[dest].set(flat_t[order])
    row_w = jnp.zeros((n_blocks * MOE_BLOCK,), jnp.float32).at[dest].set(flat_w[order])
    block_e = jnp.minimum(jnp.searchsorted(pends, jnp.arange(n_blocks) * MOE_BLOCK, side='right'),
                          N_EXPERTS - 1).astype(jnp.int32)
    x_pad = jnp.concatenate([x, jnp.zeros((1, D_MODEL), x.dtype)], axis=0)

    def block(y, xs):
        e, rows, wts = xs
        gu = x_pad[rows] @ w1[e] + b1[e]
        gate = jnp.minimum(gu[..., :D_FF], SWIGLU_LIMIT)
        up = jnp.clip(gu[..., D_FF:], -SWIGLU_LIMIT, SWIGLU_LIMIT)
        act = (up + 1.0) * gate * jax.nn.sigmoid(SWIGLU_ALPHA * gate)
        out = act @ w2[e] + b2[e]
        return y.at[rows].add(out * wts[:, None].astype(out.dtype)), None

    y0 = jnp.zeros((T + 1, D_MODEL), x.dtype)
    y, _ = lax.scan(block, y0, (block_e, row_tok.reshape(n_blocks, MOE_BLOCK),
                                row_w.reshape(n_blocks, MOE_BLOCK)))
    return y[:T].reshape(shape)


def setup_inputs(seed: int = 0) -> dict:
    key = jax.random.key(seed)
    ks = jax.random.split(key, 26)
    nrm = lambda k, s, sc: jax.random.normal(k, s, jnp.float32) * sc
    D, L_ = D_MODEL, DEPTH
    gate_base = jnp.concatenate([jnp.zeros((M_HEADS,)), jnp.full((M_HEADS,), F_BIAS_INIT),
                                 jnp.zeros((M_HEADS,)), jnp.full((M_HEADS,), F_BIAS_INIT)]).astype(jnp.float32)
    return {
        'x': nrm(ks[0], (BATCH, SEQ, D), 1.0),
        'c': nrm(ks[1], (BATCH, D), 1.0),
        'ctx': nrm(ks[2], (BATCH, CTX_LEN, D), 1.0),
        'c_ctx': nrm(ks[3], (D,), 1.0),
        'w_ada': nrm(ks[4], (L_, D, 6 * D), 0.5 * D ** -0.5),
        'b_ada': nrm(ks[5], (L_, 6 * D), 0.01),
        'norm1_w': 1.0 + nrm(ks[6], (L_, D), 0.01),
        'w_in': nrm(ks[7], (L_, D, IN_WIDTH), D ** -0.5),
        'gate_b': gate_base + nrm(ks[8], (L_, N_GATE), 0.1),
        'conv_w': nrm(ks[9], (L_, CONV_W, 2 * M_WIDTH), CONV_W ** -0.5),
        'conv_b': nrm(ks[10], (L_, 2 * M_WIDTH), 0.01),
        'w_pool': nrm(ks[11], (L_, POOL_GROUPS, POOL_GC, POOL_GC), POOL_GC ** -0.5),
        'pool_scale': 1.0 + nrm(ks[12], (L_, POOL_WIDTH), 0.1),
        'hnorm_w': 1.0 + nrm(ks[13], (L_, M_WIDTH), 0.01),
        'w_bp': nrm(ks[14], (L_, POOL_WIDTH, D), POOL_WIDTH ** -0.5),
        'w_bm': nrm(ks[15], (L_, M_WIDTH, D), M_WIDTH ** -0.5),
        'w_out': nrm(ks[16], (L_, D, D), D ** -0.5),
        'norm2_w': 1.0 + nrm(ks[17], (L_, D), 0.01),
        'w_router': nrm(ks[18], (L_, D, N_EXPERTS), D ** -0.5),
        'b_router': nrm(ks[19], (L_, N_EXPERTS), 0.01),
        'w1': nrm(ks[20], (L_, N_EXPERTS, D, 2 * D_FF), D ** -0.5),
        'b1': nrm(ks[21], (L_, N_EXPERTS, 2 * D_FF), 0.01),
        'w2': nrm(ks[22], (L_, N_EXPERTS, D_FF, D), D_FF ** -0.5),
        'b2': nrm(ks[23], (L_, N_EXPERTS, D), 0.01),
        'final_norm_w': 1.0 + nrm(ks[24], (D,), 0.01),
    }


def reference(x, c, ctx, c_ctx, w_ada, b_ada, norm1_w, w_in, gate_b, conv_w, conv_b, w_pool,
              pool_scale, hnorm_w, w_bp, w_bm, w_out, norm2_w, w_router, b_router, w1, b1, w2, b2,
              final_norm_w):
    B, L, _ = x.shape
    rows = L // GRID_W
    n_ctx = ctx.shape[1]
    sc = jax.nn.silu(c)
    scx = jax.nn.silu(c_ctx)
    for layer in range(DEPTH):
        mod = (sc @ w_ada[layer] + b_ada[layer])[:, None, :]
        mod_c = scx @ w_ada[layer] + b_ada[layer]
        sh1, s1, g1, sh2, s2, g2 = jnp.split(mod, 6, axis=-1)
        csh1, cs1, cg1, csh2, cs2, cg2 = jnp.split(mod_c, 6, axis=-1)

        p_lat = modulate(rmsnorm(x, norm1_w[layer]), sh1, s1) @ w_in[layer]
        p_ctx = modulate(rmsnorm(ctx, norm1_w[layer]), csh1, cs1) @ w_in[layer]
        ctx_in = mlstm_inputs(p_ctx, conv_w[layer], conv_b[layer], gate_b[layer])
        lat_in = mlstm_inputs(p_lat, conv_w[layer], conv_b[layer], gate_b[layer])
        z = zero_state(B)
        h_ctx, st_f, st_b = mlstm_bidir(*ctx_in, z, z)
        h_lat, _, _ = mlstm_bidir(*lat_in, st_f, st_b)
        pooled = centred_pool(p_lat[..., :POOL_WIDTH].reshape(B, rows, GRID_W, POOL_WIDTH))
        pooled = pooled.reshape(B, L, POOL_WIDTH)
        x = x + g1 * mixer_merge(p_lat, pooled, h_lat, w_pool[layer], pool_scale[layer], hnorm_w[layer],
                                 w_bp[layer], w_bm[layer], w_out[layer])
        if layer < DEPTH - 1:
            pooled_c = centred_pool(p_ctx[:, None, :, :POOL_WIDTH]).reshape(B, n_ctx, POOL_WIDTH)
            ctx = ctx + cg1 * mixer_merge(p_ctx, pooled_c, h_ctx, w_pool[layer], pool_scale[layer],
                                          hnorm_w[layer], w_bp[layer], w_bm[layer], w_out[layer])
            ctx = ctx + cg2 * moe(modulate(rmsnorm(ctx, norm2_w[layer]), csh2, cs2), w_router[layer],
                                  b_router[layer], w1[layer], b1[layer], w2[layer], b2[layer])

        x = x + g2 * moe(modulate(rmsnorm(x, norm2_w[layer]), sh2, s2), w_router[layer], b_router[layer],
                         w1[layer], b1[layer], w2[layer], b2[layer])
    return rmsnorm(x, final_norm_w)
```

```python
import functools

import numpy as np
import jax
import jax.numpy as jnp
from jax import lax
from jax.experimental import pallas as pl
from jax.experimental.pallas import tpu as pltpu

F32 = jnp.float32
BF16 = jnp.bfloat16
I32 = jnp.int32
HIGHEST = lax.Precision.HIGHEST

D_MODEL = 1024
EPS = 1e-6
GRID_W = 64
POOL_WINDOWS = (2, 4, 8, 16)
POOL_WIDTH = 512
POOL_GC = 128
M_HEADS = 4
M_HEAD_DIM = 128
M_WIDTH = 512
CHUNK = 128
N_GATE = 16
N_EXPERTS = 32
TOP_K = 4
D_FF = 1024
SWIGLU_LIMIT = 7.0
SWIGLU_ALPHA = 1.702

LANES = 128
PROJ_TILE = 256
ROUTE_TILE = 512
SEG_ALIGN = 16
SEG_PAD = TOP_K * ROUTE_TILE + N_EXPERTS * SEG_ALIGN
EXPERT_BLOCK = 512
XS_WIDTH = D_MODEL + LANES
SEG_SIZES = (512, 256, 128, 64, 32, 16)
VMEM_LIMIT = 56 * 1024 * 1024


def _dot(a, b, precision=None):
    return jnp.dot(a, b, preferred_element_type=F32, precision=precision)


def _dot_nt(a, b, precision=None):
    return lax.dot_general(a, b, (((1,), (1,)), ((), ())), preferred_element_type=F32,
                           precision=precision)


def _params(semantics):
    return pltpu.CompilerParams(dimension_semantics=semantics, vmem_limit_bytes=VMEM_LIMIT)


def _full(shape):
    nd = len(shape)
    return pl.BlockSpec(shape, lambda *_: (0,) * nd)


def _ada_kernel(c_ref, w_ref, b_ref, o_ref):
    c = c_ref[...]
    s = c * jax.nn.sigmoid(c)
    o_ref[...] = _dot(s, w_ref[...], HIGHEST) + b_ref[...]


def _ada(cc, w_ada, b_ada):
    rows, d = cc.shape
    n = w_ada.shape[1]
    tn = 1024
    return pl.pallas_call(
        _ada_kernel,
        grid=(n // tn,),
        in_specs=[pl.BlockSpec((rows, d), lambda j: (0, 0)),
                  pl.BlockSpec((d, tn), lambda j: (0, j)),
                  pl.BlockSpec((1, tn), lambda j: (0, j))],
        out_specs=pl.BlockSpec((rows, tn), lambda j: (0, j)),
        out_shape=jax.ShapeDtypeStruct((rows, n), F32),
        compiler_params=_params(("arbitrary",)),
        name="ada",
    )(cc, w_ada, b_ada.reshape(1, n))


def _pool_constants(tm):
    pos = np.arange(tm) % GRID_W
    row = np.arange(tm) // GRID_W
    pm = np.zeros((len(POOL_WINDOWS), tm, tm), np.float32)
    inv = np.zeros((tm, POOL_WIDTH), np.float32)
    for g, win in enumerate(POOL_WINDOWS):
        lo = np.clip(pos - win // 2, 0, GRID_W)
        hi = np.clip(pos + win // 2, 0, GRID_W)
        same = row[:, None] == row[None, :]
        pm[g] = (same & (pos[None, :] >= lo[:, None]) & (pos[None, :] < hi[:, None])).astype(np.float32)
        inv[:, g * POOL_GC:(g + 1) * POOL_GC] = (1.0 / (hi - lo).astype(np.float32))[:, None]
    t = np.arange(tm)
    same_chunk = (t[:, None] // CHUNK) == (t[None, :] // CHUNK)
    tl = (same_chunk & (t[None, :] <= t[:, None])).astype(np.float32)
    tu = (same_chunk & (t[None, :] >= t[:, None])).astype(np.float32)
    return pm, inv, tl, tu


def _inproj_kernel(x_ref, sh_ref, sc_ref, nw_ref, gbias_ref, wmain_ref, wgate_ref, wmerge_ref,
                   pmat_ref, invc_ref, tl_ref, tu_ref, wpool_ref, pscale_ref, wbp_ref,
                   qk_ref, v_ref, og_ref, col_ref, bra_ref, gbo_ref):
    x = x_ref[0]
    xn = x * lax.rsqrt(jnp.mean(x * x, axis=-1, keepdims=True) + EPS) * nw_ref[...]
    xm = xn * (1.0 + sc_ref[0]) + sh_ref[0]
    xb = xm.astype(BF16)

    g = _dot(xm, wgate_ref[...], HIGHEST) + gbias_ref[...]
    lf = jnp.minimum(g, 0.0) - jnp.log1p(jnp.exp(-jnp.abs(g)))
    b_pre = _dot(tl_ref[...], lf, HIGHEST)
    b_suf = _dot(tu_ref[...], lf, HIGHEST)
    lane = lax.broadcasted_iota(I32, g.shape, 1)
    col = jnp.where((lane >= 4) & (lane < 8), b_pre, jnp.where((lane >= 12) & (lane < 16), b_suf, g))
    col_ref[0] = col[:, :N_GATE]

    u = _dot(xb, wmain_ref[:, 0:POOL_WIDTH])
    invc = invc_ref[...]
    ya = []
    for gi in range(len(POOL_WINDOWS)):
        ug = u[:, gi * POOL_GC:(gi + 1) * POOL_GC]
        pooled = _dot(pmat_ref[gi], ug, HIGHEST) * invc[:, gi * POOL_GC:(gi + 1) * POOL_GC] - ug
        ya.append(_dot(pooled.astype(BF16), wpool_ref[gi]))
    ya = jnp.concatenate(ya, axis=1) * pscale_ref[...]
    ga = jax.nn.sigmoid(_dot(xb, wmerge_ref[:, 0:D_MODEL]))
    bra_ref[0] = (ga * _dot(ya.astype(BF16), wbp_ref[...])).astype(BF16)
    gbo_ref[0] = jax.nn.sigmoid(_dot(xb, wmerge_ref[:, D_MODEL:2 * D_MODEL])).astype(BF16)

    qk_ref[0] = _dot(xb, wmain_ref[:, POOL_WIDTH:POOL_WIDTH + 2 * M_WIDTH]).astype(BF16)
    v_ref[0] = _dot(xb, wmain_ref[:, POOL_WIDTH + 2 * M_WIDTH:POOL_WIDTH + 3 * M_WIDTH]).astype(BF16)
    og_ref[0] = jax.nn.sigmoid(
        _dot(xb, wmain_ref[:, POOL_WIDTH + 3 * M_WIDTH:POOL_WIDTH + 4 * M_WIDTH])).astype(BF16)


def _inproj(x, shift, scale, nw, gbias, wmain, wgate, wmerge, wpool, pscale, wbp):
    b, l, d = x.shape
    tm = min(PROJ_TILE, l)
    pm, inv, tl, tu = _pool_constants(tm)
    tok = lambda width: pl.BlockSpec((1, tm, width), lambda bi, i: (bi, i, 0))
    vec = pl.BlockSpec((1, 1, d), lambda bi, i: (bi, 0, 0))
    consts = [nw, gbias, wmain, wgate, wmerge, jnp.asarray(pm), jnp.asarray(inv), jnp.asarray(tl),
              jnp.asarray(tu), wpool, pscale, wbp]
    out_widths = (2 * M_WIDTH, M_WIDTH, M_WIDTH, N_GATE, d, d)
    out_dtypes = (BF16, BF16, BF16, F32, BF16, BF16)
    return pl.pallas_call(
        _inproj_kernel,
        grid=(b, l // tm),
        in_specs=[tok(d), vec, vec] + [_full(c.shape) for c in consts],
        out_specs=[tok(w) for w in out_widths],
        out_shape=[jax.ShapeDtypeStruct((b, l, w), dt) for w, dt in zip(out_widths, out_dtypes)],
        compiler_params=_params(("arbitrary", "arbitrary")),
        name="inproj",
    )(x, shift, scale, *consts)


def _conv_kernel(x_ref, prev_ref, next_ref, w_ref, b_ref, q_ref, k_ref):
    i = pl.program_id(1)
    last = pl.num_programs(1) - 1
    x = x_ref[0].astype(F32)
    tc = x.shape[0]
    prev_row = jnp.where(i > 0, prev_ref[0].astype(F32)[SEG_ALIGN - 1:SEG_ALIGN], 0.0)
    next_row = jnp.where(i < last, next_ref[0].astype(F32)[0:1], 0.0)
    rio = lax.broadcasted_iota(I32, x.shape, 0)
    x_prev = jnp.where(rio == 0, prev_row, pltpu.roll(x, 1, axis=0))
    x_next = jnp.where(rio == tc - 1, next_row, pltpu.roll(x, tc - 1, axis=0))
    w = w_ref[...]
    acc = b_ref[...] + x_prev * w[0:1] + x * w[1:2] + x_next * w[2:3]
    y = acc * jax.nn.sigmoid(acc)
    q_ref[0] = (y[:, :M_WIDTH] * (M_HEAD_DIM ** -0.5)).astype(BF16)
    k_ref[0] = y[:, M_WIDTH:].astype(BF16)


def _conv(qk_raw, conv_w, conv_b):
    b, l, c = qk_raw.shape
    tc = min(512, l)
    per = tc // SEG_ALIGN
    nblk = l // SEG_ALIGN
    return pl.pallas_call(
        _conv_kernel,
        grid=(b, l // tc),
        in_specs=[pl.BlockSpec((1, tc, c), lambda bi, i: (bi, i, 0)),
                  pl.BlockSpec((1, SEG_ALIGN, c), lambda bi, i: (bi, jnp.maximum(i * per - 1, 0), 0)),
                  pl.BlockSpec((1, SEG_ALIGN, c), lambda bi, i: (bi, jnp.minimum((i + 1) * per, nblk - 1), 0)),
                  _full(conv_w.shape), _full((1, c))],
        out_specs=[pl.BlockSpec((1, tc, M_WIDTH), lambda bi, i: (bi, i, 0))] * 2,
        out_shape=[jax.ShapeDtypeStruct((b, l, M_WIDTH), BF16)] * 2,
        compiler_params=_params(("arbitrary", "arbitrary")),
        name="conv",
    )(qk_raw, qk_raw, qk_raw, conv_w, conv_b.reshape(1, c))


def _state_update(k, v, i_col, b_col, btot, c_st, n_st, m_st):
    log_a = btot - b_col + i_col
    m_new = jnp.maximum(btot + m_st, jnp.max(log_a, axis=0, keepdims=True))
    a = jnp.exp(log_a - m_new)
    cd = jnp.exp(btot + m_st - m_new)
    ak = a * k.astype(F32)
    c_new = cd * c_st + _dot(ak.T.astype(BF16), v)
    n_new = cd * n_st + jnp.sum(ak, axis=0, keepdims=True)
    return c_new, n_new, m_new


def _chunk_out(q, k, v, i_row, b_row, b_col, mask, c_st, n_st, m_st):
    logw = jnp.where(mask, b_col - b_row + i_row, -jnp.inf)
    m_inter = b_col + m_st
    m_t = jnp.maximum(jnp.max(logw, axis=1, keepdims=True), m_inter)
    s = _dot_nt(q, k) * jnp.exp(logw - m_t)
    decay = jnp.exp(m_inter - m_t)
    num = _dot(s.astype(BF16), v) + decay * _dot(q, c_st.astype(BF16))
    den = jnp.sum(s, axis=1, keepdims=True) + decay * jnp.sum(q.astype(F32) * n_st, axis=1, keepdims=True)
    return num / jnp.maximum(jnp.abs(den), jnp.exp(-m_t))


def _mlstm_kernel(q_ref, k_ref, v_ref, col_ref, row_ref, qc_ref, kc_ref, vc_ref, colc_ref, rowc_ref,
                  out_ref, hacc_ref):
    del qc_ref
    nc = q_ref.shape[1]
    ncc = kc_ref.shape[1]
    t = CHUNK
    r = lax.broadcasted_iota(I32, (t, t), 0)
    c = lax.broadcasted_iota(I32, (t, t), 1)
    tril = c <= r
    triu = c >= r

    zero = (jnp.zeros((t, t), F32), jnp.zeros((1, t), F32), jnp.zeros((1, 1), F32))
    st_f = zero
    for ci in range(ncc):
        col = colc_ref[0, 0, ci]
        st_f = _state_update(kc_ref[0, ci], vc_ref[0, ci], col[:, 0:1], col[:, 1:2],
                             col[t - 1:t, 1:2], *st_f)
    st_b = zero
    for ci in reversed(range(ncc)):
        col = colc_ref[0, 0, ci]
        st_b = _state_update(kc_ref[0, ci], vc_ref[0, ci], col[:, 2:3], col[:, 3:4],
                             col[0:1, 3:4], *st_b)

    hacc_ref[...] = jnp.zeros(hacc_ref.shape, F32)

    def body(j, carry):
        cf, nf, mf, cb, nb, mb = carry
        jf = j
        q, k, v = q_ref[0, jf], k_ref[0, jf], v_ref[0, jf]
        col, row = col_ref[0, 0, jf], row_ref[0, 0, jf]
        hacc_ref[jf] += _chunk_out(q, k, v, row[0:1], row[1:2], col[:, 1:2], tril, cf, nf, mf)
        cf, nf, mf = _state_update(k, v, col[:, 0:1], col[:, 1:2], col[t - 1:t, 1:2], cf, nf, mf)
        jb = nc - 1 - j
        q, k, v = q_ref[0, jb], k_ref[0, jb], v_ref[0, jb]
        col, row = col_ref[0, 0, jb], row_ref[0, 0, jb]
        hacc_ref[jb] += _chunk_out(q, k, v, row[2:3], row[3:4], col[:, 3:4], triu, cb, nb, mb)
        cb, nb, mb = _state_update(k, v, col[:, 2:3], col[:, 3:4], col[0:1, 3:4], cb, nb, mb)
        return cf, nf, mf, cb, nb, mb

    lax.fori_loop(0, nc, body, st_f + st_b)

    def norm_body(j, _):
        h = hacc_ref[j]
        out_ref[0, j] = (h * lax.rsqrt(jnp.mean(h * h, axis=-1, keepdims=True) + EPS)).astype(BF16)
        return 0

    lax.fori_loop(0, nc, norm_body, 0)


def _per_head_gates(col):
    b, l, _ = col.shape
    nc = l // CHUNK
    c5 = col.reshape(b, nc, CHUNK, 4, M_HEADS)
    return c5.transpose(0, 4, 1, 2, 3), c5.transpose(0, 4, 1, 3, 2)


def _mlstm(q, k, v, col, qc, kc, vc, colc):
    b, l, _ = q.shape
    lc = qc.shape[1]
    nc, ncc = l // CHUNK, lc // CHUNK
    chunked = lambda a, n: a.reshape(b, n, CHUNK, M_WIDTH)
    colh, rowh = _per_head_gates(col)
    colch, rowch = _per_head_gates(colc)
    seq = lambda n: pl.BlockSpec((1, n, CHUNK, M_HEAD_DIM), lambda bi, h: (bi, 0, 0, h))
    colspec = lambda n: pl.BlockSpec((1, 1, n, CHUNK, 4), lambda bi, h: (bi, h, 0, 0, 0))
    rowspec = lambda n: pl.BlockSpec((1, 1, n, 4, CHUNK), lambda bi, h: (bi, h, 0, 0, 0))
    out = pl.pallas_call(
        _mlstm_kernel,
        grid=(b, M_HEADS),
        in_specs=[seq(nc), seq(nc), seq(nc), colspec(nc), rowspec(nc),
                  seq(ncc), seq(ncc), seq(ncc), colspec(ncc), rowspec(ncc)],
        out_specs=seq(nc),
        out_shape=jax.ShapeDtypeStruct((b, nc, CHUNK, M_WIDTH), BF16),
        scratch_shapes=[pltpu.VMEM((nc, CHUNK, M_HEAD_DIM), F32)],
        compiler_params=_params(("arbitrary", "arbitrary")),
        name="mlstm",
    )(chunked(q, nc), chunked(k, nc), chunked(v, nc), colh, rowh,
      chunked(qc, ncc), chunked(kc, ncc), chunked(vc, ncc), colch, rowch)
    return out.reshape(b, l, M_WIDTH)


def _merge_kernel(x_ref, hn_ref, og_ref, bra_ref, gb_ref, g1_ref, hw_ref, wbm_ref, wout_ref, o_ref):
    yb = hn_ref[0].astype(F32) * hw_ref[...] * og_ref[0].astype(F32)
    y = bra_ref[0].astype(F32) + gb_ref[0].astype(F32) * _dot(yb.astype(BF16), wbm_ref[...])
    o_ref[0] = x_ref[0] + g1_ref[0] * _dot(y.astype(BF16), wout_ref[...])


def _merge(x, hn, og, bra, gb, g1, hnorm_w, wbm, wout):
    b, l, d = x.shape
    tm = min(PROJ_TILE, l)
    tok = lambda width: pl.BlockSpec((1, tm, width), lambda bi, i: (bi, i, 0))
    return pl.pallas_call(
        _merge_kernel,
        grid=(b, l // tm),
        in_specs=[tok(d), tok(M_WIDTH), tok(M_WIDTH), tok(d), tok(d),
                  pl.BlockSpec((1, 1, d), lambda bi, i: (bi, 0, 0)),
                  _full((1, M_WIDTH)), _full(wbm.shape), _full(wout.shape)],
        out_specs=tok(d),
        out_shape=jax.ShapeDtypeStruct((b, l, d), F32),
        compiler_params=_params(("arbitrary", "arbitrary")),
        name="merge",
    )(x, hn, og, bra, gb, g1, hnorm_w.reshape(1, M_WIDTH), wbm, wout)


def _router_kernel(x_ref, sh_ref, sc_ref, nw_ref, wrt_ref, br_ref, ustrict_ref, lstrict_ref,
                   xn_ref, slot_ref, wts_ref, cnt_ref):
    x = x_ref[0]
    tm = x.shape[0]
    xn = x * lax.rsqrt(jnp.mean(x * x, axis=-1, keepdims=True) + EPS) * nw_ref[...]
    xm = xn * (1.0 + sc_ref[0]) + sh_ref[0]
    xn_ref[0] = xm.astype(BF16)

    logits = _dot_nt(wrt_ref[...], xm, HIGHEST) + br_ref[...]
    eio = lax.broadcasted_iota(I32, logits.shape, 0).astype(F32)
    rest = logits
    onehots, vals = [], []
    for _ in range(TOP_K):
        mx = jnp.max(rest, axis=0, keepdims=True)
        idx = jnp.min(jnp.where(rest == mx, eio, float(N_EXPERTS)), axis=0, keepdims=True)
        oh = eio == idx
        onehots.append(oh)
        vals.append(mx)
        rest = jnp.where(oh, -jnp.inf, rest)
    exps = [jnp.exp(vk - vals[0]) for vk in vals]
    denom = exps[0] + exps[1] + exps[2] + exps[3]

    oh_all = jnp.zeros(logits.shape, F32)
    for oh in onehots:
        oh_all = oh_all + oh.astype(F32)
    cnt = jnp.sum(oh_all, axis=1, keepdims=True)
    n_al = jnp.ceil(cnt * (1.0 / SEG_ALIGN)) * SEG_ALIGN
    seg_off = _dot(lstrict_ref[...], jnp.broadcast_to(n_al, (N_EXPERTS, LANES)), HIGHEST)[:, 0:1]
    rank = _dot(oh_all.astype(BF16), ustrict_ref[...])
    base = seg_off + rank
    for kk in range(TOP_K):
        slot = jnp.sum(jnp.where(onehots[kk], base, 0.0), axis=0, keepdims=True)
        slot_ref[0, kk:kk + 1, :] = slot.astype(I32)
        wts_ref[0, kk:kk + 1, :] = exps[kk] / denom
    slot_ref[0, TOP_K:, :] = jnp.full((8 - TOP_K, tm), -1, I32)
    wts_ref[0, TOP_K:, :] = jnp.zeros((8 - TOP_K, tm), F32)
    cnt_ref[0] = jnp.broadcast_to(cnt, (N_EXPERTS, LANES))


def _router(x2, shift, scale, nw, w_router, b_router):
    b, l, d = x2.shape
    tm = ROUTE_TILE
    nl = l // tm
    nt = b * nl
    s = np.arange(tm)
    ustrict = jnp.asarray((s[:, None] < s[None, :]).astype(np.float32), BF16)
    e = np.arange(N_EXPERTS)
    lstrict = jnp.asarray((e[None, :] < e[:, None]).astype(np.float32))
    vec = pl.BlockSpec((1, 1, d), lambda bi, i: (bi, 0, 0))
    tile = lambda r, c: pl.BlockSpec((1, r, c), lambda bi, i: (bi * nl + i, 0, 0))
    return pl.pallas_call(
        _router_kernel,
        grid=(b, nl),
        in_specs=[pl.BlockSpec((1, tm, d), lambda bi, i: (bi, i, 0)), vec, vec, _full((1, d)),
                  _full((N_EXPERTS, d)), _full((N_EXPERTS, 1)), _full((tm, tm)),
                  _full((N_EXPERTS, N_EXPERTS))],
        out_specs=[pl.BlockSpec((1, tm, d), lambda bi, i: (bi, i, 0)),
                   tile(8, tm), tile(8, tm), tile(N_EXPERTS, LANES)],
        out_shape=[jax.ShapeDtypeStruct((b, l, d), BF16),
                   jax.ShapeDtypeStruct((nt, 8, tm), I32),
                   jax.ShapeDtypeStruct((nt, 8, tm), F32),
                   jax.ShapeDtypeStruct((nt, N_EXPERTS, LANES), F32)],
        compiler_params=_params(("arbitrary", "arbitrary")),
        name="router",
    )(x2, shift, scale, nw, w_router.T, b_router.reshape(N_EXPERTS, 1), ustrict, lstrict)


def _segment_copies(tile, n_s, off_s, dst_s, make_copy, action):
    def body(e, _):
        idx = tile * N_EXPERTS + e
        n, seg, dst = n_s[idx], off_s[idx], dst_s[idx]
        pos = jnp.int32(0)
        for sz in SEG_SIZES:
            take = (n & sz) != 0

            @pl.when(take)
            def _(pos=pos, sz=sz):
                action(make_copy(pl.multiple_of(seg + pos, SEG_ALIGN),
                                 pl.multiple_of(dst + pos, SEG_ALIGN), sz))

            pos = pos + jnp.where(take, sz, 0)
        return 0

    lax.fori_loop(0, N_EXPERTS, body, 0)


def _zero_fill(tail_s, xs_hbm, zero_ref, sem, action):
    def make_copy(dst, sz):
        return pltpu.make_async_copy(zero_ref.at[pl.ds(0, sz)], xs_hbm.at[pl.ds(dst, sz)], sem)

    def tail_body(e, _):
        start, n = tail_s[e], tail_s[N_EXPERTS + e]
        pos = jnp.int32(0)
        for sz in SEG_SIZES[1:]:
            take = (n & sz) != 0

            @pl.when(take)
            def _(pos=pos, sz=sz):
                action(make_copy(pl.multiple_of(start + pos, SEG_ALIGN), sz))

            pos = pos + jnp.where(take, sz, 0)
        return 0

    lax.fori_loop(0, N_EXPERTS, tail_body, 0)

    def block_body(blk, _):
        action(make_copy(pl.multiple_of(blk * EXPERT_BLOCK, EXPERT_BLOCK), EXPERT_BLOCK))
        return 0

    lax.fori_loop(tail_s[2 * N_EXPERTS], tail_s[2 * N_EXPERTS + 1], block_body, 0)


def _dispatch_kernel(n_s, off_s, dst_s, tail_s, xn_ref, slot_ref, wts_ref, xs_hbm, g_ref, zero_ref, sem):
    tile = pl.program_id(0)
    x = xn_ref[...]
    tm = x.shape[0]
    slot = slot_ref[0]
    wts = wts_ref[0]
    rc = 256
    lane = lax.broadcasted_iota(I32, (rc, LANES), 1)
    for ci in range(SEG_PAD // rc):
        rio = lax.broadcasted_iota(I32, (rc, tm), 0) + ci * rc
        sel = jnp.zeros((rc, tm), F32)
        wsel = jnp.zeros((rc, tm), F32)
        for kk in range(TOP_K):
            hit = rio == slot[kk:kk + 1, :]
            sel = jnp.where(hit, 1.0, sel)
            wsel = jnp.where(hit, wts[kk:kk + 1, :], wsel)
        g_ref[ci * rc:(ci + 1) * rc, 0:D_MODEL] = _dot(sel.astype(BF16), x).astype(BF16)
        w_row = jnp.sum(wsel, axis=1, keepdims=True)
        w_hi = w_row.astype(BF16).astype(F32)
        g_ref[ci * rc:(ci + 1) * rc, D_MODEL:XS_WIDTH] = jnp.where(lane < LANES // 2, w_hi, w_row - w_hi).astype(BF16)

    def make_copy(src, dst, sz):
        return pltpu.make_async_copy(g_ref.at[pl.ds(src, sz)], xs_hbm.at[pl.ds(dst, sz)], sem)

    _segment_copies(tile, n_s, off_s, dst_s, make_copy, lambda cp: cp.start())
    _segment_copies(tile, n_s, off_s, dst_s, make_copy, lambda cp: cp.wait())

    @pl.when(tile == pl.num_programs(0) - 1)
    def _():
        zero_ref[...] = jnp.zeros(zero_ref.shape, BF16)
        _zero_fill(tail_s, xs_hbm, zero_ref, sem, lambda cp: cp.start())
        _zero_fill(tail_s, xs_hbm, zero_ref, sem, lambda cp: cp.wait())


def _dispatch(xn2, slot, wts, n_flat, off_flat, dst_flat, tail_flat, rows_total):
    t, d = xn2.shape
    tm = ROUTE_TILE
    nt = t // tm
    return pl.pallas_call(
        _dispatch_kernel,
        grid_spec=pltpu.PrefetchScalarGridSpec(
            num_scalar_prefetch=4,
            grid=(nt,),
            in_specs=[pl.BlockSpec((tm, d), lambda i, *_: (i, 0)),
                      pl.BlockSpec((1, 8, tm), lambda i, *_: (i, 0, 0)),
                      pl.BlockSpec((1, 8, tm), lambda i, *_: (i, 0, 0))],
            out_specs=pl.BlockSpec(memory_space=pl.ANY),
            scratch_shapes=[pltpu.VMEM((SEG_PAD, XS_WIDTH), BF16),
                            pltpu.VMEM((EXPERT_BLOCK, XS_WIDTH), BF16),
                            pltpu.SemaphoreType.DMA(())]),
        out_shape=jax.ShapeDtypeStruct((rows_total, XS_WIDTH), BF16),
        compiler_params=_params(("arbitrary",)),
        name="dispatch",
    )(n_flat, off_flat, dst_flat, tail_flat, xn2, slot, wts)


def _combine_kernel(n_s, off_s, dst_s, out_hbm, slott_ref, x2_ref, g2_ref, fw_ref, y_ref, buf_ref, sem):
    tile = pl.program_id(0)

    @pl.when(tile == 0)
    def _():
        buf_ref[...] = jnp.zeros(buf_ref.shape, BF16)

    def make_copy(seg, src, sz):
        return pltpu.make_async_copy(out_hbm.at[pl.ds(src, sz)], buf_ref.at[pl.ds(seg, sz)], sem)

    _segment_copies(tile, n_s, off_s, dst_s, make_copy, lambda cp: cp.start())
    _segment_copies(tile, n_s, off_s, dst_s, make_copy, lambda cp: cp.wait())

    st = slott_ref[...]
    tm = st.shape[0]
    kc = 512
    acc = jnp.zeros((tm, D_MODEL), F32)
    for ci in range(SEG_PAD // kc):
        lio = lax.broadcasted_iota(I32, (tm, kc), 1) + ci * kc
        sel = jnp.zeros((tm, kc), F32)
        for kk in range(TOP_K):
            sel = jnp.where(lio == st[:, kk:kk + 1], 1.0, sel)
        acc = acc + _dot(sel.astype(BF16), buf_ref[ci * kc:(ci + 1) * kc, :])
    x3 = x2_ref[...] + g2_ref[0] * acc
    y_ref[...] = x3 * lax.rsqrt(jnp.mean(x3 * x3, axis=-1, keepdims=True) + EPS) * fw_ref[...]


def _combine(out_rows, slot_t, x2_flat, g2, final_w, n_flat, off_flat, dst_flat, tiles_per_batch):
    t, d = x2_flat.shape
    tm = ROUTE_TILE
    nt = t // tm
    return pl.pallas_call(
        _combine_kernel,
        grid_spec=pltpu.PrefetchScalarGridSpec(
            num_scalar_prefetch=3,
            grid=(nt,),
            in_specs=[pl.BlockSpec(memory_space=pl.ANY),
                      pl.BlockSpec((tm, 8), lambda i, *_: (i, 0)),
                      pl.BlockSpec((tm, d), lambda i, *_: (i, 0)),
                      pl.BlockSpec((1, 1, d), lambda i, *_: (i // tiles_per_batch, 0, 0)),
                      pl.BlockSpec((1, d), lambda i, *_: (0, 0))],
            out_specs=pl.BlockSpec((tm, d), lambda i, *_: (i, 0)),
            scratch_shapes=[pltpu.VMEM((SEG_PAD, D_MODEL), BF16), pltpu.SemaphoreType.DMA(())]),
        out_shape=jax.ShapeDtypeStruct((t, d), F32),
        compiler_params=_params(("arbitrary",)),
        name="combine",
    )(n_flat, off_flat, dst_flat, out_rows, slot_t, x2_flat, g2, final_w.reshape(1, d))


def _expert_kernel(blk_e, nb_used, xs_ref, w1_ref, b1_ref, w2_ref, b2_ref, o_ref):
    del blk_e

    @pl.when(pl.program_id(0) >= nb_used[0])
    def _():
        o_ref[...] = jnp.zeros(o_ref.shape, BF16)

    @pl.when(pl.program_id(0) < nb_used[0])
    def _():
        x = xs_ref[...]
        w_row = (x[:, D_MODEL:D_MODEL + 1].astype(F32)
                 + x[:, D_MODEL + LANES // 2:D_MODEL + LANES // 2 + 1].astype(F32))
        gu = _dot(x[:, :D_MODEL], w1_ref[0]) + b1_ref[0]
        gate = jnp.minimum(gu[:, :D_FF], SWIGLU_LIMIT)
        up = jnp.clip(gu[:, D_FF:], -SWIGLU_LIMIT, SWIGLU_LIMIT)
        act = (up + 1.0) * gate * jax.nn.sigmoid(SWIGLU_ALPHA * gate)
        o = _dot(act.astype(BF16), w2_ref[0]) + b2_ref[0]
        o_ref[...] = (o * w_row).astype(BF16)


def _expert(xs, blk_e, nb_used, w1, b1, w2, b2):
    rows = xs.shape[0]
    nb = rows // EXPERT_BLOCK
    row_blk = lambda i, be, nu: (jnp.minimum(i, nu[0] - 1), 0)
    per_e = lambda i, be, nu: (be[i], 0, 0)
    return pl.pallas_call(
        _expert_kernel,
        grid_spec=pltpu.PrefetchScalarGridSpec(
            num_scalar_prefetch=2,
            grid=(nb,),
            in_specs=[pl.BlockSpec((EXPERT_BLOCK, XS_WIDTH), row_blk),
                      pl.BlockSpec((1, D_MODEL, 2 * D_FF), per_e),
                      pl.BlockSpec((1, 1, 2 * D_FF), per_e),
                      pl.BlockSpec((1, D_FF, D_MODEL), per_e),
                      pl.BlockSpec((1, 1, D_MODEL), per_e)],
            out_specs=pl.BlockSpec((EXPERT_BLOCK, D_MODEL), lambda i, be, nu: (i, 0))),
        out_shape=jax.ShapeDtypeStruct((rows, D_MODEL), BF16),
        compiler_params=_params(("arbitrary",)),
        name="expert",
    )(blk_e, nb_used, xs, w1, b1.reshape(N_EXPERTS, 1, 2 * D_FF), w2, b2.reshape(N_EXPERTS, 1, D_MODEL))


def _routing_tables(cnt, nb):
    cnt = cnt.astype(I32)
    n_al = (cnt + SEG_ALIGN - 1) // SEG_ALIGN * SEG_ALIGN
    seg_off = jnp.cumsum(n_al, axis=1) - n_al
    rel = jnp.cumsum(n_al, axis=0) - n_al
    tot = jnp.sum(n_al, axis=0)
    blocks_e = (tot + EXPERT_BLOCK - 1) // EXPERT_BLOCK
    blk_end = jnp.cumsum(blocks_e)
    e_start = (blk_end - blocks_e) * EXPERT_BLOCK
    dst = e_start[None, :] + rel
    blk_e = jnp.minimum(jnp.searchsorted(blk_end, jnp.arange(nb, dtype=I32), side="right"),
                        N_EXPERTS - 1).astype(I32)
    nb_used = blk_end[-1:].astype(I32)
    tails = jnp.concatenate([e_start + tot, blocks_e * EXPERT_BLOCK - tot, nb_used,
                             jnp.full((1,), nb, I32)]).astype(I32)
    return (n_al.reshape(-1), seg_off.reshape(-1).astype(I32), dst.reshape(-1).astype(I32), blk_e,
            nb_used, tails)


def kernel(x, c, ctx, c_ctx, w_ada, b_ada, norm1_w, w_in, gate_b, conv_w, conv_b, w_pool, pool_scale,
           hnorm_w, w_bp, w_bm, w_out, norm2_w, w_router, b_router, w1, b1, w2, b2, final_norm_w):
    assert w_ada.shape[0] == 1, "single-layer kernel"
    b, l, d = x.shape
    lc = ctx.shape[1]
    assert d == D_MODEL and l % ROUTE_TILE == 0 and l % GRID_W == 0 and lc % CHUNK == 0
    t = b * l

    rows = (b + 1 + 7) // 8 * 8
    cc = jnp.zeros((rows, d), F32).at[:b].set(c).at[b].set(c_ctx)
    mod = _ada(cc, w_ada[0], b_ada[0])
    sh1, s1, g1, sh2, s2, g2 = [mod[:b, i * d:(i + 1) * d].reshape(b, 1, d) for i in range(6)]
    csh1, cs1 = [jnp.broadcast_to(mod[b, i * d:(i + 1) * d].reshape(1, 1, d), (b, 1, d)) for i in range(2)]

    w_in0 = w_in[0]
    off_gate = POOL_WIDTH + 4 * M_WIDTH
    wmain = w_in0[:, :off_gate].astype(BF16)
    wgate = jnp.pad(w_in0[:, off_gate:off_gate + N_GATE], ((0, 0), (0, LANES - N_GATE)))
    wmerge = w_in0[:, off_gate + N_GATE:].astype(BF16)
    gbias = jnp.pad(gate_b[0], (0, LANES - N_GATE)).reshape(1, LANES)
    proj_consts = (norm1_w[0].reshape(1, d), gbias, wmain, wgate, wmerge, w_pool[0].astype(BF16),
                   pool_scale[0].reshape(1, POOL_WIDTH), w_bp[0].astype(BF16))

    qk_raw, v, og, col, bra, gb = _inproj(x, sh1, s1, *proj_consts)
    qk_raw_c, v_c, _, col_c, _, _ = _inproj(ctx, csh1, cs1, *proj_consts)
    q, k = _conv(qk_raw, conv_w[0], conv_b[0])
    q_c, k_c = _conv(qk_raw_c, conv_w[0], conv_b[0])
    hn = _mlstm(q, k, v, col, q_c, k_c, v_c, col_c)
    x2 = _merge(x, hn, og, bra, gb, g1, hnorm_w[0], w_bm[0].astype(BF16), w_out[0].astype(BF16))

    xn2, slot, wts, cnt = _router(x2, sh2, s2, norm2_w[0].reshape(1, d), w_router[0], b_router[0])
    nt = t // ROUTE_TILE
    nb = (t * TOP_K + nt * N_EXPERTS * (SEG_ALIGN - 1)) // EXPERT_BLOCK + N_EXPERTS
    n_flat, off_flat, dst_flat, blk_e, nb_used, tails = _routing_tables(cnt[:, :, 0], nb)
    xs = _dispatch(xn2.reshape(t, d), slot, wts, n_flat, off_flat, dst_flat, tails, nb * EXPERT_BLOCK)
    out_rows = _expert(xs, blk_e, nb_used, w1[0].astype(BF16), b1[0], w2[0].astype(BF16), b2[0])
    slot_t = slot.transpose(0, 2, 1).reshape(t, 8)
    y = _combine(out_rows, slot_t, x2.reshape(t, d), g2, final_norm_w, n_flat, off_flat, dst_flat,
                 l // ROUTE_TILE)
    return y.reshape(b, l, d)
```

```python
import functools

import numpy as np
import jax
import jax.numpy as jnp
from jax import lax
from jax.experimental import pallas as pl
from jax.experimental.pallas import tpu as pltpu

F32 = jnp.float32
BF16 = jnp.bfloat16
I32 = jnp.int32
HIGHEST = lax.Precision.HIGHEST

D_MODEL = 1024
EPS = 1e-6
GRID_W = 64
POOL_WINDOWS = (2, 4, 8, 16)
POOL_WIDTH = 512
POOL_GC = 128
M_HEADS = 4
M_HEAD_DIM = 128
M_WIDTH = 512
CHUNK = 128
N_GATE = 16
N_EXPERTS = 32
TOP_K = 4
D_FF = 1024
SWIGLU_LIMIT = 7.0
SWIGLU_ALPHA = 1.702

LANES = 128
PROJ_TILE = 512
ROUTE_TILE = 512
SEG_ALIGN = 16
SEG_PAD = TOP_K * ROUTE_TILE + N_EXPERTS * SEG_ALIGN
EXPERT_BLOCK = 512
XS_WIDTH = D_MODEL + LANES
SEG_SIZES = (512, 256, 128, 64, 32, 16)
VMEM_LIMIT = 56 * 1024 * 1024


def _dot(a, b, precision=None):
    return jnp.dot(a, b, preferred_element_type=F32, precision=precision)


def _dot_nt(a, b, precision=None):
    return lax.dot_general(a, b, (((1,), (1,)), ((), ())), preferred_element_type=F32,
                           precision=precision)


def _params(semantics):
    return pltpu.CompilerParams(dimension_semantics=semantics, vmem_limit_bytes=VMEM_LIMIT)


def _full(shape):
    nd = len(shape)
    return pl.BlockSpec(shape, lambda *_: (0,) * nd, pipeline_mode=pl.Buffered(1))


def _split_bf16(a):
    hi = a.astype(BF16)
    return hi, (a - hi.astype(F32)).astype(BF16)


def _ada_kernel(c_ref, w_ref, b_ref, o_ref):
    c = c_ref[...]
    s = c * jax.nn.sigmoid(c)
    o_ref[...] = _dot(s, w_ref[...], HIGHEST) + b_ref[...]


def _ada(cc, w_ada, b_ada):
    rows, d = cc.shape
    n = w_ada.shape[1]
    tn = 1024
    return pl.pallas_call(
        _ada_kernel,
        grid=(n // tn,),
        in_specs=[pl.BlockSpec((rows, d), lambda j: (0, 0)),
                  pl.BlockSpec((d, tn), lambda j: (0, j)),
                  pl.BlockSpec((1, tn), lambda j: (0, j))],
        out_specs=pl.BlockSpec((rows, tn), lambda j: (0, j)),
        out_shape=jax.ShapeDtypeStruct((rows, n), F32),
        compiler_params=_params(("arbitrary",)),
        name="ada",
    )(cc, w_ada, b_ada.reshape(1, n))


def _pool_constants(tm):
    pos = np.arange(tm) % GRID_W
    row = np.arange(tm) // GRID_W
    pm = np.zeros((len(POOL_WINDOWS), tm, tm), np.float32)
    inv = np.zeros((tm, POOL_WIDTH), np.float32)
    for g, win in enumerate(POOL_WINDOWS):
        lo = np.clip(pos - win // 2, 0, GRID_W)
        hi = np.clip(pos + win // 2, 0, GRID_W)
        same = row[:, None] == row[None, :]
        pm[g] = (same & (pos[None, :] >= lo[:, None]) & (pos[None, :] < hi[:, None])).astype(np.float32)
        inv[:, g * POOL_GC:(g + 1) * POOL_GC] = (1.0 / (hi - lo).astype(np.float32))[:, None]
    t = np.arange(tm)
    same_chunk = (t[:, None] // CHUNK) == (t[None, :] // CHUNK)
    tl = (same_chunk & (t[None, :] <= t[:, None])).astype(np.float32)
    tu = (same_chunk & (t[None, :] >= t[:, None])).astype(np.float32)
    return pm, inv, tl, tu


def _inproj_kernel(x_ref, sh_ref, sc_ref, nw_ref, gbias_ref, wmain_ref, wgate_ref, wmerge_ref,
                   pmat_ref, invc_ref, tl_ref, tu_ref, wpool_ref, pscale_ref, wbp_ref,
                   qk_ref, vt_ref, og_ref, row_ref, bra_ref, gbo_ref):
    x = x_ref[0]
    xn = x * lax.rsqrt(jnp.mean(x * x, axis=-1, keepdims=True) + EPS) * nw_ref[...]
    xm = xn * (1.0 + sc_ref[0]) + sh_ref[0]
    xb = xm.astype(BF16)

    xlo = (xm - xb.astype(F32)).astype(BF16)
    wg = wgate_ref[...]
    r_hi = _dot(xb, wg)
    g = r_hi + pltpu.roll(r_hi, LANES - N_GATE, axis=1) + _dot(xlo, wg) + gbias_ref[...]
    lf = jnp.minimum(g, 0.0) - jnp.log1p(jnp.exp(-jnp.abs(g)))
    lane = lax.broadcasted_iota(I32, g.shape, 1)
    lf_hi = lf.astype(BF16).astype(F32)
    lf2 = jnp.where(lane < N_GATE, lf_hi, pltpu.roll(lf - lf_hi, N_GATE, axis=1)).astype(BF16)
    r_pre = _dot(tl_ref[...], lf2)
    r_suf = _dot(tu_ref[...], lf2)
    b_pre = r_pre + pltpu.roll(r_pre, LANES - N_GATE, axis=1)
    b_suf = r_suf + pltpu.roll(r_suf, LANES - N_GATE, axis=1)
    kind = (lane & 7) >> 1
    col = jnp.where(kind == 2, b_pre, jnp.where(kind == 3, b_suf, g))
    row_ref[0] = col.T[:N_GATE]

    u = _dot(xb, wmain_ref[:, 0:POOL_WIDTH])
    u_hi, u_lo = _split_bf16(u)
    invc = invc_ref[...]
    ya = []
    for gi in range(len(POOL_WINDOWS)):
        cols = slice(gi * POOL_GC, (gi + 1) * POOL_GC)
        win_sum = _dot(pmat_ref[gi], u_hi[:, cols]) + _dot(pmat_ref[gi], u_lo[:, cols])
        pooled = win_sum * invc[:, cols] - u[:, cols]
        ya.append(_dot(pooled.astype(BF16), wpool_ref[gi]))
    ya = jnp.concatenate(ya, axis=1) * pscale_ref[...]
    ga = jax.nn.sigmoid(_dot(xb, wmerge_ref[:, 0:D_MODEL]))
    bra_ref[0] = (ga * _dot(ya.astype(BF16), wbp_ref[...])).astype(BF16)
    gbo_ref[0] = jax.nn.sigmoid(_dot(xb, wmerge_ref[:, D_MODEL:2 * D_MODEL])).astype(BF16)

    qk_ref[0] = _dot(xb, wmain_ref[:, POOL_WIDTH:POOL_WIDTH + 2 * M_WIDTH]).astype(BF16)
    v = _dot(xb, wmain_ref[:, POOL_WIDTH + 2 * M_WIDTH:POOL_WIDTH + 3 * M_WIDTH])
    for ci in range(v.shape[0] // CHUNK):
        vt_ref[0, ci] = v[ci * CHUNK:(ci + 1) * CHUNK].T.astype(BF16)
    og_ref[0] = jax.nn.sigmoid(
        _dot(xb, wmain_ref[:, POOL_WIDTH + 3 * M_WIDTH:POOL_WIDTH + 4 * M_WIDTH])).astype(BF16)


def _chunk_t_out(b, l, tm):
    return (pl.BlockSpec((1, tm // CHUNK, M_WIDTH, CHUNK), lambda bi, i: (bi, i, 0, 0)),
            jax.ShapeDtypeStruct((b, l // CHUNK, M_WIDTH, CHUNK), BF16))


def _inproj(x, shift, scale, nw, gbias, wmain, wgate, wmerge, wpool, pscale, wbp):
    b, l, d = x.shape
    tm = min(PROJ_TILE, l)
    pm, inv, tl, tu = _pool_constants(tm)
    tok = lambda width: pl.BlockSpec((1, tm, width), lambda bi, i: (bi, i, 0))
    vec = pl.BlockSpec((1, 1, d), lambda bi, i: (bi, 0, 0))
    consts = [nw, gbias, wmain, wgate, wmerge, jnp.asarray(pm, BF16), jnp.asarray(inv),
              jnp.asarray(tl, BF16), jnp.asarray(tu, BF16), wpool, pscale, wbp]
    tok_out = lambda width: (tok(width), jax.ShapeDtypeStruct((b, l, width), BF16))
    gate_out = (pl.BlockSpec((1, N_GATE, tm), lambda bi, i: (bi, 0, i)),
                jax.ShapeDtypeStruct((b, N_GATE, l), F32))
    outs = [tok_out(2 * M_WIDTH), _chunk_t_out(b, l, tm), tok_out(M_WIDTH), gate_out, tok_out(d), tok_out(d)]
    return pl.pallas_call(
        _inproj_kernel,
        grid=(b, l // tm),
        in_specs=[tok(d), vec, vec] + [_full(c.shape) for c in consts],
        out_specs=[spec for spec, _ in outs],
        out_shape=[shape for _, shape in outs],
        compiler_params=_params(("arbitrary", "arbitrary")),
        name="inproj",
    )(x, shift, scale, *consts)


def _conv_kernel(x_ref, prev_ref, next_ref, w_ref, b_ref, qt_ref, k_ref):
    i = pl.program_id(1)
    last = pl.num_programs(1) - 1
    x = x_ref[0].astype(F32)
    tc = x.shape[0]
    prev_row = jnp.where(i > 0, prev_ref[0].astype(F32)[SEG_ALIGN - 1:SEG_ALIGN], 0.0)
    next_row = jnp.where(i < last, next_ref[0].astype(F32)[0:1], 0.0)
    rio = lax.broadcasted_iota(I32, x.shape, 0)
    x_prev = jnp.where(rio == 0, prev_row, pltpu.roll(x, 1, axis=0))
    x_next = jnp.where(rio == tc - 1, next_row, pltpu.roll(x, tc - 1, axis=0))
    w = w_ref[...]
    acc = b_ref[...] + x_prev * w[0:1] + x * w[1:2] + x_next * w[2:3]
    y = acc * jax.nn.sigmoid(acc)
    q = y[:, :M_WIDTH] * (M_HEAD_DIM ** -0.5)
    for ci in range(tc // CHUNK):
        qt_ref[0, ci] = q[ci * CHUNK:(ci + 1) * CHUNK].T.astype(BF16)
    k_ref[0] = y[:, M_WIDTH:].astype(BF16)


def _conv(qk_raw, conv_w, conv_b):
    b, l, c = qk_raw.shape
    tc = min(512, l)
    per = tc // SEG_ALIGN
    nblk = l // SEG_ALIGN
    return pl.pallas_call(
        _conv_kernel,
        grid=(b, l // tc),
        in_specs=[pl.BlockSpec((1, tc, c), lambda bi, i: (bi, i, 0)),
                  pl.BlockSpec((1, SEG_ALIGN, c), lambda bi, i: (bi, jnp.maximum(i * per - 1, 0), 0)),
                  pl.BlockSpec((1, SEG_ALIGN, c), lambda bi, i: (bi, jnp.minimum((i + 1) * per, nblk - 1), 0)),
                  _full(conv_w.shape), _full((1, c))],
        out_specs=[_chunk_t_out(b, l, tc)[0], pl.BlockSpec((1, tc, M_WIDTH), lambda bi, i: (bi, i, 0))],
        out_shape=[_chunk_t_out(b, l, tc)[1], jax.ShapeDtypeStruct((b, l, M_WIDTH), BF16)],
        compiler_params=_params(("arbitrary", "arbitrary")),
        name="conv",
    )(qk_raw, qk_raw, qk_raw, conv_w, conv_b.reshape(1, c))


HEADS_PER_STEP = 2
N_CHAINS = 2 * HEADS_PER_STEP
GATE_ROWS = 2 * N_CHAINS


STATE_ROWS = M_HEAD_DIM + SEG_ALIGN


def _state_step(k, vt, i_row, b_row, btot, m_old, m_new, state):
    t = k.shape[0]
    a = jnp.exp(btot - b_row + i_row - m_new)
    cd = jnp.exp(btot + m_old - m_new)
    ones = (lax.broadcasted_iota(I32, (STATE_ROWS - M_HEAD_DIM, t), 0) == 0).astype(F32)
    vta = (jnp.concatenate([vt.astype(F32), ones], axis=0) * a).astype(BF16)
    return cd * state + _dot(vta, k)


def _chunk_out_t(qt, k, vt, i_row, b_row, mask_t, m_old, state):
    t = k.shape[0]
    g_src = jnp.broadcast_to(i_row - b_row, (t, t)).T
    logw = jnp.where(mask_t, b_row + g_src, -jnp.inf)
    m_inter = b_row + m_old
    m_t = jnp.maximum(jnp.max(logw, axis=0, keepdims=True), m_inter)
    both = _dot(jnp.concatenate([k, state.astype(BF16)], axis=0), qt)
    s = both[:t] * jnp.exp(logw - m_t)
    inter = both[t:]
    decay = jnp.exp(m_inter - m_t)
    num = _dot(vt, s.astype(BF16)) + decay * inter[:M_HEAD_DIM]
    den = jnp.sum(s, axis=0, keepdims=True) + decay * inter[M_HEAD_DIM:M_HEAD_DIM + 1]
    return num * (1.0 / jnp.maximum(jnp.abs(den), jnp.exp(-m_t)))


def _mlstm_kernel(qt_ref, k_ref, vt_ref, row_ref, kc_ref, vtc_ref, rowc_ref, out_ref,
                  h_ref, bt_ref, ma_ref, mold_ref, mnew_ref):
    nc, ncc, t = k_ref.shape[1], kc_ref.shape[1], CHUNK
    src = lax.broadcasted_iota(I32, (t, t), 0)
    tgt = lax.broadcasted_iota(I32, (t, t), 1)
    masks_t = (src <= tgt, src >= tgt)
    grow = pl.ds(pl.multiple_of(pl.program_id(1) * GATE_ROWS, GATE_ROWS), GATE_ROWS)
    head = [slice(j * M_HEAD_DIM, (j + 1) * M_HEAD_DIM) for j in range(HEADS_PER_STEP)]

    state = [jnp.zeros((STATE_ROWS, M_HEAD_DIM), F32)] * N_CHAINS
    m_st = [jnp.zeros((1, 1), F32)] * N_CHAINS
    for dirn in range(2):
        for ci in (range(ncc) if dirn == 0 else reversed(range(ncc))):
            rows = rowc_ref[0, ci, grow, :]
            for j in range(HEADS_PER_STEP):
                ch = 2 * dirn + j
                i_row, b_row = rows[ch:ch + 1], rows[N_CHAINS + ch:N_CHAINS + ch + 1]
                btot = b_row[:, t - 1:t] if dirn == 0 else b_row[:, 0:1]
                m_new = jnp.maximum(btot + m_st[ch], jnp.max(btot - b_row + i_row, axis=1, keepdims=True))
                state[ch] = _state_step(kc_ref[0, ci, :, head[j]], vtc_ref[0, ci, head[j], :], i_row, b_row,
                                        btot, m_st[ch], m_new, state[ch])
                m_st[ch] = m_new

    gates = row_ref[0, :, grow, :]
    gi, gb = gates[:, :N_CHAINS], gates[:, N_CHAINS:]
    is_fwd = lax.broadcasted_iota(I32, gb.shape, 1) < HEADS_PER_STEP
    bt = jnp.where(is_fwd, jnp.broadcast_to(gb[:, :, t - 1:t], gb.shape),
                   jnp.broadcast_to(gb[:, :, 0:1], gb.shape))
    bt_ref[...] = bt
    ma_ref[...] = jnp.broadcast_to(jnp.max(bt - gb + gi, axis=2, keepdims=True), gb.shape)

    def m_scan(s, carry):
        m_f, m_b = carry
        cf, cb = s, nc - 1 - s
        mold_ref[cf, 0:HEADS_PER_STEP, :] = m_f[0:HEADS_PER_STEP]
        m_f = jnp.maximum(bt_ref[cf] + m_f, ma_ref[cf])
        mnew_ref[cf, 0:HEADS_PER_STEP, :] = m_f[0:HEADS_PER_STEP]
        mold_ref[cb, HEADS_PER_STEP:, :] = m_b[HEADS_PER_STEP:]
        m_b = jnp.maximum(bt_ref[cb] + m_b, ma_ref[cb])
        mnew_ref[cb, HEADS_PER_STEP:, :] = m_b[HEADS_PER_STEP:]
        return m_f, m_b

    m0 = jnp.concatenate([jnp.broadcast_to(m, (1, t)) for m in m_st], axis=0)
    lax.fori_loop(0, nc, m_scan, (m0, m0))

    def advance(ci, dirn, states):
        rows = row_ref[0, ci, grow, :]
        m_old, m_new, btot = mold_ref[ci], mnew_ref[ci], bt_ref[ci]
        new_states = []
        for j in range(HEADS_PER_STEP):
            ch = 2 * dirn + j
            qt, k, vt = qt_ref[0, ci, head[j], :], k_ref[0, ci, :, head[j]], vt_ref[0, ci, head[j], :]
            i_row, b_row = rows[ch:ch + 1], rows[N_CHAINS + ch:N_CHAINS + ch + 1]
            h_ref[dirn, ci, head[j], :] = _chunk_out_t(qt, k, vt, i_row, b_row, masks_t[dirn],
                                                       m_old[ch:ch + 1, 0:1], states[j])
            new_states.append(_state_step(k, vt, i_row, b_row, btot[ch:ch + 1, 0:1], m_old[ch:ch + 1, 0:1],
                                          m_new[ch:ch + 1, 0:1], states[j]))
        return new_states

    def body(s, states):
        fwd = advance(s, 0, states[:HEADS_PER_STEP])
        bwd = advance(nc - 1 - s, 1, states[HEADS_PER_STEP:])
        return tuple(fwd + bwd)

    lax.fori_loop(0, nc, body, tuple(state), unroll=2)

    def norm_body(ci, _):
        for j in range(HEADS_PER_STEP):
            ht = h_ref[0, ci, head[j], :] + h_ref[1, ci, head[j], :]
            hn = ht * lax.rsqrt(jnp.mean(ht * ht, axis=0, keepdims=True) + EPS)
            out_ref[0, ci, :, head[j]] = hn.T.astype(BF16)
        return 0

    lax.fori_loop(0, nc, norm_body, 0)


def _mlstm(qt, k, vt, gate_rows, kc, vtc, gate_rows_c):
    b, l, _ = k.shape
    lc = kc.shape[1]
    nc, ncc = l // CHUNK, lc // CHUNK
    pairs = M_HEADS // HEADS_PER_STEP
    width = HEADS_PER_STEP * M_HEAD_DIM
    chunked = lambda a, n: a.reshape(b, n, CHUNK, M_WIDTH)
    by_chunk = lambda g, n: g.reshape(b, pairs * GATE_ROWS, n, CHUNK).transpose(0, 2, 1, 3)
    once = pl.Buffered(1)
    seq = lambda n: pl.BlockSpec((1, n, CHUNK, width), lambda bi, h: (bi, 0, 0, h), pipeline_mode=once)
    seq_t = lambda n: pl.BlockSpec((1, n, width, CHUNK), lambda bi, h: (bi, 0, h, 0), pipeline_mode=once)
    rows = lambda n: pl.BlockSpec((1, n, pairs * GATE_ROWS, CHUNK), lambda bi, h: (bi, 0, 0, 0),
                                  pipeline_mode=once)
    table = pltpu.VMEM((nc, N_CHAINS, CHUNK), F32)
    out = pl.pallas_call(
        _mlstm_kernel,
        grid=(b, pairs),
        in_specs=[seq_t(nc), seq(nc), seq_t(nc), rows(nc), seq(ncc), seq_t(ncc), rows(ncc)],
        out_specs=pl.BlockSpec((1, nc, CHUNK, width), lambda bi, h: (bi, 0, 0, h)),
        out_shape=jax.ShapeDtypeStruct((b, nc, CHUNK, M_WIDTH), BF16),
        scratch_shapes=[pltpu.VMEM((2, nc, width, CHUNK), F32), table, table, table, table],
        compiler_params=_params(("arbitrary", "arbitrary")),
        name="mlstm",
    )(qt, chunked(k, nc), vt, by_chunk(gate_rows, nc), chunked(kc, ncc), vtc, by_chunk(gate_rows_c, ncc))
    return out.reshape(b, l, M_WIDTH)


def _merge_kernel(x_ref, hn_ref, og_ref, bra_ref, gb_ref, g1_ref, hw_ref, wbm_ref, wout_ref, o_ref):
    yb = hn_ref[0].astype(F32) * hw_ref[...] * og_ref[0].astype(F32)
    y = bra_ref[0].astype(F32) + gb_ref[0].astype(F32) * _dot(yb.astype(BF16), wbm_ref[...])
    o_ref[0] = x_ref[0] + g1_ref[0] * _dot(y.astype(BF16), wout_ref[...])


def _merge(x, hn, og, bra, gb, g1, hnorm_w, wbm, wout):
    b, l, d = x.shape
    tm = min(PROJ_TILE, l)
    tok = lambda width: pl.BlockSpec((1, tm, width), lambda bi, i: (bi, i, 0))
    return pl.pallas_call(
        _merge_kernel,
        grid=(b, l // tm),
        in_specs=[tok(d), tok(M_WIDTH), tok(M_WIDTH), tok(d), tok(d),
                  pl.BlockSpec((1, 1, d), lambda bi, i: (bi, 0, 0)),
                  _full((1, M_WIDTH)), _full(wbm.shape), _full(wout.shape)],
        out_specs=tok(d),
        out_shape=jax.ShapeDtypeStruct((b, l, d), F32),
        compiler_params=_params(("arbitrary", "arbitrary")),
        name="merge",
    )(x, hn, og, bra, gb, g1, hnorm_w.reshape(1, M_WIDTH), wbm, wout)


def _router_kernel(x_ref, sh_ref, sc_ref, nw_ref, wrt_ref, br_ref, ustrict_ref, lstrict_ref,
                   xn_ref, slot_ref, wts_ref, cnt_ref):
    x = x_ref[0]
    tm = x.shape[0]
    xn = x * lax.rsqrt(jnp.mean(x * x, axis=-1, keepdims=True) + EPS) * nw_ref[...]
    xm = xn * (1.0 + sc_ref[0]) + sh_ref[0]
    xn_ref[0] = xm.astype(BF16)

    logits = _dot_nt(wrt_ref[...], xm, HIGHEST) + br_ref[...]
    eio = lax.broadcasted_iota(I32, logits.shape, 0).astype(F32)
    rest = logits
    onehots, vals = [], []
    for _ in range(TOP_K):
        mx = jnp.max(rest, axis=0, keepdims=True)
        idx = jnp.min(jnp.where(rest == mx, eio, float(N_EXPERTS)), axis=0, keepdims=True)
        oh = eio == idx
        onehots.append(oh)
        vals.append(mx)
        rest = jnp.where(oh, -jnp.inf, rest)
    exps = [jnp.exp(vk - vals[0]) for vk in vals]
    denom = exps[0] + exps[1] + exps[2] + exps[3]

    oh_all = jnp.zeros(logits.shape, F32)
    for oh in onehots:
        oh_all = oh_all + oh.astype(F32)
    cnt = jnp.sum(oh_all, axis=1, keepdims=True)
    n_al = jnp.ceil(cnt * (1.0 / SEG_ALIGN)) * SEG_ALIGN
    seg_off = _dot(lstrict_ref[...], jnp.broadcast_to(n_al, (N_EXPERTS, LANES)), HIGHEST)[:, 0:1]
    rank = _dot(oh_all.astype(BF16), ustrict_ref[...])
    base = seg_off + rank
    for kk in range(TOP_K):
        slot = jnp.sum(jnp.where(onehots[kk], base, 0.0), axis=0, keepdims=True)
        slot_ref[0, kk:kk + 1, :] = slot.astype(I32)
        wts_ref[0, kk:kk + 1, :] = exps[kk] / denom
    slot_ref[0, TOP_K:, :] = jnp.full((8 - TOP_K, tm), -1, I32)
    wts_ref[0, TOP_K:, :] = jnp.zeros((8 - TOP_K, tm), F32)
    cnt_ref[0] = jnp.broadcast_to(cnt, (N_EXPERTS, LANES))


def _router(x2, shift, scale, nw, w_router, b_router):
    b, l, d = x2.shape
    tm = ROUTE_TILE
    nl = l // tm
    nt = b * nl
    s = np.arange(tm)
    ustrict = jnp.asarray((s[:, None] < s[None, :]).astype(np.float32), BF16)
    e = np.arange(N_EXPERTS)
    lstrict = jnp.asarray((e[None, :] < e[:, None]).astype(np.float32))
    vec = pl.BlockSpec((1, 1, d), lambda bi, i: (bi, 0, 0))
    tile = lambda r, c: pl.BlockSpec((1, r, c), lambda bi, i: (bi * nl + i, 0, 0))
    return pl.pallas_call(
        _router_kernel,
        grid=(b, nl),
        in_specs=[pl.BlockSpec((1, tm, d), lambda bi, i: (bi, i, 0)), vec, vec, _full((1, d)),
                  _full((N_EXPERTS, d)), _full((N_EXPERTS, 1)), _full((tm, tm)),
                  _full((N_EXPERTS, N_EXPERTS))],
        out_specs=[pl.BlockSpec((1, tm, d), lambda bi, i: (bi, i, 0)),
                   tile(8, tm), tile(8, tm), tile(N_EXPERTS, LANES)],
        out_shape=[jax.ShapeDtypeStruct((b, l, d), BF16),
                   jax.ShapeDtypeStruct((nt, 8, tm), I32),
                   jax.ShapeDtypeStruct((nt, 8, tm), F32),
                   jax.ShapeDtypeStruct((nt, N_EXPERTS, LANES), F32)],
        compiler_params=_params(("arbitrary", "arbitrary")),
        name="router",
    )(x2, shift, scale, nw, w_router.T, b_router.reshape(N_EXPERTS, 1), ustrict, lstrict)


def _segment_copies(tile, n_s, off_s, dst_s, make_copy, action):
    def body(e, _):
        idx = tile * N_EXPERTS + e
        n, seg, dst = n_s[idx], off_s[idx], dst_s[idx]
        pos = jnp.int32(0)
        for sz in SEG_SIZES:
            take = (n & sz) != 0

            @pl.when(take)
            def _(pos=pos, sz=sz):
                action(make_copy(pl.multiple_of(seg + pos, SEG_ALIGN),
                                 pl.multiple_of(dst + pos, SEG_ALIGN), sz))

            pos = pos + jnp.where(take, sz, 0)
        return 0

    lax.fori_loop(0, N_EXPERTS, body, 0)


def _zero_fill(tail_s, xs_hbm, zero_ref, sem, action):
    def make_copy(dst, sz):
        return pltpu.make_async_copy(zero_ref.at[pl.ds(0, sz)], xs_hbm.at[pl.ds(dst, sz)], sem)

    def tail_body(e, _):
        start, n = tail_s[e], tail_s[N_EXPERTS + e]
        pos = jnp.int32(0)
        for sz in SEG_SIZES[1:]:
            take = (n & sz) != 0

            @pl.when(take)
            def _(pos=pos, sz=sz):
                action(make_copy(pl.multiple_of(start + pos, SEG_ALIGN), sz))

            pos = pos + jnp.where(take, sz, 0)
        return 0

    lax.fori_loop(0, N_EXPERTS, tail_body, 0)

    def block_body(blk, _):
        action(make_copy(pl.multiple_of(blk * EXPERT_BLOCK, EXPERT_BLOCK), EXPERT_BLOCK))
        return 0

    lax.fori_loop(tail_s[2 * N_EXPERTS], tail_s[2 * N_EXPERTS + 1], block_body, 0)


def _dispatch_kernel(n_s, off_s, dst_s, tail_s, xn_ref, slot_ref, wts_ref, xs_hbm, g_ref, zero_ref, sem):
    tile = pl.program_id(0)
    x = xn_ref[...]
    tm = x.shape[0]
    slot = slot_ref[0]
    wts = wts_ref[0]
    rc = 256
    lane = lax.broadcasted_iota(I32, (rc, LANES), 1)
    for ci in range(SEG_PAD // rc):
        rio = lax.broadcasted_iota(I32, (rc, tm), 0) + ci * rc
        sel = jnp.zeros((rc, tm), F32)
        wsel = jnp.zeros((rc, tm), F32)
        for kk in range(TOP_K):
            hit = rio == slot[kk:kk + 1, :]
            sel = jnp.where(hit, 1.0, sel)
            wsel = jnp.where(hit, wts[kk:kk + 1, :], wsel)
        g_ref[ci * rc:(ci + 1) * rc, 0:D_MODEL] = _dot(sel.astype(BF16), x).astype(BF16)
        w_row = jnp.sum(wsel, axis=1, keepdims=True)
        w_hi = w_row.astype(BF16).astype(F32)
        g_ref[ci * rc:(ci + 1) * rc, D_MODEL:XS_WIDTH] = jnp.where(lane < LANES // 2, w_hi, w_row - w_hi).astype(BF16)

    def make_copy(src, dst, sz):
        return pltpu.make_async_copy(g_ref.at[pl.ds(src, sz)], xs_hbm.at[pl.ds(dst, sz)], sem)

    _segment_copies(tile, n_s, off_s, dst_s, make_copy, lambda cp: cp.start())
    _segment_copies(tile, n_s, off_s, dst_s, make_copy, lambda cp: cp.wait())

    @pl.when(tile == pl.num_programs(0) - 1)
    def _():
        zero_ref[...] = jnp.zeros(zero_ref.shape, BF16)
        _zero_fill(tail_s, xs_hbm, zero_ref, sem, lambda cp: cp.start())
        _zero_fill(tail_s, xs_hbm, zero_ref, sem, lambda cp: cp.wait())


def _dispatch(xn2, slot, wts, n_flat, off_flat, dst_flat, tail_flat, rows_total):
    t, d = xn2.shape
    tm = ROUTE_TILE
    nt = t // tm
    return pl.pallas_call(
        _dispatch_kernel,
        grid_spec=pltpu.PrefetchScalarGridSpec(
            num_scalar_prefetch=4,
            grid=(nt,),
            in_specs=[pl.BlockSpec((tm, d), lambda i, *_: (i, 0)),
                      pl.BlockSpec((1, 8, tm), lambda i, *_: (i, 0, 0)),
                      pl.BlockSpec((1, 8, tm), lambda i, *_: (i, 0, 0))],
            out_specs=pl.BlockSpec(memory_space=pl.ANY),
            scratch_shapes=[pltpu.VMEM((SEG_PAD, XS_WIDTH), BF16),
                            pltpu.VMEM((EXPERT_BLOCK, XS_WIDTH), BF16),
                            pltpu.SemaphoreType.DMA(())]),
        out_shape=jax.ShapeDtypeStruct((rows_total, XS_WIDTH), BF16),
        compiler_params=_params(("arbitrary",)),
        name="dispatch",
    )(n_flat, off_flat, dst_flat, tail_flat, xn2, slot, wts)


def _combine_kernel(n_s, off_s, dst_s, out_hbm, slott_ref, x2_ref, g2_ref, fw_ref, y_ref, buf_ref, sem):
    tile = pl.program_id(0)

    @pl.when(tile == 0)
    def _():
        buf_ref[...] = jnp.zeros(buf_ref.shape, BF16)

    def make_copy(seg, src, sz):
        return pltpu.make_async_copy(out_hbm.at[pl.ds(src, sz)], buf_ref.at[pl.ds(seg, sz)], sem)

    _segment_copies(tile, n_s, off_s, dst_s, make_copy, lambda cp: cp.start())
    _segment_copies(tile, n_s, off_s, dst_s, make_copy, lambda cp: cp.wait())

    st = slott_ref[...]
    tm = st.shape[0]
    kc = 512
    acc = jnp.zeros((tm, D_MODEL), F32)
    for ci in range(SEG_PAD // kc):
        lio = lax.broadcasted_iota(I32, (tm, kc), 1) + ci * kc
        sel = jnp.zeros((tm, kc), F32)
        for kk in range(TOP_K):
            sel = jnp.where(lio == st[:, kk:kk + 1], 1.0, sel)
        acc = acc + _dot(sel.astype(BF16), buf_ref[ci * kc:(ci + 1) * kc, :])
    x3 = x2_ref[...] + g2_ref[0] * acc
    y_ref[...] = x3 * lax.rsqrt(jnp.mean(x3 * x3, axis=-1, keepdims=True) + EPS) * fw_ref[...]


def _combine(out_rows, slot_t, x2_flat, g2, final_w, n_flat, off_flat, dst_flat, tiles_per_batch):
    t, d = x2_flat.shape
    tm = ROUTE_TILE
    nt = t // tm
    return pl.pallas_call(
        _combine_kernel,
        grid_spec=pltpu.PrefetchScalarGridSpec(
            num_scalar_prefetch=3,
            grid=(nt,),
            in_specs=[pl.BlockSpec(memory_space=pl.ANY),
                      pl.BlockSpec((tm, 8), lambda i, *_: (i, 0)),
                      pl.BlockSpec((tm, d), lambda i, *_: (i, 0)),
                      pl.BlockSpec((1, 1, d), lambda i, *_: (i // tiles_per_batch, 0, 0)),
                      pl.BlockSpec((1, d), lambda i, *_: (0, 0))],
            out_specs=pl.BlockSpec((tm, d), lambda i, *_: (i, 0)),
            scratch_shapes=[pltpu.VMEM((SEG_PAD, D_MODEL), BF16), pltpu.SemaphoreType.DMA(())]),
        out_shape=jax.ShapeDtypeStruct((t, d), F32),
        compiler_params=_params(("arbitrary",)),
        name="combine",
    )(n_flat, off_flat, dst_flat, out_rows, slot_t, x2_flat, g2, final_w.reshape(1, d))


def _expert_kernel(blk_e, nb_used, xs_ref, w1_ref, b1_ref, w2_ref, b2_ref, o_ref, w1b_ref, w2b_ref):
    i = pl.program_id(0)
    used = i < nb_used[0]

    @pl.when(jnp.logical_not(used))
    def _():
        o_ref[...] = jnp.zeros(o_ref.shape, BF16)

    @pl.when(used & ((i == 0) | (blk_e[i] != blk_e[jnp.maximum(i - 1, 0)])))
    def _():
        w1b_ref[...] = w1_ref[0].astype(BF16)
        w2b_ref[...] = w2_ref[0].astype(BF16)

    @pl.when(used)
    def _():
        x = xs_ref[...]
        w_row = (x[:, D_MODEL:D_MODEL + 1].astype(F32)
                 + x[:, D_MODEL + LANES // 2:D_MODEL + LANES // 2 + 1].astype(F32))
        gu = _dot(x[:, :D_MODEL], w1b_ref[...]) + b1_ref[0]
        gate = jnp.minimum(gu[:, :D_FF], SWIGLU_LIMIT)
        up = jnp.clip(gu[:, D_FF:], -SWIGLU_LIMIT, SWIGLU_LIMIT)
        act = (up + 1.0) * gate * jax.nn.sigmoid(SWIGLU_ALPHA * gate)
        o = _dot(act.astype(BF16), w2b_ref[...]) + b2_ref[0]
        o_ref[...] = (o * w_row).astype(BF16)


def _expert(xs, blk_e, nb_used, w1, b1, w2, b2):
    rows = xs.shape[0]
    nb = rows // EXPERT_BLOCK
    row_blk = lambda i, be, nu: (jnp.minimum(i, nu[0] - 1), 0)
    per_e = lambda i, be, nu: (be[i], 0, 0)
    return pl.pallas_call(
        _expert_kernel,
        grid_spec=pltpu.PrefetchScalarGridSpec(
            num_scalar_prefetch=2,
            grid=(nb,),
            in_specs=[pl.BlockSpec((EXPERT_BLOCK, XS_WIDTH), row_blk),
                      pl.BlockSpec((1, D_MODEL, 2 * D_FF), per_e),
                      pl.BlockSpec((1, 1, 2 * D_FF), per_e),
                      pl.BlockSpec((1, D_FF, D_MODEL), per_e),
                      pl.BlockSpec((1, 1, D_MODEL), per_e)],
            out_specs=pl.BlockSpec((EXPERT_BLOCK, D_MODEL), lambda i, be, nu: (i, 0)),
            scratch_shapes=[pltpu.VMEM((D_MODEL, 2 * D_FF), BF16), pltpu.VMEM((D_FF, D_MODEL), BF16)]),
        out_shape=jax.ShapeDtypeStruct((rows, D_MODEL), BF16),
        compiler_params=_params(("arbitrary",)),
        name="expert",
    )(blk_e, nb_used, xs, w1, b1.reshape(N_EXPERTS, 1, 2 * D_FF), w2, b2.reshape(N_EXPERTS, 1, D_MODEL))


def _routing_tables(cnt, nb):
    cnt = cnt.astype(I32)
    n_al = (cnt + SEG_ALIGN - 1) // SEG_ALIGN * SEG_ALIGN
    seg_off = jnp.cumsum(n_al, axis=1) - n_al
    rel = jnp.cumsum(n_al, axis=0) - n_al
    tot = jnp.sum(n_al, axis=0)
    blocks_e = (tot + EXPERT_BLOCK - 1) // EXPERT_BLOCK
    blk_end = jnp.cumsum(blocks_e)
    e_start = (blk_end - blocks_e) * EXPERT_BLOCK
    dst = e_start[None, :] + rel
    blk_e = jnp.minimum(jnp.sum(blk_end[None, :] <= jnp.arange(nb, dtype=I32)[:, None], axis=1),
                        N_EXPERTS - 1).astype(I32)
    nb_used = blk_end[-1:].astype(I32)
    tails = jnp.concatenate([e_start + tot, blocks_e * EXPERT_BLOCK - tot, nb_used,
                             jnp.full((1,), nb, I32)]).astype(I32)
    return (n_al.reshape(-1), seg_off.reshape(-1).astype(I32), dst.reshape(-1).astype(I32), blk_e,
            nb_used, tails)


def kernel(x, c, ctx, c_ctx, w_ada, b_ada, norm1_w, w_in, gate_b, conv_w, conv_b, w_pool, pool_scale,
           hnorm_w, w_bp, w_bm, w_out, norm2_w, w_router, b_router, w1, b1, w2, b2, final_norm_w):
    assert w_ada.shape[0] == 1, "single-layer kernel"
    b, l, d = x.shape
    lc = ctx.shape[1]
    assert d == D_MODEL and l % ROUTE_TILE == 0 and l % GRID_W == 0 and lc % CHUNK == 0
    t = b * l

    rows = (b + 1 + 7) // 8 * 8
    cc = jnp.zeros((rows, d), F32).at[:b].set(c).at[b].set(c_ctx)
    mod = _ada(cc, w_ada[0], b_ada[0])
    sh1, s1, g1, sh2, s2, g2 = [mod[:b, i * d:(i + 1) * d].reshape(b, 1, d) for i in range(6)]
    csh1, cs1 = [jnp.broadcast_to(mod[b, i * d:(i + 1) * d].reshape(1, 1, d), (b, 1, d)) for i in range(2)]

    w_in0 = w_in[0]
    off_gate = POOL_WIDTH + 4 * M_WIDTH
    wmain = w_in0[:, :off_gate].astype(BF16)
    order = [4 * (2 * dirn + is_f) + HEADS_PER_STEP * pair + j
             for pair in range(M_HEADS // HEADS_PER_STEP) for is_f in range(2) for dirn in range(2)
             for j in range(HEADS_PER_STEP)]
    wg_hi, wg_lo = _split_bf16(w_in0[:, off_gate:off_gate + N_GATE][:, order])
    wgate = jnp.pad(jnp.concatenate([wg_hi, wg_lo], axis=1), ((0, 0), (0, LANES - 2 * N_GATE)))
    wmerge = w_in0[:, off_gate + N_GATE:].astype(BF16)
    gbias = jnp.pad(gate_b[0][jnp.asarray(order)], (0, LANES - N_GATE)).reshape(1, LANES)
    proj_consts = (norm1_w[0].reshape(1, d), gbias, wmain, wgate, wmerge, w_pool[0].astype(BF16),
                   pool_scale[0].reshape(1, POOL_WIDTH), w_bp[0].astype(BF16))

    qk_raw, vt, og, gate_rows, bra, gb = _inproj(x, sh1, s1, *proj_consts)
    qk_raw_c, vt_c, _, gate_rows_c, _, _ = _inproj(ctx, csh1, cs1, *proj_consts)
    qt, k = _conv(qk_raw, conv_w[0], conv_b[0])
    _, k_c = _conv(qk_raw_c, conv_w[0], conv_b[0])
    hn = _mlstm(qt, k, vt, gate_rows, k_c, vt_c, gate_rows_c)
    x2 = _merge(x, hn, og, bra, gb, g1, hnorm_w[0], w_bm[0].astype(BF16), w_out[0].astype(BF16))

    xn2, slot, wts, cnt = _router(x2, sh2, s2, norm2_w[0].reshape(1, d), w_router[0], b_router[0])
    nt = t // ROUTE_TILE
    nb = (t * TOP_K + nt * N_EXPERTS * (SEG_ALIGN - 1)) // EXPERT_BLOCK + N_EXPERTS
    n_flat, off_flat, dst_flat, blk_e, nb_used, tails = _routing_tables(cnt[:, :, 0], nb)
    xs = _dispatch(xn2.reshape(t, d), slot, wts, n_flat, off_flat, dst_flat, tails, nb * EXPERT_BLOCK)
    out_rows = _expert(xs, blk_e, nb_used, w1[0], b1[0], w2[0], b2[0])
    slot_t = slot.transpose(0, 2, 1).reshape(t, 8)
    y = _combine(out_rows, slot_t, x2.reshape(t, d), g2, final_norm_w, n_flat, off_flat, dst_flat,
                 l // ROUTE_TILE)
    return y.reshape(b, l, d)
```

```python
import functools

import numpy as np
import jax
import jax.numpy as jnp
from jax import lax
from jax.experimental import pallas as pl
from jax.experimental.pallas import tpu as pltpu

F32 = jnp.float32
BF16 = jnp.bfloat16
I32 = jnp.int32
HIGHEST = lax.Precision.HIGHEST

D_MODEL = 1024
EPS = 1e-6
GRID_W = 64
POOL_WINDOWS = (2, 4, 8, 16)
POOL_WIDTH = 512
POOL_GC = 128
M_HEADS = 4
M_HEAD_DIM = 128
M_WIDTH = 512
CHUNK = 128
N_GATE = 16
N_EXPERTS = 32
TOP_K = 4
D_FF = 1024
SWIGLU_LIMIT = 7.0
SWIGLU_ALPHA = 1.702

LANES = 128
PROJ_TILE = 512
POOL_SLAB = 256
ROUTE_TILE = 512
SEG_ALIGN = 16
SEG_PAD = TOP_K * ROUTE_TILE + N_EXPERTS * SEG_ALIGN
EXPERT_BLOCK = 512
VMEM_LIMIT = 56 * 1024 * 1024


def _dot(a, b, precision=None):
    return jnp.dot(a, b, preferred_element_type=F32, precision=precision)


def _params(semantics):
    return pltpu.CompilerParams(dimension_semantics=semantics, vmem_limit_bytes=VMEM_LIMIT)


def _full(shape):
    nd = len(shape)
    return pl.BlockSpec(shape, lambda *_: (0,) * nd, pipeline_mode=pl.Buffered(1))


def _split_bf16(a):
    hi = a.astype(BF16)
    return hi, (a - hi.astype(F32)).astype(BF16)


def _ada_kernel(c_ref, w_ref, b_ref, o_ref):
    c = c_ref[...]
    s = c * jax.nn.sigmoid(c)
    o_ref[...] = _dot(s, w_ref[...], HIGHEST) + b_ref[...]


def _ada(cc, w_ada, b_ada):
    rows, d = cc.shape
    n = w_ada.shape[1]
    tn = 1024
    return pl.pallas_call(
        _ada_kernel,
        grid=(n // tn,),
        in_specs=[pl.BlockSpec((rows, d), lambda j: (0, 0)),
                  pl.BlockSpec((d, tn), lambda j: (0, j)),
                  pl.BlockSpec((1, tn), lambda j: (0, j))],
        out_specs=pl.BlockSpec((rows, tn), lambda j: (0, j)),
        out_shape=jax.ShapeDtypeStruct((rows, n), F32),
        compiler_params=_params(("arbitrary",)),
        name="ada",
    )(cc, w_ada, b_ada.reshape(1, n))


def _pool_constants(tm):
    pos = np.arange(tm) % GRID_W
    row = np.arange(tm) // GRID_W
    slab = min(tm, POOL_SLAB)
    pm = np.zeros((len(POOL_WINDOWS), slab, slab), np.float32)
    inv = np.zeros((tm, POOL_WIDTH), np.float32)
    for g, win in enumerate(POOL_WINDOWS):
        lo = np.clip(pos - win // 2, 0, GRID_W)
        hi = np.clip(pos + win // 2, 0, GRID_W)
        same = row[:, None] == row[None, :]
        full = (same & (pos[None, :] >= lo[:, None]) & (pos[None, :] < hi[:, None])).astype(np.float32)
        pm[g] = full[:slab, :slab]
        inv[:, g * POOL_GC:(g + 1) * POOL_GC] = (1.0 / (hi - lo).astype(np.float32))[:, None]
    t = np.arange(tm)
    same_chunk = (t[:, None] // CHUNK) == (t[None, :] // CHUNK)
    tl = (same_chunk & (t[None, :] <= t[:, None])).astype(np.float32)
    tu = (same_chunk & (t[None, :] >= t[:, None])).astype(np.float32)
    return pm, inv, tl, tu


def _inproj_kernel(x_ref, sh_ref, sc_ref, nw_ref, gbias_ref, wmain_ref, wgate_ref, wmerge_ref,
                   pmat_ref, invc_ref, tl_ref, tu_ref, wpool_ref, pscale_ref, wbp_ref,
                   qk_ref, vt_ref, og_ref, row_ref, bra_ref, gbo_ref):
    x = x_ref[0]
    xn = x * lax.rsqrt(jnp.mean(x * x, axis=-1, keepdims=True) + EPS) * nw_ref[...]
    xm = xn * (1.0 + sc_ref[0]) + sh_ref[0]
    xb = xm.astype(BF16)

    xlo = (xm - xb.astype(F32)).astype(BF16)
    wg = wgate_ref[...]
    r_hi = _dot(xb, wg)
    g = r_hi + pltpu.roll(r_hi, LANES - N_GATE, axis=1) + _dot(xlo, wg) + gbias_ref[...]
    lf = jnp.minimum(g, 0.0) - jnp.log1p(jnp.exp(-jnp.abs(g)))
    lane = lax.broadcasted_iota(I32, g.shape, 1)
    lf_hi = lf.astype(BF16).astype(F32)
    lf2 = jnp.where(lane < N_GATE, lf_hi, pltpu.roll(lf - lf_hi, N_GATE, axis=1)).astype(BF16)
    r_pre = _dot(tl_ref[...], lf2)
    r_suf = _dot(tu_ref[...], lf2)
    b_pre = r_pre + pltpu.roll(r_pre, LANES - N_GATE, axis=1)
    b_suf = r_suf + pltpu.roll(r_suf, LANES - N_GATE, axis=1)
    kind = (lane & 7) >> 1
    col = jnp.where(kind == 2, b_pre, jnp.where(kind == 3, b_suf, g))
    row_ref[0] = col.T[:N_GATE]

    u = _dot(xb, wmain_ref[:, 0:POOL_WIDTH])
    u_hi, u_lo = _split_bf16(u)
    invc = invc_ref[...]
    slab = pmat_ref.shape[1]
    ya = []
    for gi in range(len(POOL_WINDOWS)):
        cols = slice(gi * POOL_GC, (gi + 1) * POOL_GC)
        both = jnp.concatenate([u_hi[:, cols], u_lo[:, cols]], axis=1)
        sums = jnp.concatenate([_dot(pmat_ref[gi], both[s:s + slab]) for s in range(0, u.shape[0], slab)],
                               axis=0)
        pooled = (sums[:, :POOL_GC] + sums[:, POOL_GC:]) * invc[:, cols] - u[:, cols]
        ya.append(_dot(pooled.astype(BF16), wpool_ref[gi]))
    ya = jnp.concatenate(ya, axis=1) * pscale_ref[...]
    ga = jax.nn.sigmoid(_dot(xb, wmerge_ref[:, 0:D_MODEL]))
    bra_ref[0] = (ga * _dot(ya.astype(BF16), wbp_ref[...])).astype(BF16)
    gbo_ref[0] = jax.nn.sigmoid(_dot(xb, wmerge_ref[:, D_MODEL:2 * D_MODEL])).astype(BF16)

    qk_ref[0] = _dot(xb, wmain_ref[:, POOL_WIDTH:POOL_WIDTH + 2 * M_WIDTH]).astype(BF16)
    v = _dot(xb, wmain_ref[:, POOL_WIDTH + 2 * M_WIDTH:POOL_WIDTH + 3 * M_WIDTH])
    for ci in range(v.shape[0] // CHUNK):
        vt_ref[0, ci] = v[ci * CHUNK:(ci + 1) * CHUNK].T.astype(BF16)
    og_ref[0] = jax.nn.sigmoid(
        _dot(xb, wmain_ref[:, POOL_WIDTH + 3 * M_WIDTH:POOL_WIDTH + 4 * M_WIDTH])).astype(BF16)


def _chunk_t_out(b, l, tm):
    return (pl.BlockSpec((1, tm // CHUNK, M_WIDTH, CHUNK), lambda bi, i: (bi, i, 0, 0)),
            jax.ShapeDtypeStruct((b, l // CHUNK, M_WIDTH, CHUNK), BF16))


def _inproj(x, shift, scale, nw, gbias, wmain, wgate, wmerge, wpool, pscale, wbp):
    b, l, d = x.shape
    tm = min(PROJ_TILE, l)
    pm, inv, tl, tu = _pool_constants(tm)
    tok = lambda width: pl.BlockSpec((1, tm, width), lambda bi, i: (bi, i, 0))
    vec = pl.BlockSpec((1, 1, d), lambda bi, i: (bi, 0, 0))
    consts = [nw, gbias, wmain, wgate, wmerge, jnp.asarray(pm, BF16), jnp.asarray(inv),
              jnp.asarray(tl, BF16), jnp.asarray(tu, BF16), wpool, pscale, wbp]
    tok_out = lambda width: (tok(width), jax.ShapeDtypeStruct((b, l, width), BF16))
    gate_out = (pl.BlockSpec((1, N_GATE, tm), lambda bi, i: (bi, 0, i)),
                jax.ShapeDtypeStruct((b, N_GATE, l), F32))
    outs = [tok_out(2 * M_WIDTH), _chunk_t_out(b, l, tm), tok_out(M_WIDTH), gate_out, tok_out(d), tok_out(d)]
    return pl.pallas_call(
        _inproj_kernel,
        grid=(b, l // tm),
        in_specs=[tok(d), vec, vec] + [_full(c.shape) for c in consts],
        out_specs=[spec for spec, _ in outs],
        out_shape=[shape for _, shape in outs],
        compiler_params=_params(("arbitrary", "arbitrary")),
        name="inproj",
    )(x, shift, scale, *consts)


def _conv_kernel(x_ref, prev_ref, next_ref, w_ref, b_ref, qt_ref, k_ref):
    i = pl.program_id(1)
    last = pl.num_programs(1) - 1
    x = x_ref[0].astype(F32)
    tc = x.shape[0]
    prev_row = jnp.where(i > 0, prev_ref[0].astype(F32)[SEG_ALIGN - 1:SEG_ALIGN], 0.0)
    next_row = jnp.where(i < last, next_ref[0].astype(F32)[0:1], 0.0)
    rio = lax.broadcasted_iota(I32, x.shape, 0)
    x_prev = jnp.where(rio == 0, prev_row, pltpu.roll(x, 1, axis=0))
    x_next = jnp.where(rio == tc - 1, next_row, pltpu.roll(x, tc - 1, axis=0))
    w = w_ref[...]
    acc = b_ref[...] + x_prev * w[0:1] + x * w[1:2] + x_next * w[2:3]
    y = acc * jax.nn.sigmoid(acc)
    q = y[:, :M_WIDTH] * (M_HEAD_DIM ** -0.5)
    for ci in range(tc // CHUNK):
        qt_ref[0, ci] = q[ci * CHUNK:(ci + 1) * CHUNK].T.astype(BF16)
    k_ref[0] = y[:, M_WIDTH:].astype(BF16)


def _conv(qk_raw, conv_w, conv_b):
    b, l, c = qk_raw.shape
    tc = min(512, l)
    per = tc // SEG_ALIGN
    nblk = l // SEG_ALIGN
    return pl.pallas_call(
        _conv_kernel,
        grid=(b, l // tc),
        in_specs=[pl.BlockSpec((1, tc, c), lambda bi, i: (bi, i, 0)),
                  pl.BlockSpec((1, SEG_ALIGN, c), lambda bi, i: (bi, jnp.maximum(i * per - 1, 0), 0)),
                  pl.BlockSpec((1, SEG_ALIGN, c), lambda bi, i: (bi, jnp.minimum((i + 1) * per, nblk - 1), 0)),
                  _full(conv_w.shape), _full((1, c))],
        out_specs=[_chunk_t_out(b, l, tc)[0], pl.BlockSpec((1, tc, M_WIDTH), lambda bi, i: (bi, i, 0))],
        out_shape=[_chunk_t_out(b, l, tc)[1], jax.ShapeDtypeStruct((b, l, M_WIDTH), BF16)],
        compiler_params=_params(("arbitrary", "arbitrary")),
        name="conv",
    )(qk_raw, qk_raw, qk_raw, conv_w, conv_b.reshape(1, c))


HEADS_PER_STEP = 2
N_CHAINS = 2 * HEADS_PER_STEP
GATE_ROWS = 2 * N_CHAINS


STATE_ROWS = M_HEAD_DIM + SEG_ALIGN


def _state_step(k, vt, i_row, b_row, btot, m_old, m_new, state):
    t = k.shape[0]
    a = jnp.exp(btot - b_row + i_row - m_new)
    cd = jnp.exp(btot + m_old - m_new)
    ones = (lax.broadcasted_iota(I32, (STATE_ROWS - M_HEAD_DIM, t), 0) == 0).astype(F32)
    vta = (jnp.concatenate([vt.astype(F32), ones], axis=0) * a).astype(BF16)
    return cd * state + _dot(vta, k)


def _chunk_out_t(qt, k, vt, i_row, b_row, mask_t, m_old, state):
    t = k.shape[0]
    g_src = jnp.broadcast_to(i_row - b_row, (t, t)).T
    logw = jnp.where(mask_t, b_row + g_src, -jnp.inf)
    m_inter = b_row + m_old
    m_t = jnp.maximum(jnp.max(logw, axis=0, keepdims=True), m_inter)
    both = _dot(jnp.concatenate([k, state.astype(BF16)], axis=0), qt)
    s = both[:t] * jnp.exp(logw - m_t)
    inter = both[t:]
    decay = jnp.exp(m_inter - m_t)
    num = _dot(vt, s.astype(BF16)) + decay * inter[:M_HEAD_DIM]
    den = jnp.sum(s, axis=0, keepdims=True) + decay * inter[M_HEAD_DIM:M_HEAD_DIM + 1]
    return num * (1.0 / jnp.maximum(jnp.abs(den), jnp.exp(-m_t)))


def _mlstm_kernel(qt_ref, k_ref, vt_ref, row_ref, kc_ref, vtc_ref, rowc_ref, out_ref,
                  h_ref, bt_ref, ma_ref, mold_ref, mnew_ref):
    nc, ncc, t = k_ref.shape[1], kc_ref.shape[1], CHUNK
    src = lax.broadcasted_iota(I32, (t, t), 0)
    tgt = lax.broadcasted_iota(I32, (t, t), 1)
    masks_t = (src <= tgt, src >= tgt)
    grow = pl.ds(pl.multiple_of(pl.program_id(1) * GATE_ROWS, GATE_ROWS), GATE_ROWS)
    head = [slice(j * M_HEAD_DIM, (j + 1) * M_HEAD_DIM) for j in range(HEADS_PER_STEP)]

    state = [jnp.zeros((STATE_ROWS, M_HEAD_DIM), F32)] * N_CHAINS
    m_st = [jnp.zeros((1, 1), F32)] * N_CHAINS
    for dirn in range(2):
        for ci in (range(ncc) if dirn == 0 else reversed(range(ncc))):
            rows = rowc_ref[0, ci, grow, :]
            for j in range(HEADS_PER_STEP):
                ch = 2 * dirn + j
                i_row, b_row = rows[ch:ch + 1], rows[N_CHAINS + ch:N_CHAINS + ch + 1]
                btot = b_row[:, t - 1:t] if dirn == 0 else b_row[:, 0:1]
                m_new = jnp.maximum(btot + m_st[ch], jnp.max(btot - b_row + i_row, axis=1, keepdims=True))
                state[ch] = _state_step(kc_ref[0, ci, :, head[j]], vtc_ref[0, ci, head[j], :], i_row, b_row,
                                        btot, m_st[ch], m_new, state[ch])
                m_st[ch] = m_new

    gates = row_ref[0, :, grow, :]
    gi, gb = gates[:, :N_CHAINS], gates[:, N_CHAINS:]
    is_fwd = lax.broadcasted_iota(I32, gb.shape, 1) < HEADS_PER_STEP
    bt = jnp.where(is_fwd, jnp.broadcast_to(gb[:, :, t - 1:t], gb.shape),
                   jnp.broadcast_to(gb[:, :, 0:1], gb.shape))
    bt_ref[...] = bt
    ma_ref[...] = jnp.broadcast_to(jnp.max(bt - gb + gi, axis=2, keepdims=True), gb.shape)

    def m_scan(s, carry):
        m_f, m_b = carry
        cf, cb = s, nc - 1 - s
        mold_ref[cf, 0:HEADS_PER_STEP, :] = m_f[0:HEADS_PER_STEP]
        m_f = jnp.maximum(bt_ref[cf] + m_f, ma_ref[cf])
        mnew_ref[cf, 0:HEADS_PER_STEP, :] = m_f[0:HEADS_PER_STEP]
        mold_ref[cb, HEADS_PER_STEP:, :] = m_b[HEADS_PER_STEP:]
        m_b = jnp.maximum(bt_ref[cb] + m_b, ma_ref[cb])
        mnew_ref[cb, HEADS_PER_STEP:, :] = m_b[HEADS_PER_STEP:]
        return m_f, m_b

    m0 = jnp.concatenate([jnp.broadcast_to(m, (1, t)) for m in m_st], axis=0)
    lax.fori_loop(0, nc, m_scan, (m0, m0))

    def advance(ci, dirn, states):
        rows = row_ref[0, ci, grow, :]
        m_old, m_new, btot = mold_ref[ci], mnew_ref[ci], bt_ref[ci]
        new_states = []
        for j in range(HEADS_PER_STEP):
            ch = 2 * dirn + j
            qt, k, vt = qt_ref[0, ci, head[j], :], k_ref[0, ci, :, head[j]], vt_ref[0, ci, head[j], :]
            i_row, b_row = rows[ch:ch + 1], rows[N_CHAINS + ch:N_CHAINS + ch + 1]
            h_ref[dirn, ci, head[j], :] = _chunk_out_t(qt, k, vt, i_row, b_row, masks_t[dirn],
                                                       m_old[ch:ch + 1, 0:1], states[j])
            new_states.append(_state_step(k, vt, i_row, b_row, btot[ch:ch + 1, 0:1], m_old[ch:ch + 1, 0:1],
                                          m_new[ch:ch + 1, 0:1], states[j]))
        return new_states

    def body(s, states):
        fwd = advance(s, 0, states[:HEADS_PER_STEP])
        bwd = advance(nc - 1 - s, 1, states[HEADS_PER_STEP:])
        return tuple(fwd + bwd)

    lax.fori_loop(0, nc, body, tuple(state), unroll=2)

    def norm_body(ci, _):
        for j in range(HEADS_PER_STEP):
            ht = h_ref[0, ci, head[j], :] + h_ref[1, ci, head[j], :]
            hn = ht * lax.rsqrt(jnp.mean(ht * ht, axis=0, keepdims=True) + EPS)
            out_ref[0, ci, :, head[j]] = hn.T.astype(BF16)
        return 0

    lax.fori_loop(0, nc, norm_body, 0)


def _mlstm(qt, k, vt, gate_rows, kc, vtc, gate_rows_c):
    b, l, _ = k.shape
    lc = kc.shape[1]
    nc, ncc = l // CHUNK, lc // CHUNK
    pairs = M_HEADS // HEADS_PER_STEP
    width = HEADS_PER_STEP * M_HEAD_DIM
    chunked = lambda a, n: a.reshape(b, n, CHUNK, M_WIDTH)
    by_chunk = lambda g, n: g.reshape(b, pairs * GATE_ROWS, n, CHUNK).transpose(0, 2, 1, 3)
    once = pl.Buffered(1)
    seq = lambda n: pl.BlockSpec((1, n, CHUNK, width), lambda bi, h: (bi, 0, 0, h), pipeline_mode=once)
    seq_t = lambda n: pl.BlockSpec((1, n, width, CHUNK), lambda bi, h: (bi, 0, h, 0), pipeline_mode=once)
    rows = lambda n: pl.BlockSpec((1, n, pairs * GATE_ROWS, CHUNK), lambda bi, h: (bi, 0, 0, 0),
                                  pipeline_mode=once)
    table = pltpu.VMEM((nc, N_CHAINS, CHUNK), F32)
    out = pl.pallas_call(
        _mlstm_kernel,
        grid=(b, pairs),
        in_specs=[seq_t(nc), seq(nc), seq_t(nc), rows(nc), seq(ncc), seq_t(ncc), rows(ncc)],
        out_specs=pl.BlockSpec((1, nc, CHUNK, width), lambda bi, h: (bi, 0, 0, h)),
        out_shape=jax.ShapeDtypeStruct((b, nc, CHUNK, M_WIDTH), BF16),
        scratch_shapes=[pltpu.VMEM((2, nc, width, CHUNK), F32), table, table, table, table],
        compiler_params=_params(("arbitrary", "arbitrary")),
        name="mlstm",
    )(qt, chunked(k, nc), vt, by_chunk(gate_rows, nc), chunked(kc, ncc), vtc, by_chunk(gate_rows_c, ncc))
    return out.reshape(b, l, M_WIDTH)


def _merge_kernel(x_ref, hn_ref, og_ref, bra_ref, gb_ref, g1_ref, hw_ref, wbm_ref, wout_ref, o_ref):
    yb = hn_ref[0].astype(F32) * hw_ref[...] * og_ref[0].astype(F32)
    y = bra_ref[0].astype(F32) + gb_ref[0].astype(F32) * _dot(yb.astype(BF16), wbm_ref[...])
    o_ref[0] = x_ref[0] + g1_ref[0] * _dot(y.astype(BF16), wout_ref[...])


def _merge(x, hn, og, bra, gb, g1, hnorm_w, wbm, wout):
    b, l, d = x.shape
    tm = min(PROJ_TILE, l)
    tok = lambda width: pl.BlockSpec((1, tm, width), lambda bi, i: (bi, i, 0))
    return pl.pallas_call(
        _merge_kernel,
        grid=(b, l // tm),
        in_specs=[tok(d), tok(M_WIDTH), tok(M_WIDTH), tok(d), tok(d),
                  pl.BlockSpec((1, 1, d), lambda bi, i: (bi, 0, 0)),
                  _full((1, M_WIDTH)), _full(wbm.shape), _full(wout.shape)],
        out_specs=tok(d),
        out_shape=jax.ShapeDtypeStruct((b, l, d), F32),
        compiler_params=_params(("arbitrary", "arbitrary")),
        name="merge",
    )(x, hn, og, bra, gb, g1, hnorm_w.reshape(1, M_WIDTH), wbm, wout)


def _router_kernel(x_ref, sh_ref, sc_ref, nw_ref, wr_ref, br_ref, ustrict_ref, lstrict_ref,
                   xn_ref, slot_ref, wts_ref, cnt_ref):
    x = x_ref[0]
    tm = x.shape[0]
    xn = x * lax.rsqrt(jnp.mean(x * x, axis=-1, keepdims=True) + EPS) * nw_ref[...]
    xm = xn * (1.0 + sc_ref[0]) + sh_ref[0]
    xn_ref[0] = xm.astype(BF16)

    x_hi, x_lo = _split_bf16(xm)
    r_hi = _dot(x_hi, wr_ref[...])
    r = r_hi + pltpu.roll(r_hi, LANES - N_EXPERTS, axis=1) + _dot(x_lo, wr_ref[...])
    logits = r.T[:N_EXPERTS] + br_ref[...]
    eio = lax.broadcasted_iota(I32, logits.shape, 0).astype(F32)
    rest = logits
    onehots, vals = [], []
    for _ in range(TOP_K):
        mx = jnp.max(rest, axis=0, keepdims=True)
        idx = jnp.min(jnp.where(rest == mx, eio, float(N_EXPERTS)), axis=0, keepdims=True)
        oh = eio == idx
        onehots.append(oh)
        vals.append(mx)
        rest = jnp.where(oh, -jnp.inf, rest)
    exps = [jnp.exp(vk - vals[0]) for vk in vals]
    denom = exps[0] + exps[1] + exps[2] + exps[3]

    oh_all = jnp.zeros(logits.shape, F32)
    for oh in onehots:
        oh_all = oh_all + oh.astype(F32)
    cnt = jnp.sum(oh_all, axis=1, keepdims=True)
    n_al = jnp.ceil(cnt * (1.0 / SEG_ALIGN)) * SEG_ALIGN
    seg_off = _dot(lstrict_ref[...], jnp.broadcast_to(n_al, (N_EXPERTS, LANES)).astype(BF16))[:, 0:1]
    rank = _dot(oh_all.astype(BF16), ustrict_ref[...])
    base = seg_off + rank
    for kk in range(TOP_K):
        slot = jnp.sum(jnp.where(onehots[kk], base, 0.0), axis=0, keepdims=True)
        slot_ref[0, kk:kk + 1, :] = slot.astype(I32)
        wts_ref[0, kk:kk + 1, :] = exps[kk] / denom
    slot_ref[0, TOP_K:, :] = jnp.full((8 - TOP_K, tm), -1, I32)
    wts_ref[0, TOP_K:, :] = jnp.zeros((8 - TOP_K, tm), F32)
    cnt_ref[0] = jnp.broadcast_to(cnt, (N_EXPERTS, LANES))


def _router(x2, shift, scale, nw, w_router, b_router):
    b, l, d = x2.shape
    tm = ROUTE_TILE
    nl = l // tm
    nt = b * nl
    s = np.arange(tm)
    ustrict = jnp.asarray((s[:, None] < s[None, :]).astype(np.float32), BF16)
    e = np.arange(N_EXPERTS)
    lstrict = jnp.asarray((e[None, :] < e[:, None]).astype(np.float32), BF16)
    w_cat = jnp.pad(jnp.concatenate(_split_bf16(w_router), axis=1), ((0, 0), (0, LANES - 2 * N_EXPERTS)))
    vec = pl.BlockSpec((1, 1, d), lambda bi, i: (bi, 0, 0))
    tile = lambda r, c: pl.BlockSpec((1, r, c), lambda bi, i: (bi * nl + i, 0, 0))
    return pl.pallas_call(
        _router_kernel,
        grid=(b, nl),
        in_specs=[pl.BlockSpec((1, tm, d), lambda bi, i: (bi, i, 0)), vec, vec, _full((1, d)),
                  _full((d, LANES)), _full((N_EXPERTS, 1)), _full((tm, tm)),
                  _full((N_EXPERTS, N_EXPERTS))],
        out_specs=[pl.BlockSpec((1, tm, d), lambda bi, i: (bi, i, 0)),
                   tile(8, tm), tile(8, tm), tile(N_EXPERTS, LANES)],
        out_shape=[jax.ShapeDtypeStruct((b, l, d), BF16),
                   jax.ShapeDtypeStruct((nt, 8, tm), I32),
                   jax.ShapeDtypeStruct((nt, 8, tm), F32),
                   jax.ShapeDtypeStruct((nt, N_EXPERTS, LANES), F32)],
        compiler_params=_params(("arbitrary", "arbitrary")),
        name="router",
    )(x2, shift, scale, nw, w_cat, b_router.reshape(N_EXPERTS, 1), ustrict, lstrict)


def _segment_starts(tile, n_s, off_s, dst_s, make_copy):
    def body(e, _):
        idx = tile * N_EXPERTS + e
        n = n_s[idx]

        @pl.when(n > 0)
        def _():
            make_copy(pl.multiple_of(off_s[idx], SEG_ALIGN), pl.multiple_of(dst_s[idx], SEG_ALIGN),
                      pl.multiple_of(n, SEG_ALIGN)).start()

        return 0

    lax.fori_loop(0, N_EXPERTS, body, 0)


def _segment_wait(tile, n_s, make_copy):
    total = n_s[pl.num_programs(0) * N_EXPERTS + tile]
    make_copy(0, 0, pl.multiple_of(total, SEG_ALIGN)).wait()


def _zero_fill(tail_s, xs_hbm, zero_ref, sem, action):
    def make_copy(dst, sz):
        return pltpu.make_async_copy(zero_ref.at[pl.ds(0, sz)], xs_hbm.at[pl.ds(dst, sz)], sem)

    def tail_body(e, _):
        n = tail_s[N_EXPERTS + e]

        @pl.when(n > 0)
        def _():
            action(make_copy(pl.multiple_of(tail_s[e], SEG_ALIGN), pl.multiple_of(n, SEG_ALIGN)))

        return 0

    lax.fori_loop(0, N_EXPERTS, tail_body, 0)

    def block_body(blk, _):
        action(make_copy(pl.multiple_of(blk * EXPERT_BLOCK, EXPERT_BLOCK), EXPERT_BLOCK))
        return 0

    lax.fori_loop(tail_s[2 * N_EXPERTS], tail_s[2 * N_EXPERTS + 1], block_body, 0)


ROUTE_CHUNK = 512


def _dispatch_kernel(n_s, off_s, dst_s, tail_s, xn_ref, slot_ref, xs_hbm, g_ref, zero_ref, sems):
    tile = pl.program_id(0)
    last = pl.num_programs(0) - 1
    cur = tile % 2

    def copier(of_tile):
        par = of_tile % 2
        return lambda src, dst, n: pltpu.make_async_copy(
            g_ref.at[par, pl.ds(src, n)], xs_hbm.at[pl.ds(dst, n)], sems.at[par])

    @pl.when(tile >= 2)
    def _():
        _segment_wait(tile - 2, n_s, copier(tile - 2))

    x = xn_ref[...]
    tm = x.shape[0]
    slot = slot_ref[0]
    for ci in range(SEG_PAD // ROUTE_CHUNK):
        rio = lax.broadcasted_iota(I32, (ROUTE_CHUNK, tm), 0) + ci * ROUTE_CHUNK
        sel = jnp.zeros((ROUTE_CHUNK, tm), F32)
        for kk in range(TOP_K):
            sel = jnp.where(rio == slot[kk:kk + 1, :], 1.0, sel)
        g_ref[cur, ci * ROUTE_CHUNK:(ci + 1) * ROUTE_CHUNK, :] = _dot(sel.astype(BF16), x).astype(BF16)

    _segment_starts(tile, n_s, off_s, dst_s, copier(tile))

    @pl.when(tile == last)
    def _():
        @pl.when(tile >= 1)
        def _():
            _segment_wait(tile - 1, n_s, copier(tile - 1))

        _segment_wait(tile, n_s, copier(tile))
        zero_ref[...] = jnp.zeros(zero_ref.shape, BF16)
        _zero_fill(tail_s, xs_hbm, zero_ref, sems.at[2], lambda cp: cp.start())
        _zero_fill(tail_s, xs_hbm, zero_ref, sems.at[2], lambda cp: cp.wait())


def _dispatch(xn2, slot, n_flat, off_flat, dst_flat, tail_flat, rows_total):
    t, d = xn2.shape
    tm = ROUTE_TILE
    nt = t // tm
    return pl.pallas_call(
        _dispatch_kernel,
        grid_spec=pltpu.PrefetchScalarGridSpec(
            num_scalar_prefetch=4,
            grid=(nt,),
            in_specs=[pl.BlockSpec((tm, d), lambda i, *_: (i, 0)),
                      pl.BlockSpec((1, 8, tm), lambda i, *_: (i, 0, 0))],
            out_specs=pl.BlockSpec(memory_space=pl.ANY),
            scratch_shapes=[pltpu.VMEM((2, SEG_PAD, d), BF16),
                            pltpu.VMEM((EXPERT_BLOCK, d), BF16),
                            pltpu.SemaphoreType.DMA((3,))]),
        out_shape=jax.ShapeDtypeStruct((rows_total, d), BF16),
        compiler_params=_params(("arbitrary",)),
        name="dispatch",
    )(n_flat, off_flat, dst_flat, tail_flat, xn2, slot)


def _combine_kernel(n_s, off_s, dst_s, out_hbm, slott_ref, wtst_ref, x2_ref, g2_ref, fw_ref, y_ref,
                    buf_ref, sel_ref, sems):
    tile = pl.program_id(0)
    cur = tile % 2

    def copier(of_tile):
        par = of_tile % 2
        return lambda seg, src, n: pltpu.make_async_copy(
            out_hbm.at[pl.ds(src, n)], buf_ref.at[par, pl.ds(seg, n)], sems.at[par])

    @pl.when(tile == 0)
    def _():
        buf_ref[...] = jnp.zeros(buf_ref.shape, BF16)
        _segment_starts(tile, n_s, off_s, dst_s, copier(tile))

    @pl.when(tile + 1 < pl.num_programs(0))
    def _():
        _segment_starts(tile + 1, n_s, off_s, dst_s, copier(tile + 1))

    _segment_wait(tile, n_s, copier(tile))

    st = slott_ref[...]
    wt = wtst_ref[...]
    tm = st.shape[0]
    for ci in range(SEG_PAD // ROUTE_CHUNK):
        lio = lax.broadcasted_iota(I32, (tm, ROUTE_CHUNK), 1) + ci * ROUTE_CHUNK
        sel = jnp.zeros((tm, ROUTE_CHUNK), F32)
        for kk in range(TOP_K):
            sel = jnp.where(lio == st[:, kk:kk + 1], wt[:, kk:kk + 1], sel)
        sel_ref[:, ci * ROUTE_CHUNK:(ci + 1) * ROUTE_CHUNK] = sel.astype(BF16)
    x3 = x2_ref[...] + g2_ref[0] * _dot(sel_ref[...], buf_ref[cur])
    y_ref[...] = x3 * lax.rsqrt(jnp.mean(x3 * x3, axis=-1, keepdims=True) + EPS) * fw_ref[...]


def _combine(out_rows, slot_t, wts_t, x2_flat, g2, final_w, n_flat, off_flat, dst_flat, tiles_per_batch):
    t, d = x2_flat.shape
    tm = ROUTE_TILE
    nt = t // tm
    return pl.pallas_call(
        _combine_kernel,
        grid_spec=pltpu.PrefetchScalarGridSpec(
            num_scalar_prefetch=3,
            grid=(nt,),
            in_specs=[pl.BlockSpec(memory_space=pl.ANY),
                      pl.BlockSpec((tm, 8), lambda i, *_: (i, 0)),
                      pl.BlockSpec((tm, 8), lambda i, *_: (i, 0)),
                      pl.BlockSpec((tm, d), lambda i, *_: (i, 0)),
                      pl.BlockSpec((1, 1, d), lambda i, *_: (i // tiles_per_batch, 0, 0)),
                      pl.BlockSpec((1, d), lambda i, *_: (0, 0))],
            out_specs=pl.BlockSpec((tm, d), lambda i, *_: (i, 0)),
            scratch_shapes=[pltpu.VMEM((2, SEG_PAD, d), BF16), pltpu.VMEM((tm, SEG_PAD), BF16),
                            pltpu.SemaphoreType.DMA((2,))]),
        out_shape=jax.ShapeDtypeStruct((t, d), F32),
        compiler_params=_params(("arbitrary",)),
        name="combine",
    )(n_flat, off_flat, dst_flat, out_rows, slot_t, wts_t, x2_flat, g2, final_w.reshape(1, d))


def _expert_kernel(blk_e, nb_used, xs_ref, w1_ref, b1_ref, w2_ref, b2_ref, o_ref, w1b_ref, w2b_ref):
    i = pl.program_id(0)
    used = i < nb_used[0]

    @pl.when(jnp.logical_not(used))
    def _():
        o_ref[...] = jnp.zeros(o_ref.shape, BF16)

    @pl.when(used & ((i == 0) | (blk_e[i] != blk_e[jnp.maximum(i - 1, 0)])))
    def _():
        w1b_ref[...] = w1_ref[0].astype(BF16)
        w2b_ref[...] = w2_ref[0].astype(BF16)

    @pl.when(used)
    def _():
        gu = _dot(xs_ref[...], w1b_ref[...]) + b1_ref[0]
        gate = jnp.minimum(gu[:, :D_FF], SWIGLU_LIMIT)
        up = jnp.clip(gu[:, D_FF:], -SWIGLU_LIMIT, SWIGLU_LIMIT)
        act = (up + 1.0) * gate * jax.nn.sigmoid(SWIGLU_ALPHA * gate)
        o_ref[...] = (_dot(act.astype(BF16), w2b_ref[...]) + b2_ref[0]).astype(BF16)


def _expert(xs, blk_e, nb_used, w1, b1, w2, b2):
    rows = xs.shape[0]
    nb = rows // EXPERT_BLOCK
    row_blk = lambda i, be, nu: (jnp.maximum(jnp.minimum(i, nu[0] - 1), 0), 0)
    per_e = lambda i, be, nu: (be[i], 0, 0)
    return pl.pallas_call(
        _expert_kernel,
        grid_spec=pltpu.PrefetchScalarGridSpec(
            num_scalar_prefetch=2,
            grid=(nb,),
            in_specs=[pl.BlockSpec((EXPERT_BLOCK, D_MODEL), row_blk),
                      pl.BlockSpec((1, D_MODEL, 2 * D_FF), per_e),
                      pl.BlockSpec((1, 1, 2 * D_FF), per_e),
                      pl.BlockSpec((1, D_FF, D_MODEL), per_e),
                      pl.BlockSpec((1, 1, D_MODEL), per_e)],
            out_specs=pl.BlockSpec((EXPERT_BLOCK, D_MODEL), lambda i, be, nu: (i, 0)),
            scratch_shapes=[pltpu.VMEM((D_MODEL, 2 * D_FF), BF16), pltpu.VMEM((D_FF, D_MODEL), BF16)]),
        out_shape=jax.ShapeDtypeStruct((rows, D_MODEL), BF16),
        compiler_params=_params(("arbitrary",)),
        name="expert",
    )(blk_e, nb_used, xs, w1, b1.reshape(N_EXPERTS, 1, 2 * D_FF), w2, b2.reshape(N_EXPERTS, 1, D_MODEL))


def _routing_tables(cnt, nb):
    cnt = cnt.astype(I32)
    n_al = (cnt + SEG_ALIGN - 1) // SEG_ALIGN * SEG_ALIGN
    seg_off = jnp.cumsum(n_al, axis=1) - n_al
    rel = jnp.cumsum(n_al, axis=0) - n_al
    tot = jnp.sum(n_al, axis=0)
    blocks_e = (tot + EXPERT_BLOCK - 1) // EXPERT_BLOCK
    blk_end = jnp.cumsum(blocks_e)
    e_start = (blk_end - blocks_e) * EXPERT_BLOCK
    dst = e_start[None, :] + rel
    blk_e = jnp.minimum(jnp.sum(blk_end[None, :] <= jnp.arange(nb, dtype=I32)[:, None], axis=1),
                        N_EXPERTS - 1).astype(I32)
    nb_used = blk_end[-1:].astype(I32)
    tails = jnp.concatenate([e_start + tot, blocks_e * EXPERT_BLOCK - tot, nb_used,
                             jnp.full((1,), nb, I32)]).astype(I32)
    counts = jnp.concatenate([n_al.reshape(-1), jnp.sum(n_al, axis=1)])
    return (counts, seg_off.reshape(-1).astype(I32), dst.reshape(-1).astype(I32), blk_e, nb_used, tails)


def kernel(x, c, ctx, c_ctx, w_ada, b_ada, norm1_w, w_in, gate_b, conv_w, conv_b, w_pool, pool_scale,
           hnorm_w, w_bp, w_bm, w_out, norm2_w, w_router, b_router, w1, b1, w2, b2, final_norm_w):
    assert w_ada.shape[0] == 1, "single-layer kernel"
    b, l, d = x.shape
    lc = ctx.shape[1]
    assert d == D_MODEL and l % ROUTE_TILE == 0 and l % GRID_W == 0 and lc % CHUNK == 0
    t = b * l

    rows = (b + 1 + 7) // 8 * 8
    cc = jnp.zeros((rows, d), F32).at[:b].set(c).at[b].set(c_ctx)
    mod = _ada(cc, w_ada[0], b_ada[0])
    sh1, s1, g1, sh2, s2, g2 = [mod[:b, i * d:(i + 1) * d].reshape(b, 1, d) for i in range(6)]
    csh1, cs1 = [jnp.broadcast_to(mod[b, i * d:(i + 1) * d].reshape(1, 1, d), (b, 1, d)) for i in range(2)]

    w_in0 = w_in[0]
    off_gate = POOL_WIDTH + 4 * M_WIDTH
    wmain = w_in0[:, :off_gate].astype(BF16)
    order = [4 * (2 * dirn + is_f) + HEADS_PER_STEP * pair + j
             for pair in range(M_HEADS // HEADS_PER_STEP) for is_f in range(2) for dirn in range(2)
             for j in range(HEADS_PER_STEP)]
    wg_hi, wg_lo = _split_bf16(w_in0[:, off_gate:off_gate + N_GATE][:, order])
    wgate = jnp.pad(jnp.concatenate([wg_hi, wg_lo], axis=1), ((0, 0), (0, LANES - 2 * N_GATE)))
    wmerge = w_in0[:, off_gate + N_GATE:].astype(BF16)
    gbias = jnp.pad(gate_b[0][jnp.asarray(order)], (0, LANES - N_GATE)).reshape(1, LANES)
    proj_consts = (norm1_w[0].reshape(1, d), gbias, wmain, wgate, wmerge, w_pool[0].astype(BF16),
                   pool_scale[0].reshape(1, POOL_WIDTH), w_bp[0].astype(BF16))

    qk_raw, vt, og, gate_rows, bra, gb = _inproj(x, sh1, s1, *proj_consts)
    qk_raw_c, vt_c, _, gate_rows_c, _, _ = _inproj(ctx, csh1, cs1, *proj_consts)
    qt, k = _conv(qk_raw, conv_w[0], conv_b[0])
    _, k_c = _conv(qk_raw_c, conv_w[0], conv_b[0])
    hn = _mlstm(qt, k, vt, gate_rows, k_c, vt_c, gate_rows_c)
    x2 = _merge(x, hn, og, bra, gb, g1, hnorm_w[0], w_bm[0].astype(BF16), w_out[0].astype(BF16))

    xn2, slot, wts, cnt = _router(x2, sh2, s2, norm2_w[0].reshape(1, d), w_router[0], b_router[0])
    nt = t // ROUTE_TILE
    nb = (t * TOP_K + nt * N_EXPERTS * (SEG_ALIGN - 1)) // EXPERT_BLOCK + N_EXPERTS
    n_flat, off_flat, dst_flat, blk_e, nb_used, tails = _routing_tables(cnt[:, :, 0], nb)
    xs = _dispatch(xn2.reshape(t, d), slot, n_flat, off_flat, dst_flat, tails, nb * EXPERT_BLOCK)
    out_rows = _expert(xs, blk_e, nb_used, w1[0], b1[0], w2[0], b2[0])
    slot_t = slot.transpose(0, 2, 1).reshape(t, 8)
    wts_t = wts.transpose(0, 2, 1).reshape(t, 8)
    y = _combine(out_rows, slot_t, wts_t, x2.reshape(t, d), g2, final_norm_w, n_flat, off_flat, dst_flat,
                 l // ROUTE_TILE)
    return y.reshape(b, l, d)
```

```python
import functools

import numpy as np
import jax
import jax.numpy as jnp
from jax import lax
from jax.experimental import pallas as pl
from jax.experimental.pallas import tpu as pltpu

F32 = jnp.float32
BF16 = jnp.bfloat16
I32 = jnp.int32
HIGHEST = lax.Precision.HIGHEST

D_MODEL = 1024
EPS = 1e-6
GRID_W = 64
POOL_WINDOWS = (2, 4, 8, 16)
POOL_WIDTH = 512
POOL_GC = 128
M_HEADS = 4
M_HEAD_DIM = 128
M_WIDTH = 512
CHUNK = 128
N_GATE = 16
N_EXPERTS = 32
TOP_K = 4
D_FF = 1024
SWIGLU_LIMIT = 7.0
SWIGLU_ALPHA = 1.702

LANES = 128
PROJ_TILE = 512
POOL_SLAB = 256
ROUTE_TILE = 512
SEG_ALIGN = 16
SEG_PAD = TOP_K * ROUTE_TILE + N_EXPERTS * SEG_ALIGN
EXPERT_BLOCK = 512
VMEM_LIMIT = 56 * 1024 * 1024


def _dot(a, b, precision=None):
    return jnp.dot(a, b, preferred_element_type=F32, precision=precision)


def _params(semantics):
    return pltpu.CompilerParams(dimension_semantics=semantics, vmem_limit_bytes=VMEM_LIMIT)


def _full(shape):
    nd = len(shape)
    return pl.BlockSpec(shape, lambda *_: (0,) * nd, pipeline_mode=pl.Buffered(1))


def _split_bf16(a):
    hi = a.astype(BF16)
    return hi, (a - hi.astype(F32)).astype(BF16)


def _ada_kernel(c_ref, w_ref, b_ref, o_ref):
    c = c_ref[...]
    s = c * jax.nn.sigmoid(c)
    o_ref[...] = _dot(s, w_ref[...], HIGHEST) + b_ref[...]


def _ada(cc, w_ada, b_ada):
    rows, d = cc.shape
    n = w_ada.shape[1]
    tn = 1024
    return pl.pallas_call(
        _ada_kernel,
        grid=(n // tn,),
        in_specs=[pl.BlockSpec((rows, d), lambda j: (0, 0)),
                  pl.BlockSpec((d, tn), lambda j: (0, j)),
                  pl.BlockSpec((1, tn), lambda j: (0, j))],
        out_specs=pl.BlockSpec((rows, tn), lambda j: (0, j)),
        out_shape=jax.ShapeDtypeStruct((rows, n), F32),
        compiler_params=_params(("arbitrary",)),
        name="ada",
    )(cc, w_ada, b_ada.reshape(1, n))


def _pool_constants(tm):
    pos = np.arange(tm) % GRID_W
    row = np.arange(tm) // GRID_W
    slab = min(tm, POOL_SLAB)
    pm = np.zeros((len(POOL_WINDOWS), slab, slab), np.float32)
    inv = np.zeros((tm, POOL_WIDTH), np.float32)
    for g, win in enumerate(POOL_WINDOWS):
        lo = np.clip(pos - win // 2, 0, GRID_W)
        hi = np.clip(pos + win // 2, 0, GRID_W)
        same = row[:, None] == row[None, :]
        full = (same & (pos[None, :] >= lo[:, None]) & (pos[None, :] < hi[:, None])).astype(np.float32)
        pm[g] = full[:slab, :slab]
        inv[:, g * POOL_GC:(g + 1) * POOL_GC] = (1.0 / (hi - lo).astype(np.float32))[:, None]
    t = np.arange(tm)
    same_chunk = (t[:, None] // CHUNK) == (t[None, :] // CHUNK)
    tl = (same_chunk & (t[None, :] <= t[:, None])).astype(np.float32)
    tu = (same_chunk & (t[None, :] >= t[:, None])).astype(np.float32)
    return pm, inv, tl, tu


def _modulated_norm(x, nw, scale, shift):
    xn = x * lax.rsqrt(jnp.mean(x * x, axis=-1, keepdims=True) + EPS) * nw
    return xn * (1.0 + scale) + shift


def _inproj_kernel(x_ref, sh_ref, sc_ref, nw_ref, gbias_ref, wqkvo_ref, wgate_ref, tl_ref, tu_ref,
                   qk_ref, vt_ref, og_ref, row_ref):
    xm = _modulated_norm(x_ref[0], nw_ref[...], sc_ref[0], sh_ref[0])
    xb = xm.astype(BF16)

    xlo = (xm - xb.astype(F32)).astype(BF16)
    wg = wgate_ref[...]
    r_hi = _dot(xb, wg)
    g = r_hi + pltpu.roll(r_hi, LANES - N_GATE, axis=1) + _dot(xlo, wg) + gbias_ref[...]
    lf = jnp.minimum(g, 0.0) - jnp.log1p(jnp.exp(-jnp.abs(g)))
    lane = lax.broadcasted_iota(I32, g.shape, 1)
    lf_hi = lf.astype(BF16).astype(F32)
    lf2 = jnp.where(lane < N_GATE, lf_hi, pltpu.roll(lf - lf_hi, N_GATE, axis=1)).astype(BF16)
    r_pre = _dot(tl_ref[...], lf2)
    r_suf = _dot(tu_ref[...], lf2)
    b_pre = r_pre + pltpu.roll(r_pre, LANES - N_GATE, axis=1)
    b_suf = r_suf + pltpu.roll(r_suf, LANES - N_GATE, axis=1)
    kind = (lane & 7) >> 1
    col = jnp.where(kind == 2, b_pre, jnp.where(kind == 3, b_suf, g))
    row_ref[0] = col.T[:N_GATE]

    qk_ref[0] = _dot(xb, wqkvo_ref[:, 0:2 * M_WIDTH]).astype(BF16)
    v = _dot(xb, wqkvo_ref[:, 2 * M_WIDTH:3 * M_WIDTH])
    for ci in range(v.shape[0] // CHUNK):
        vt_ref[0, ci] = v[ci * CHUNK:(ci + 1) * CHUNK].T.astype(BF16)
    og_ref[0] = jax.nn.sigmoid(_dot(xb, wqkvo_ref[:, 3 * M_WIDTH:4 * M_WIDTH])).astype(BF16)


def _chunk_t_out(b, l, tm):
    return (pl.BlockSpec((1, tm // CHUNK, M_WIDTH, CHUNK), lambda bi, i: (bi, i, 0, 0)),
            jax.ShapeDtypeStruct((b, l // CHUNK, M_WIDTH, CHUNK), BF16))


def _inproj(x, shift, scale, nw, gbias, wqkvo, wgate):
    b, l, d = x.shape
    tm = min(PROJ_TILE, l)
    _, _, tl, tu = _pool_constants(tm)
    tok = lambda width: pl.BlockSpec((1, tm, width), lambda bi, i: (bi, i, 0))
    vec = pl.BlockSpec((1, 1, d), lambda bi, i: (bi, 0, 0))
    consts = [nw, gbias, wqkvo, wgate, jnp.asarray(tl, BF16), jnp.asarray(tu, BF16)]
    tok_out = lambda width: (tok(width), jax.ShapeDtypeStruct((b, l, width), BF16))
    gate_out = (pl.BlockSpec((1, N_GATE, tm), lambda bi, i: (bi, 0, i)),
                jax.ShapeDtypeStruct((b, N_GATE, l), F32))
    outs = [tok_out(2 * M_WIDTH), _chunk_t_out(b, l, tm), tok_out(M_WIDTH), gate_out]
    return pl.pallas_call(
        _inproj_kernel,
        grid=(b, l // tm),
        in_specs=[tok(d), vec, vec] + [_full(c.shape) for c in consts],
        out_specs=[spec for spec, _ in outs],
        out_shape=[shape for _, shape in outs],
        compiler_params=_params(("arbitrary", "arbitrary")),
        name="inproj",
    )(x, shift, scale, *consts)


def _conv_kernel(x_ref, prev_ref, next_ref, w_ref, b_ref, qt_ref, k_ref):
    i = pl.program_id(1)
    last = pl.num_programs(1) - 1
    x = x_ref[0].astype(F32)
    tc = x.shape[0]
    prev_row = jnp.where(i > 0, prev_ref[0].astype(F32)[SEG_ALIGN - 1:SEG_ALIGN], 0.0)
    next_row = jnp.where(i < last, next_ref[0].astype(F32)[0:1], 0.0)
    rio = lax.broadcasted_iota(I32, x.shape, 0)
    x_prev = jnp.where(rio == 0, prev_row, pltpu.roll(x, 1, axis=0))
    x_next = jnp.where(rio == tc - 1, next_row, pltpu.roll(x, tc - 1, axis=0))
    w = w_ref[...]
    acc = b_ref[...] + x_prev * w[0:1] + x * w[1:2] + x_next * w[2:3]
    y = acc * jax.nn.sigmoid(acc)
    q = y[:, :M_WIDTH] * (M_HEAD_DIM ** -0.5)
    for ci in range(tc // CHUNK):
        qt_ref[0, ci] = q[ci * CHUNK:(ci + 1) * CHUNK].T.astype(BF16)
    k_ref[0] = y[:, M_WIDTH:].astype(BF16)


def _conv(qk_raw, conv_w, conv_b):
    b, l, c = qk_raw.shape
    tc = min(512, l)
    per = tc // SEG_ALIGN
    nblk = l // SEG_ALIGN
    return pl.pallas_call(
        _conv_kernel,
        grid=(b, l // tc),
        in_specs=[pl.BlockSpec((1, tc, c), lambda bi, i: (bi, i, 0)),
                  pl.BlockSpec((1, SEG_ALIGN, c), lambda bi, i: (bi, jnp.maximum(i * per - 1, 0), 0)),
                  pl.BlockSpec((1, SEG_ALIGN, c), lambda bi, i: (bi, jnp.minimum((i + 1) * per, nblk - 1), 0)),
                  _full(conv_w.shape), _full((1, c))],
        out_specs=[_chunk_t_out(b, l, tc)[0], pl.BlockSpec((1, tc, M_WIDTH), lambda bi, i: (bi, i, 0))],
        out_shape=[_chunk_t_out(b, l, tc)[1], jax.ShapeDtypeStruct((b, l, M_WIDTH), BF16)],
        compiler_params=_params(("arbitrary", "arbitrary")),
        name="conv",
    )(qk_raw, qk_raw, qk_raw, conv_w, conv_b.reshape(1, c))


HEADS_PER_STEP = 2
N_CHAINS = 2 * HEADS_PER_STEP
GATE_ROWS = 2 * N_CHAINS


STATE_ROWS = M_HEAD_DIM + SEG_ALIGN


def _lane_pair(rows2):
    return jnp.concatenate([rows2[0:1], rows2[1:2]], axis=1)


def _block_diag(a, b):
    za, zb = jnp.zeros((a.shape[0], b.shape[1]), a.dtype), jnp.zeros((b.shape[0], a.shape[1]), a.dtype)
    return jnp.concatenate([jnp.concatenate([a, za], axis=1), jnp.concatenate([zb, b], axis=1)], axis=0)


def _side_by_side(stacked):
    d = stacked.shape[0] // 2
    return jnp.concatenate([stacked[:d], stacked[d:]], axis=1)


def _state_step(k2, vt2, i_row, b_row, btot, m_old, m_new, state):
    t = k2.shape[0]
    a = jnp.exp(btot - b_row + i_row - m_new)
    cd = jnp.exp(btot + m_old - m_new)
    ones = (lax.broadcasted_iota(I32, (STATE_ROWS - M_HEAD_DIM, 2 * t), 0) == 0).astype(F32)
    vta = (jnp.concatenate([_side_by_side(vt2).astype(F32), ones], axis=0) * a).astype(BF16)
    return cd * state + _dot(vta, _block_diag(k2[:, :M_HEAD_DIM], k2[:, M_HEAD_DIM:]))


def _chunk_out_t(qt2, k2, vt2, i_row, b_row, mask_t, m_old, state):
    t = k2.shape[0]
    g = i_row - b_row
    g_src = jnp.concatenate([jnp.broadcast_to(g[:, :t], (t, t)).T, jnp.broadcast_to(g[:, t:], (t, t)).T],
                            axis=1)
    logw = jnp.where(mask_t, b_row + g_src, -jnp.inf)
    m_inter = b_row + m_old
    m_t = jnp.maximum(jnp.max(logw, axis=0, keepdims=True), m_inter)
    q_bd = _block_diag(qt2[:M_HEAD_DIM], qt2[M_HEAD_DIM:])
    both = _dot(jnp.concatenate([k2, state.astype(BF16)], axis=0), q_bd)
    s = both[:t] * jnp.exp(logw - m_t)
    inter = both[t:]
    decay = jnp.exp(m_inter - m_t)
    s16 = s.astype(BF16)
    num = _dot(_side_by_side(vt2), _block_diag(s16[:, :t], s16[:, t:])) + decay * inter[:M_HEAD_DIM]
    den = jnp.sum(s, axis=0, keepdims=True) + decay * inter[M_HEAD_DIM:M_HEAD_DIM + 1]
    return num * (1.0 / jnp.maximum(jnp.abs(den), jnp.exp(-m_t)))


def _mlstm_kernel(qt_ref, k_ref, vt_ref, row_ref, kc_ref, vtc_ref, rowc_ref, out_ref,
                  h_ref, bt_ref, ma_ref, mold_ref, mnew_ref):
    nc, ncc, t = k_ref.shape[1], kc_ref.shape[1], CHUNK
    src = lax.broadcasted_iota(I32, (t, t), 0)
    tgt = lax.broadcasted_iota(I32, (t, t), 1)
    causal_t = jnp.concatenate([src <= tgt] * HEADS_PER_STEP, axis=1)
    masks_t = (causal_t, jnp.concatenate([src >= tgt] * HEADS_PER_STEP, axis=1))
    grow = pl.ds(pl.multiple_of(pl.program_id(1) * GATE_ROWS, GATE_ROWS), GATE_ROWS)
    head = [slice(j * M_HEAD_DIM, (j + 1) * M_HEAD_DIM) for j in range(HEADS_PER_STEP)]
    dir_rows = [slice(HEADS_PER_STEP * dirn, HEADS_PER_STEP * (dirn + 1)) for dirn in range(2)]

    def gate_pairs(rows, dirn):
        return (_lane_pair(rows[dir_rows[dirn]]),
                _lane_pair(rows[N_CHAINS + HEADS_PER_STEP * dirn:N_CHAINS + HEADS_PER_STEP * (dirn + 1)]))

    def bcast_pair(col2):
        return _lane_pair(jnp.broadcast_to(col2, (HEADS_PER_STEP, t)))

    state = [jnp.zeros((STATE_ROWS, HEADS_PER_STEP * M_HEAD_DIM), F32)] * 2
    m_st = [jnp.zeros((HEADS_PER_STEP, 1), F32)] * 2
    for dirn in range(2):
        for ci in (range(ncc) if dirn == 0 else reversed(range(ncc))):
            rows = rowc_ref[0, ci, grow, :]
            gi2 = rows[dir_rows[dirn]]
            gb2 = rows[N_CHAINS + HEADS_PER_STEP * dirn:N_CHAINS + HEADS_PER_STEP * (dirn + 1)]
            btot = gb2[:, t - 1:t] if dirn == 0 else gb2[:, 0:1]
            m_new = jnp.maximum(btot + m_st[dirn], jnp.max(btot - gb2 + gi2, axis=1, keepdims=True))
            i_row, b_row = gate_pairs(rows, dirn)
            state[dirn] = _state_step(kc_ref[0, ci], vtc_ref[0, ci], i_row, b_row, bcast_pair(btot),
                                      bcast_pair(m_st[dirn]), bcast_pair(m_new), state[dirn])
            m_st[dirn] = m_new

    gates = row_ref[0, :, grow, :]
    gi, gb = gates[:, :N_CHAINS], gates[:, N_CHAINS:]
    is_fwd = lax.broadcasted_iota(I32, gb.shape, 1) < HEADS_PER_STEP
    bt = jnp.where(is_fwd, jnp.broadcast_to(gb[:, :, t - 1:t], gb.shape),
                   jnp.broadcast_to(gb[:, :, 0:1], gb.shape))
    bt_ref[...] = bt
    ma_ref[...] = jnp.broadcast_to(jnp.max(bt - gb + gi, axis=2, keepdims=True), gb.shape)

    def m_scan(s, carry):
        m_f, m_b = carry
        cf, cb = s, nc - 1 - s
        mold_ref[cf, 0:HEADS_PER_STEP, :] = m_f[0:HEADS_PER_STEP]
        m_f = jnp.maximum(bt_ref[cf] + m_f, ma_ref[cf])
        mnew_ref[cf, 0:HEADS_PER_STEP, :] = m_f[0:HEADS_PER_STEP]
        mold_ref[cb, HEADS_PER_STEP:, :] = m_b[HEADS_PER_STEP:]
        m_b = jnp.maximum(bt_ref[cb] + m_b, ma_ref[cb])
        mnew_ref[cb, HEADS_PER_STEP:, :] = m_b[HEADS_PER_STEP:]
        return m_f, m_b

    m0 = jnp.concatenate([jnp.broadcast_to(m, (HEADS_PER_STEP, t)) for m in m_st], axis=0)
    lax.fori_loop(0, nc, m_scan, (m0, m0))

    def advance(ci, dirn, st):
        i_row, b_row = gate_pairs(row_ref[0, ci, grow, :], dirn)
        m_old, m_new, btot = [_lane_pair(tab[ci][dir_rows[dirn]]) for tab in (mold_ref, mnew_ref, bt_ref)]
        qt2, k2, vt2 = qt_ref[0, ci], k_ref[0, ci], vt_ref[0, ci]
        h_ref[dirn, ci] = _chunk_out_t(qt2, k2, vt2, i_row, b_row, masks_t[dirn], m_old, st)
        return _state_step(k2, vt2, i_row, b_row, btot, m_old, m_new, st)

    def body(s, states):
        return advance(s, 0, states[0]), advance(nc - 1 - s, 1, states[1])

    lax.fori_loop(0, nc, body, tuple(state), unroll=2)

    def norm_body(ci, _):
        ht2 = h_ref[0, ci] + h_ref[1, ci]
        for j in range(HEADS_PER_STEP):
            ht = ht2[:, j * t:(j + 1) * t]
            hn = ht * lax.rsqrt(jnp.mean(ht * ht, axis=0, keepdims=True) + EPS)
            out_ref[0, ci, :, head[j]] = hn.T.astype(BF16)
        return 0

    lax.fori_loop(0, nc, norm_body, 0)


def _mlstm(qt, k, vt, gate_rows, kc, vtc, gate_rows_c):
    b, l, _ = k.shape
    lc = kc.shape[1]
    nc, ncc = l // CHUNK, lc // CHUNK
    pairs = M_HEADS // HEADS_PER_STEP
    width = HEADS_PER_STEP * M_HEAD_DIM
    chunked = lambda a, n: a.reshape(b, n, CHUNK, M_WIDTH)
    by_chunk = lambda g, n: g.reshape(b, pairs * GATE_ROWS, n, CHUNK).transpose(0, 2, 1, 3)
    once = pl.Buffered(1)
    seq = lambda n: pl.BlockSpec((1, n, CHUNK, width), lambda bi, h: (bi, 0, 0, h), pipeline_mode=once)
    seq_t = lambda n: pl.BlockSpec((1, n, width, CHUNK), lambda bi, h: (bi, 0, h, 0), pipeline_mode=once)
    rows = lambda n: pl.BlockSpec((1, n, pairs * GATE_ROWS, CHUNK), lambda bi, h: (bi, 0, 0, 0),
                                  pipeline_mode=once)
    table = pltpu.VMEM((nc, N_CHAINS, CHUNK), F32)
    out = pl.pallas_call(
        _mlstm_kernel,
        grid=(b, pairs),
        in_specs=[seq_t(nc), seq(nc), seq_t(nc), rows(nc), seq(ncc), seq_t(ncc), rows(ncc)],
        out_specs=pl.BlockSpec((1, nc, CHUNK, width), lambda bi, h: (bi, 0, 0, h)),
        out_shape=jax.ShapeDtypeStruct((b, nc, CHUNK, M_WIDTH), BF16),
        scratch_shapes=[pltpu.VMEM((2, nc, M_HEAD_DIM, HEADS_PER_STEP * CHUNK), F32),
                        table, table, table, table],
        compiler_params=_params(("arbitrary", "arbitrary")),
        name="mlstm",
    )(qt, chunked(k, nc), vt, by_chunk(gate_rows, nc), chunked(kc, ncc), vtc, by_chunk(gate_rows_c, ncc))
    return out.reshape(b, l, M_WIDTH)


def _merge_kernel(x_ref, hn_ref, og_ref, sh_ref, sc_ref, g1_ref, sh2_ref, sc2_ref, nw_ref, wpin_ref,
                  wmerge_ref, pmat_ref, invc_ref, wpool_ref, pscale_ref, wbp_ref, hw_ref, wbm_ref, wout_ref,
                  nw2_ref, wr_ref, br_ref, ustrict_ref, lstrict_ref,
                  o_ref, xn_ref, slot_ref, wts_ref, cnt_ref):
    x = x_ref[0]
    xb = _modulated_norm(x, nw_ref[...], sc_ref[0], sh_ref[0]).astype(BF16)

    u = _dot(xb, wpin_ref[...])
    u_hi, u_lo = _split_bf16(u)
    invc = invc_ref[...]
    slab = pmat_ref.shape[1]
    ya = []
    for gi in range(len(POOL_WINDOWS)):
        cols = slice(gi * POOL_GC, (gi + 1) * POOL_GC)
        both = jnp.concatenate([u_hi[:, cols], u_lo[:, cols]], axis=1)
        sums = jnp.concatenate([_dot(pmat_ref[gi], both[s:s + slab]) for s in range(0, u.shape[0], slab)],
                               axis=0)
        pooled = (sums[:, :POOL_GC] + sums[:, POOL_GC:]) * invc[:, cols] - u[:, cols]
        ya.append(_dot(pooled.astype(BF16), wpool_ref[gi]))
    ya = jnp.concatenate(ya, axis=1) * pscale_ref[...]
    y = jax.nn.sigmoid(_dot(xb, wmerge_ref[:, 0:D_MODEL])) * _dot(ya.astype(BF16), wbp_ref[...])

    yb = hn_ref[0].astype(F32) * hw_ref[...] * og_ref[0].astype(F32)
    y = y + jax.nn.sigmoid(_dot(xb, wmerge_ref[:, D_MODEL:2 * D_MODEL])) * _dot(yb.astype(BF16), wbm_ref[...])
    x2 = x + g1_ref[0] * _dot(y.astype(BF16), wout_ref[...])
    o_ref[0] = x2

    _route_tile(x2, sh2_ref, sc2_ref, nw2_ref, wr_ref, br_ref, ustrict_ref, lstrict_ref,
                xn_ref, slot_ref, wts_ref, cnt_ref)


def _merge_route(x, hn, og, shift, scale, g1, shift2, scale2, nw, wpin, wmerge, wpool, pscale, wbp, hnorm_w,
                 wbm, wout, nw2, w_router, b_router):
    b, l, d = x.shape
    tm = ROUTE_TILE
    nl = l // tm
    nt = b * nl
    pm, inv, _, _ = _pool_constants(tm)
    tok = lambda width: pl.BlockSpec((1, tm, width), lambda bi, i: (bi, i, 0))
    vec = pl.BlockSpec((1, 1, d), lambda bi, i: (bi, 0, 0))
    tile = lambda r, c: pl.BlockSpec((1, r, c), lambda bi, i: (bi * nl + i, 0, 0))
    consts = [nw, wpin, wmerge, jnp.asarray(pm, BF16), jnp.asarray(inv), wpool, pscale, wbp,
              hnorm_w.reshape(1, M_WIDTH), wbm, wout, nw2] + _router_consts(w_router, b_router)
    return pl.pallas_call(
        _merge_kernel,
        grid=(b, nl),
        in_specs=[tok(d), tok(M_WIDTH), tok(M_WIDTH), vec, vec, vec, vec, vec]
        + [_full(c.shape) for c in consts],
        out_specs=[tok(d), tok(d), tile(8, tm), tile(8, tm), tile(N_EXPERTS, LANES)],
        out_shape=[jax.ShapeDtypeStruct((b, l, d), F32),
                   jax.ShapeDtypeStruct((b, l, d), BF16),
                   jax.ShapeDtypeStruct((nt, 8, tm), I32),
                   jax.ShapeDtypeStruct((nt, 8, tm), F32),
                   jax.ShapeDtypeStruct((nt, N_EXPERTS, LANES), F32)],
        compiler_params=_params(("arbitrary", "arbitrary")),
        name="merge_route",
    )(x, hn, og, shift, scale, g1, shift2, scale2, *consts)


def _route_tile(x, sh_ref, sc_ref, nw_ref, wr_ref, br_ref, ustrict_ref, lstrict_ref,
                xn_ref, slot_ref, wts_ref, cnt_ref):
    tm = x.shape[0]
    xm = _modulated_norm(x, nw_ref[...], sc_ref[0], sh_ref[0])
    xn_ref[0] = xm.astype(BF16)

    x_hi, x_lo = _split_bf16(xm)
    r_hi = _dot(x_hi, wr_ref[...])
    r = r_hi + pltpu.roll(r_hi, LANES - N_EXPERTS, axis=1) + _dot(x_lo, wr_ref[...])
    logits = r.T[:N_EXPERTS] + br_ref[...]
    eio = lax.broadcasted_iota(I32, logits.shape, 0).astype(F32)
    rest = logits
    onehots, vals = [], []
    for _ in range(TOP_K):
        mx = jnp.max(rest, axis=0, keepdims=True)
        idx = jnp.min(jnp.where(rest == mx, eio, float(N_EXPERTS)), axis=0, keepdims=True)
        oh = eio == idx
        onehots.append(oh)
        vals.append(mx)
        rest = jnp.where(oh, -jnp.inf, rest)
    exps = [jnp.exp(vk - vals[0]) for vk in vals]
    denom = exps[0] + exps[1] + exps[2] + exps[3]

    oh_all = jnp.zeros(logits.shape, F32)
    for oh in onehots:
        oh_all = oh_all + oh.astype(F32)
    cnt = jnp.sum(oh_all, axis=1, keepdims=True)
    n_al = jnp.ceil(cnt * (1.0 / SEG_ALIGN)) * SEG_ALIGN
    seg_off = _dot(lstrict_ref[...], jnp.broadcast_to(n_al, (N_EXPERTS, LANES)).astype(BF16))[:, 0:1]
    rank = _dot(oh_all.astype(BF16), ustrict_ref[...])
    base = seg_off + rank
    for kk in range(TOP_K):
        slot = jnp.sum(jnp.where(onehots[kk], base, 0.0), axis=0, keepdims=True)
        slot_ref[0, kk:kk + 1, :] = slot.astype(I32)
        wts_ref[0, kk:kk + 1, :] = exps[kk] / denom
    slot_ref[0, TOP_K:, :] = jnp.full((8 - TOP_K, tm), -1, I32)
    wts_ref[0, TOP_K:, :] = jnp.zeros((8 - TOP_K, tm), F32)
    cnt_ref[0] = jnp.broadcast_to(cnt, (N_EXPERTS, LANES))


def _router_consts(w_router, b_router):
    s = np.arange(ROUTE_TILE)
    ustrict = jnp.asarray((s[:, None] < s[None, :]).astype(np.float32), BF16)
    e = np.arange(N_EXPERTS)
    lstrict = jnp.asarray((e[None, :] < e[:, None]).astype(np.float32), BF16)
    w_cat = jnp.pad(jnp.concatenate(_split_bf16(w_router), axis=1), ((0, 0), (0, LANES - 2 * N_EXPERTS)))
    return [w_cat, b_router.reshape(N_EXPERTS, 1), ustrict, lstrict]


def _segment_starts(tile, n_s, off_s, dst_s, make_copy):
    def body(e, _):
        idx = tile * N_EXPERTS + e
        n = n_s[idx]

        @pl.when(n > 0)
        def _():
            make_copy(pl.multiple_of(off_s[idx], SEG_ALIGN), pl.multiple_of(dst_s[idx], SEG_ALIGN),
                      pl.multiple_of(n, SEG_ALIGN)).start()

        return 0

    lax.fori_loop(0, N_EXPERTS, body, 0)


def _segment_wait(tile, n_s, make_copy):
    total = n_s[pl.num_programs(0) * N_EXPERTS + tile]
    make_copy(0, 0, pl.multiple_of(total, SEG_ALIGN)).wait()


def _zero_fill(tail_s, xs_hbm, zero_ref, sem, action):
    def make_copy(dst, sz):
        return pltpu.make_async_copy(zero_ref.at[pl.ds(0, sz)], xs_hbm.at[pl.ds(dst, sz)], sem)

    def tail_body(e, _):
        n = tail_s[N_EXPERTS + e]

        @pl.when(n > 0)
        def _():
            action(make_copy(pl.multiple_of(tail_s[e], SEG_ALIGN), pl.multiple_of(n, SEG_ALIGN)))

        return 0

    lax.fori_loop(0, N_EXPERTS, tail_body, 0)

    def block_body(blk, _):
        action(make_copy(pl.multiple_of(blk * EXPERT_BLOCK, EXPERT_BLOCK), EXPERT_BLOCK))
        return 0

    lax.fori_loop(tail_s[2 * N_EXPERTS], tail_s[2 * N_EXPERTS + 1], block_body, 0)


ROUTE_CHUNK = 512


def _dispatch_kernel(n_s, off_s, dst_s, tail_s, xn_ref, slot_ref, xs_hbm, g_ref, zero_ref, sems):
    tile = pl.program_id(0)
    last = pl.num_programs(0) - 1
    cur = tile % 2

    def copier(of_tile):
        par = of_tile % 2
        return lambda src, dst, n: pltpu.make_async_copy(
            g_ref.at[par, pl.ds(src, n)], xs_hbm.at[pl.ds(dst, n)], sems.at[par])

    @pl.when(tile >= 2)
    def _():
        _segment_wait(tile - 2, n_s, copier(tile - 2))

    x = xn_ref[...]
    tm = x.shape[0]
    slot = slot_ref[0].astype(jnp.int16)
    rio = lax.broadcasted_iota(I32, (ROUTE_CHUNK, tm), 0).astype(jnp.int16)
    one = jnp.ones((), BF16)
    for ci in range(SEG_PAD // ROUTE_CHUNK):
        rel = slot - ci * ROUTE_CHUNK
        sel = jnp.zeros((ROUTE_CHUNK, tm), BF16)
        for kk in range(TOP_K):
            sel = jnp.where(rio == rel[kk:kk + 1, :], one, sel)
        g_ref[cur, ci * ROUTE_CHUNK:(ci + 1) * ROUTE_CHUNK, :] = _dot(sel, x).astype(BF16)

    _segment_starts(tile, n_s, off_s, dst_s, copier(tile))

    @pl.when(tile == last)
    def _():
        @pl.when(tile >= 1)
        def _():
            _segment_wait(tile - 1, n_s, copier(tile - 1))

        _segment_wait(tile, n_s, copier(tile))
        zero_ref[...] = jnp.zeros(zero_ref.shape, BF16)
        _zero_fill(tail_s, xs_hbm, zero_ref, sems.at[2], lambda cp: cp.start())
        _zero_fill(tail_s, xs_hbm, zero_ref, sems.at[2], lambda cp: cp.wait())


def _dispatch(xn2, slot, n_flat, off_flat, dst_flat, tail_flat, rows_total):
    t, d = xn2.shape
    tm = ROUTE_TILE
    nt = t // tm
    return pl.pallas_call(
        _dispatch_kernel,
        grid_spec=pltpu.PrefetchScalarGridSpec(
            num_scalar_prefetch=4,
            grid=(nt,),
            in_specs=[pl.BlockSpec((tm, d), lambda i, *_: (i, 0)),
                      pl.BlockSpec((1, 8, tm), lambda i, *_: (i, 0, 0))],
            out_specs=pl.BlockSpec(memory_space=pl.ANY),
            scratch_shapes=[pltpu.VMEM((2, SEG_PAD, d), BF16),
                            pltpu.VMEM((EXPERT_BLOCK, d), BF16),
                            pltpu.SemaphoreType.DMA((3,))]),
        out_shape=jax.ShapeDtypeStruct((rows_total, d), BF16),
        compiler_params=_params(("arbitrary",)),
        name="dispatch",
    )(n_flat, off_flat, dst_flat, tail_flat, xn2, slot)


def _combine_kernel(n_s, off_s, dst_s, out_hbm, slott_ref, wtst_ref, x2_ref, g2_ref, fw_ref, y_ref,
                    buf_ref, sel_ref, sems):
    tile = pl.program_id(0)
    cur = tile % 2

    def copier(of_tile):
        par = of_tile % 2
        return lambda seg, src, n: pltpu.make_async_copy(
            out_hbm.at[pl.ds(src, n)], buf_ref.at[par, pl.ds(seg, n)], sems.at[par])

    @pl.when(tile == 0)
    def _():
        buf_ref[...] = jnp.zeros(buf_ref.shape, BF16)
        _segment_starts(tile, n_s, off_s, dst_s, copier(tile))

    @pl.when(tile + 1 < pl.num_programs(0))
    def _():
        _segment_starts(tile + 1, n_s, off_s, dst_s, copier(tile + 1))

    _segment_wait(tile, n_s, copier(tile))

    st = slott_ref[...].astype(jnp.int16)
    wt = wtst_ref[...].astype(BF16)
    tm = st.shape[0]
    lio = lax.broadcasted_iota(I32, (tm, ROUTE_CHUNK), 1).astype(jnp.int16)
    for ci in range(SEG_PAD // ROUTE_CHUNK):
        rel = st - ci * ROUTE_CHUNK
        sel = jnp.zeros((tm, ROUTE_CHUNK), BF16)
        for kk in range(TOP_K):
            sel = jnp.where(lio == rel[:, kk:kk + 1], wt[:, kk:kk + 1], sel)
        sel_ref[:, ci * ROUTE_CHUNK:(ci + 1) * ROUTE_CHUNK] = sel
    x3 = x2_ref[...] + g2_ref[0] * _dot(sel_ref[...], buf_ref[cur])
    y_ref[...] = x3 * lax.rsqrt(jnp.mean(x3 * x3, axis=-1, keepdims=True) + EPS) * fw_ref[...]


def _combine(out_rows, slot_t, wts_t, x2_flat, g2, final_w, n_flat, off_flat, dst_flat, tiles_per_batch):
    t, d = x2_flat.shape
    tm = ROUTE_TILE
    nt = t // tm
    return pl.pallas_call(
        _combine_kernel,
        grid_spec=pltpu.PrefetchScalarGridSpec(
            num_scalar_prefetch=3,
            grid=(nt,),
            in_specs=[pl.BlockSpec(memory_space=pl.ANY),
                      pl.BlockSpec((tm, 8), lambda i, *_: (i, 0)),
                      pl.BlockSpec((tm, 8), lambda i, *_: (i, 0)),
                      pl.BlockSpec((tm, d), lambda i, *_: (i, 0)),
                      pl.BlockSpec((1, 1, d), lambda i, *_: (i // tiles_per_batch, 0, 0)),
                      pl.BlockSpec((1, d), lambda i, *_: (0, 0))],
            out_specs=pl.BlockSpec((tm, d), lambda i, *_: (i, 0)),
            scratch_shapes=[pltpu.VMEM((2, SEG_PAD, d), BF16), pltpu.VMEM((tm, SEG_PAD), BF16),
                            pltpu.SemaphoreType.DMA((2,))]),
        out_shape=jax.ShapeDtypeStruct((t, d), F32),
        compiler_params=_params(("arbitrary",)),
        name="combine",
    )(n_flat, off_flat, dst_flat, out_rows, slot_t, wts_t, x2_flat, g2, final_w.reshape(1, d))


def _expert_kernel(blk_e, nb_used, xs_ref, w1_ref, b1_ref, w2_ref, b2_ref, o_ref, w1b_ref, w2b_ref):
    i = pl.program_id(0)
    used = i < nb_used[0]

    @pl.when(jnp.logical_not(used))
    def _():
        o_ref[...] = jnp.zeros(o_ref.shape, BF16)

    @pl.when(used & ((i == 0) | (blk_e[i] != blk_e[jnp.maximum(i - 1, 0)])))
    def _():
        w1b_ref[...] = w1_ref[0].astype(BF16)
        w2b_ref[...] = w2_ref[0].astype(BF16)

    @pl.when(used)
    def _():
        gu = _dot(xs_ref[...], w1b_ref[...]) + b1_ref[0]
        gate = jnp.minimum(gu[:, :D_FF], SWIGLU_LIMIT)
        up = jnp.clip(gu[:, D_FF:], -SWIGLU_LIMIT, SWIGLU_LIMIT)
        act = (up + 1.0) * gate * jax.nn.sigmoid(SWIGLU_ALPHA * gate)
        o_ref[...] = (_dot(act.astype(BF16), w2b_ref[...]) + b2_ref[0]).astype(BF16)


def _expert(xs, blk_e, nb_used, w1, b1, w2, b2):
    rows = xs.shape[0]
    nb = rows // EXPERT_BLOCK
    row_blk = lambda i, be, nu: (jnp.maximum(jnp.minimum(i, nu[0] - 1), 0), 0)
    per_e = lambda i, be, nu: (be[i], 0, 0)
    return pl.pallas_call(
        _expert_kernel,
        grid_spec=pltpu.PrefetchScalarGridSpec(
            num_scalar_prefetch=2,
            grid=(nb,),
            in_specs=[pl.BlockSpec((EXPERT_BLOCK, D_MODEL), row_blk),
                      pl.BlockSpec((1, D_MODEL, 2 * D_FF), per_e),
                      pl.BlockSpec((1, 1, 2 * D_FF), per_e),
                      pl.BlockSpec((1, D_FF, D_MODEL), per_e),
                      pl.BlockSpec((1, 1, D_MODEL), per_e)],
            out_specs=pl.BlockSpec((EXPERT_BLOCK, D_MODEL), lambda i, be, nu: (i, 0)),
            scratch_shapes=[pltpu.VMEM((D_MODEL, 2 * D_FF), BF16), pltpu.VMEM((D_FF, D_MODEL), BF16)]),
        out_shape=jax.ShapeDtypeStruct((rows, D_MODEL), BF16),
        compiler_params=_params(("arbitrary",)),
        name="expert",
    )(blk_e, nb_used, xs, w1, b1.reshape(N_EXPERTS, 1, 2 * D_FF), w2, b2.reshape(N_EXPERTS, 1, D_MODEL))


def _routing_tables(cnt, nb):
    cnt = cnt.astype(I32)
    n_al = (cnt + SEG_ALIGN - 1) // SEG_ALIGN * SEG_ALIGN
    seg_off = jnp.cumsum(n_al, axis=1) - n_al
    rel = jnp.cumsum(n_al, axis=0) - n_al
    tot = jnp.sum(n_al, axis=0)
    blocks_e = (tot + EXPERT_BLOCK - 1) // EXPERT_BLOCK
    blk_end = jnp.cumsum(blocks_e)
    e_start = (blk_end - blocks_e) * EXPERT_BLOCK
    dst = e_start[None, :] + rel
    blk_e = jnp.minimum(jnp.sum(blk_end[None, :] <= jnp.arange(nb, dtype=I32)[:, None], axis=1),
                        N_EXPERTS - 1).astype(I32)
    nb_used = blk_end[-1:].astype(I32)
    tails = jnp.concatenate([e_start + tot, blocks_e * EXPERT_BLOCK - tot, nb_used,
                             jnp.full((1,), nb, I32)]).astype(I32)
    counts = jnp.concatenate([n_al.reshape(-1), jnp.sum(n_al, axis=1)])
    return (counts, seg_off.reshape(-1).astype(I32), dst.reshape(-1).astype(I32), blk_e, nb_used, tails)


def kernel(x, c, ctx, c_ctx, w_ada, b_ada, norm1_w, w_in, gate_b, conv_w, conv_b, w_pool, pool_scale,
           hnorm_w, w_bp, w_bm, w_out, norm2_w, w_router, b_router, w1, b1, w2, b2, final_norm_w):
    assert w_ada.shape[0] == 1, "single-layer kernel"
    b, l, d = x.shape
    lc = ctx.shape[1]
    assert d == D_MODEL and l % ROUTE_TILE == 0 and l % GRID_W == 0 and lc % CHUNK == 0
    t = b * l

    rows = (b + 1 + 7) // 8 * 8
    cc = jnp.zeros((rows, d), F32).at[:b].set(c).at[b].set(c_ctx)
    mod = _ada(cc, w_ada[0], b_ada[0])
    sh1, s1, g1, sh2, s2, g2 = [mod[:b, i * d:(i + 1) * d].reshape(b, 1, d) for i in range(6)]
    csh1, cs1 = [jnp.broadcast_to(mod[b, i * d:(i + 1) * d].reshape(1, 1, d), (b, 1, d)) for i in range(2)]

    w_in0 = w_in[0]
    off_gate = POOL_WIDTH + 4 * M_WIDTH
    wpin = w_in0[:, :POOL_WIDTH].astype(BF16)
    wqkvo = w_in0[:, POOL_WIDTH:off_gate].astype(BF16)
    order = [4 * (2 * dirn + is_f) + HEADS_PER_STEP * pair + j
             for pair in range(M_HEADS // HEADS_PER_STEP) for is_f in range(2) for dirn in range(2)
             for j in range(HEADS_PER_STEP)]
    wg_hi, wg_lo = _split_bf16(w_in0[:, off_gate:off_gate + N_GATE][:, order])
    wgate = jnp.pad(jnp.concatenate([wg_hi, wg_lo], axis=1), ((0, 0), (0, LANES - 2 * N_GATE)))
    wmerge = w_in0[:, off_gate + N_GATE:].astype(BF16)
    gbias = jnp.pad(gate_b[0][jnp.asarray(order)], (0, LANES - N_GATE)).reshape(1, LANES)
    nw1 = norm1_w[0].reshape(1, d)
    proj_consts = (nw1, gbias, wqkvo, wgate)

    qk_raw, vt, og, gate_rows = _inproj(x, sh1, s1, *proj_consts)
    qk_raw_c, vt_c, _, gate_rows_c = _inproj(ctx, csh1, cs1, *proj_consts)
    qt, k = _conv(qk_raw, conv_w[0], conv_b[0])
    _, k_c = _conv(qk_raw_c, conv_w[0], conv_b[0])
    hn = _mlstm(qt, k, vt, gate_rows, k_c, vt_c, gate_rows_c)
    x2, xn2, slot, wts, cnt = _merge_route(
        x, hn, og, sh1, s1, g1, sh2, s2, nw1, wpin, wmerge, w_pool[0].astype(BF16),
        pool_scale[0].reshape(1, POOL_WIDTH), w_bp[0].astype(BF16), hnorm_w[0], w_bm[0].astype(BF16),
        w_out[0].astype(BF16), norm2_w[0].reshape(1, d), w_router[0], b_router[0])

    nt = t // ROUTE_TILE
    nb = (t * TOP_K + nt * N_EXPERTS * (SEG_ALIGN - 1)) // EXPERT_BLOCK + N_EXPERTS
    n_flat, off_flat, dst_flat, blk_e, nb_used, tails = _routing_tables(cnt[:, :, 0], nb)
    xs = _dispatch(xn2.reshape(t, d), slot, n_flat, off_flat, dst_flat, tails, nb * EXPERT_BLOCK)
    out_rows = _expert(xs, blk_e, nb_used, w1[0], b1[0], w2[0], b2[0])
    slot_t = slot.transpose(0, 2, 1).reshape(t, 8)
    wts_t = wts.transpose(0, 2, 1).reshape(t, 8)
    y = _combine(out_rows, slot_t, wts_t, x2.reshape(t, d), g2, final_norm_w, n_flat, off_flat, dst_flat,
                 l // ROUTE_TILE)
    return y.reshape(b, l, d)
```

```python
import functools

import numpy as np
import jax
import jax.numpy as jnp
from jax import lax
from jax.experimental import pallas as pl
from jax.experimental.pallas import tpu as pltpu

F32 = jnp.float32
BF16 = jnp.bfloat16
I32 = jnp.int32
U32 = jnp.uint32
HIGHEST = lax.Precision.HIGHEST

D_MODEL = 1024
EPS = 1e-6
GRID_W = 64
POOL_WINDOWS = (2, 4, 8, 16)
POOL_WIDTH = 512
POOL_GC = 128
M_HEADS = 4
M_HEAD_DIM = 128
M_WIDTH = 512
CHUNK = 128
N_GATE = 16
N_EXPERTS = 32
TOP_K = 4
D_FF = 1024
SWIGLU_LIMIT = 7.0
SWIGLU_ALPHA = 1.702

LANES = 128
PROJ_TILE = 512
POOL_SLAB = 256
ROUTE_TILE = 512
BF16_ROWS = 16
SEG_ALIGN = 8
SEG_PAD = TOP_K * ROUTE_TILE + N_EXPERTS * SEG_ALIGN
PAIR_WIDTH = D_MODEL // 2
EXPERT_BLOCK = 512
VMEM_LIMIT = 56 * 1024 * 1024


def _dot(a, b, precision=None):
    return jnp.dot(a, b, preferred_element_type=F32, precision=precision)


def _params(semantics):
    return pltpu.CompilerParams(dimension_semantics=semantics, vmem_limit_bytes=VMEM_LIMIT)


def _full(shape):
    nd = len(shape)
    return pl.BlockSpec(shape, lambda *_: (0,) * nd, pipeline_mode=pl.Buffered(1))


def _pack_pairs(v):
    half = v.shape[1] // 2
    lo = lax.bitcast_convert_type(v[:, :half], U32) >> 16
    hi = lax.bitcast_convert_type(v[:, half:], U32) & jnp.uint32(0xFFFF0000)
    return lo | hi


def _unpack_pairs(u):
    lo = lax.bitcast_convert_type(u << 16, F32)
    hi = lax.bitcast_convert_type(u & jnp.uint32(0xFFFF0000), F32)
    return jnp.concatenate([lo, hi], axis=1).astype(BF16)


def _split_bf16(a):
    hi = a.astype(BF16)
    return hi, (a - hi.astype(F32)).astype(BF16)


def _ada_kernel(c_ref, w_ref, b_ref, o_ref):
    c = c_ref[...]
    s = c * jax.nn.sigmoid(c)
    o_ref[...] = _dot(s, w_ref[...], HIGHEST) + b_ref[...]


def _ada(cc, w_ada, b_ada):
    rows, d = cc.shape
    n = w_ada.shape[1]
    tn = 1024
    return pl.pallas_call(
        _ada_kernel,
        grid=(n // tn,),
        in_specs=[pl.BlockSpec((rows, d), lambda j: (0, 0)),
                  pl.BlockSpec((d, tn), lambda j: (0, j)),
                  pl.BlockSpec((1, tn), lambda j: (0, j))],
        out_specs=pl.BlockSpec((rows, tn), lambda j: (0, j)),
        out_shape=jax.ShapeDtypeStruct((rows, n), F32),
        compiler_params=_params(("arbitrary",)),
        name="ada",
    )(cc, w_ada, b_ada.reshape(1, n))


def _pool_constants(tm):
    pos = np.arange(tm) % GRID_W
    row = np.arange(tm) // GRID_W
    slab = min(tm, POOL_SLAB)
    pm = np.zeros((len(POOL_WINDOWS), slab, slab), np.float32)
    inv = np.zeros((tm, POOL_WIDTH), np.float32)
    for g, win in enumerate(POOL_WINDOWS):
        lo = np.clip(pos - win // 2, 0, GRID_W)
        hi = np.clip(pos + win // 2, 0, GRID_W)
        same = row[:, None] == row[None, :]
        full = (same & (pos[None, :] >= lo[:, None]) & (pos[None, :] < hi[:, None])).astype(np.float32)
        pm[g] = full[:slab, :slab]
        inv[:, g * POOL_GC:(g + 1) * POOL_GC] = (1.0 / (hi - lo).astype(np.float32))[:, None]
    t = np.arange(tm)
    same_chunk = (t[:, None] // CHUNK) == (t[None, :] // CHUNK)
    tl = (same_chunk & (t[None, :] <= t[:, None])).astype(np.float32)
    tu = (same_chunk & (t[None, :] >= t[:, None])).astype(np.float32)
    return pm, inv, tl, tu


def _modulated_norm(x, nw, scale, shift):
    xn = x * lax.rsqrt(jnp.mean(x * x, axis=-1, keepdims=True) + EPS) * nw
    return xn * (1.0 + scale) + shift


def _inproj_kernel(x_ref, sh_ref, sc_ref, nw_ref, gbias_ref, wqkvo_ref, wgate_ref, tl_ref, tu_ref,
                   qk_ref, vt_ref, og_ref, row_ref):
    xm = _modulated_norm(x_ref[0], nw_ref[...], sc_ref[0], sh_ref[0])
    xb = xm.astype(BF16)

    xlo = (xm - xb.astype(F32)).astype(BF16)
    wg = wgate_ref[...]
    r_hi = _dot(xb, wg)
    g = r_hi + pltpu.roll(r_hi, LANES - N_GATE, axis=1) + _dot(xlo, wg) + gbias_ref[...]
    lf = jnp.minimum(g, 0.0) - jnp.log1p(jnp.exp(-jnp.abs(g)))
    lane = lax.broadcasted_iota(I32, g.shape, 1)
    lf_hi = lf.astype(BF16).astype(F32)
    lf2 = jnp.where(lane < N_GATE, lf_hi, pltpu.roll(lf - lf_hi, N_GATE, axis=1)).astype(BF16)
    r_pre = _dot(tl_ref[...], lf2)
    r_suf = _dot(tu_ref[...], lf2)
    b_pre = r_pre + pltpu.roll(r_pre, LANES - N_GATE, axis=1)
    b_suf = r_suf + pltpu.roll(r_suf, LANES - N_GATE, axis=1)
    kind = (lane & 7) >> 1
    col = jnp.where(kind == 2, b_pre, jnp.where(kind == 3, b_suf, g))
    row_ref[0] = col.T[:N_GATE]

    qk_ref[0] = _dot(xb, wqkvo_ref[:, 0:2 * M_WIDTH]).astype(BF16)
    v = _dot(xb, wqkvo_ref[:, 2 * M_WIDTH:3 * M_WIDTH])
    for ci in range(v.shape[0] // CHUNK):
        vt_ref[0, ci] = v[ci * CHUNK:(ci + 1) * CHUNK].T.astype(BF16)
    og_ref[0] = jax.nn.sigmoid(_dot(xb, wqkvo_ref[:, 3 * M_WIDTH:4 * M_WIDTH])).astype(BF16)


def _chunk_t_out(b, l, tm):
    return (pl.BlockSpec((1, tm // CHUNK, M_WIDTH, CHUNK), lambda bi, i: (bi, i, 0, 0)),
            jax.ShapeDtypeStruct((b, l // CHUNK, M_WIDTH, CHUNK), BF16))


def _inproj(x, shift, scale, nw, gbias, wqkvo, wgate):
    b, l, d = x.shape
    tm = min(PROJ_TILE, l)
    _, _, tl, tu = _pool_constants(tm)
    tok = lambda width: pl.BlockSpec((1, tm, width), lambda bi, i: (bi, i, 0))
    vec = pl.BlockSpec((1, 1, d), lambda bi, i: (bi, 0, 0))
    consts = [nw, gbias, wqkvo, wgate, jnp.asarray(tl, BF16), jnp.asarray(tu, BF16)]
    tok_out = lambda width: (tok(width), jax.ShapeDtypeStruct((b, l, width), BF16))
    gate_out = (pl.BlockSpec((1, N_GATE, tm), lambda bi, i: (bi, 0, i)),
                jax.ShapeDtypeStruct((b, N_GATE, l), F32))
    outs = [tok_out(2 * M_WIDTH), _chunk_t_out(b, l, tm), tok_out(M_WIDTH), gate_out]
    return pl.pallas_call(
        _inproj_kernel,
        grid=(b, l // tm),
        in_specs=[tok(d), vec, vec] + [_full(c.shape) for c in consts],
        out_specs=[spec for spec, _ in outs],
        out_shape=[shape for _, shape in outs],
        compiler_params=_params(("arbitrary", "arbitrary")),
        name="inproj",
    )(x, shift, scale, *consts)


def _conv_kernel(x_ref, prev_ref, next_ref, w_ref, b_ref, qt_ref, k_ref):
    i = pl.program_id(1)
    last = pl.num_programs(1) - 1
    x = x_ref[0].astype(F32)
    tc = x.shape[0]
    prev_row = jnp.where(i > 0, prev_ref[0].astype(F32)[BF16_ROWS - 1:BF16_ROWS], 0.0)
    next_row = jnp.where(i < last, next_ref[0].astype(F32)[0:1], 0.0)
    rio = lax.broadcasted_iota(I32, x.shape, 0)
    x_prev = jnp.where(rio == 0, prev_row, pltpu.roll(x, 1, axis=0))
    x_next = jnp.where(rio == tc - 1, next_row, pltpu.roll(x, tc - 1, axis=0))
    w = w_ref[...]
    acc = b_ref[...] + x_prev * w[0:1] + x * w[1:2] + x_next * w[2:3]
    y = acc * jax.nn.sigmoid(acc)
    q = y[:, :M_WIDTH] * (M_HEAD_DIM ** -0.5)
    for ci in range(tc // CHUNK):
        qt_ref[0, ci] = q[ci * CHUNK:(ci + 1) * CHUNK].T.astype(BF16)
    k_ref[0] = y[:, M_WIDTH:].astype(BF16)


def _conv(qk_raw, conv_w, conv_b):
    b, l, c = qk_raw.shape
    tc = min(512, l)
    per = tc // BF16_ROWS
    nblk = l // BF16_ROWS
    return pl.pallas_call(
        _conv_kernel,
        grid=(b, l // tc),
        in_specs=[pl.BlockSpec((1, tc, c), lambda bi, i: (bi, i, 0)),
                  pl.BlockSpec((1, BF16_ROWS, c), lambda bi, i: (bi, jnp.maximum(i * per - 1, 0), 0)),
                  pl.BlockSpec((1, BF16_ROWS, c), lambda bi, i: (bi, jnp.minimum((i + 1) * per, nblk - 1), 0)),
                  _full(conv_w.shape), _full((1, c))],
        out_specs=[_chunk_t_out(b, l, tc)[0], pl.BlockSpec((1, tc, M_WIDTH), lambda bi, i: (bi, i, 0))],
        out_shape=[_chunk_t_out(b, l, tc)[1], jax.ShapeDtypeStruct((b, l, M_WIDTH), BF16)],
        compiler_params=_params(("arbitrary", "arbitrary")),
        name="conv",
    )(qk_raw, qk_raw, qk_raw, conv_w, conv_b.reshape(1, c))


HEADS_PER_STEP = 2
N_CHAINS = 2 * HEADS_PER_STEP
GATE_ROWS = 2 * N_CHAINS


STATE_ROWS = M_HEAD_DIM + BF16_ROWS


def _lane_pair(rows2):
    return jnp.concatenate([rows2[0:1], rows2[1:2]], axis=1)


def _block_diag(a, b):
    za, zb = jnp.zeros((a.shape[0], b.shape[1]), a.dtype), jnp.zeros((b.shape[0], a.shape[1]), a.dtype)
    return jnp.concatenate([jnp.concatenate([a, za], axis=1), jnp.concatenate([zb, b], axis=1)], axis=0)


def _side_by_side(stacked):
    d = stacked.shape[0] // 2
    return jnp.concatenate([stacked[:d], stacked[d:]], axis=1)


def _state_step(k2, vt2, i_row, b_row, btot, m_old, m_new, state):
    t = k2.shape[0]
    a = jnp.exp(btot - b_row + i_row - m_new)
    cd = jnp.exp(btot + m_old - m_new)
    ones = (lax.broadcasted_iota(I32, (STATE_ROWS - M_HEAD_DIM, 2 * t), 0) == 0).astype(F32)
    vta = (jnp.concatenate([_side_by_side(vt2).astype(F32), ones], axis=0) * a).astype(BF16)
    return cd * state + _dot(vta, _block_diag(k2[:, :M_HEAD_DIM], k2[:, M_HEAD_DIM:]))


def _chunk_out_t(qt2, k2, vt2, i_row, b_row, mask_t, m_old, state):
    t = k2.shape[0]
    g = i_row - b_row
    g_src = jnp.concatenate([jnp.broadcast_to(g[:, :t], (t, t)).T, jnp.broadcast_to(g[:, t:], (t, t)).T],
                            axis=1)
    logw = jnp.where(mask_t, b_row + g_src, -jnp.inf)
    m_inter = b_row + m_old
    m_t = jnp.maximum(jnp.max(logw, axis=0, keepdims=True), m_inter)
    q_bd = _block_diag(qt2[:M_HEAD_DIM], qt2[M_HEAD_DIM:])
    both = _dot(jnp.concatenate([k2, state.astype(BF16)], axis=0), q_bd)
    s = both[:t] * jnp.exp(logw - m_t)
    inter = both[t:]
    decay = jnp.exp(m_inter - m_t)
    s16 = s.astype(BF16)
    num = _dot(_side_by_side(vt2), _block_diag(s16[:, :t], s16[:, t:])) + decay * inter[:M_HEAD_DIM]
    den = jnp.sum(s, axis=0, keepdims=True) + decay * inter[M_HEAD_DIM:M_HEAD_DIM + 1]
    return num * (1.0 / jnp.maximum(jnp.abs(den), jnp.exp(-m_t)))


def _mlstm_kernel(qt_ref, k_ref, vt_ref, row_ref, kc_ref, vtc_ref, rowc_ref, out_ref,
                  h_ref, bt_ref, ma_ref, mold_ref, mnew_ref):
    nc, ncc, t = k_ref.shape[1], kc_ref.shape[1], CHUNK
    src = lax.broadcasted_iota(I32, (t, t), 0)
    tgt = lax.broadcasted_iota(I32, (t, t), 1)
    causal_t = jnp.concatenate([src <= tgt] * HEADS_PER_STEP, axis=1)
    masks_t = (causal_t, jnp.concatenate([src >= tgt] * HEADS_PER_STEP, axis=1))
    grow = pl.ds(pl.multiple_of(pl.program_id(1) * GATE_ROWS, GATE_ROWS), GATE_ROWS)
    head = [slice(j * M_HEAD_DIM, (j + 1) * M_HEAD_DIM) for j in range(HEADS_PER_STEP)]
    dir_rows = [slice(HEADS_PER_STEP * dirn, HEADS_PER_STEP * (dirn + 1)) for dirn in range(2)]

    def gate_pairs(rows, dirn):
        return (_lane_pair(rows[dir_rows[dirn]]),
                _lane_pair(rows[N_CHAINS + HEADS_PER_STEP * dirn:N_CHAINS + HEADS_PER_STEP * (dirn + 1)]))

    def bcast_pair(col2):
        return _lane_pair(jnp.broadcast_to(col2, (HEADS_PER_STEP, t)))

    state = [jnp.zeros((STATE_ROWS, HEADS_PER_STEP * M_HEAD_DIM), F32)] * 2
    m_st = [jnp.zeros((HEADS_PER_STEP, 1), F32)] * 2
    for dirn in range(2):
        for ci in (range(ncc) if dirn == 0 else reversed(range(ncc))):
            rows = rowc_ref[0, ci, grow, :]
            gi2 = rows[dir_rows[dirn]]
            gb2 = rows[N_CHAINS + HEADS_PER_STEP * dirn:N_CHAINS + HEADS_PER_STEP * (dirn + 1)]
            btot = gb2[:, t - 1:t] if dirn == 0 else gb2[:, 0:1]
            m_new = jnp.maximum(btot + m_st[dirn], jnp.max(btot - gb2 + gi2, axis=1, keepdims=True))
            i_row, b_row = gate_pairs(rows, dirn)
            state[dirn] = _state_step(kc_ref[0, ci], vtc_ref[0, ci], i_row, b_row, bcast_pair(btot),
                                      bcast_pair(m_st[dirn]), bcast_pair(m_new), state[dirn])
            m_st[dirn] = m_new

    gates = row_ref[0, :, grow, :]
    gi, gb = gates[:, :N_CHAINS], gates[:, N_CHAINS:]
    is_fwd = lax.broadcasted_iota(I32, gb.shape, 1) < HEADS_PER_STEP
    bt = jnp.where(is_fwd, jnp.broadcast_to(gb[:, :, t - 1:t], gb.shape),
                   jnp.broadcast_to(gb[:, :, 0:1], gb.shape))
    bt_ref[...] = bt
    ma_ref[...] = jnp.broadcast_to(jnp.max(bt - gb + gi, axis=2, keepdims=True), gb.shape)

    def m_scan(s, carry):
        m_f, m_b = carry
        cf, cb = s, nc - 1 - s
        mold_ref[cf, 0:HEADS_PER_STEP, :] = m_f[0:HEADS_PER_STEP]
        m_f = jnp.maximum(bt_ref[cf] + m_f, ma_ref[cf])
        mnew_ref[cf, 0:HEADS_PER_STEP, :] = m_f[0:HEADS_PER_STEP]
        mold_ref[cb, HEADS_PER_STEP:, :] = m_b[HEADS_PER_STEP:]
        m_b = jnp.maximum(bt_ref[cb] + m_b, ma_ref[cb])
        mnew_ref[cb, HEADS_PER_STEP:, :] = m_b[HEADS_PER_STEP:]
        return m_f, m_b

    m0 = jnp.concatenate([jnp.broadcast_to(m, (HEADS_PER_STEP, t)) for m in m_st], axis=0)
    lax.fori_loop(0, nc, m_scan, (m0, m0))

    def advance(ci, dirn, st):
        i_row, b_row = gate_pairs(row_ref[0, ci, grow, :], dirn)
        m_old, m_new, btot = [_lane_pair(tab[ci][dir_rows[dirn]]) for tab in (mold_ref, mnew_ref, bt_ref)]
        qt2, k2, vt2 = qt_ref[0, ci], k_ref[0, ci], vt_ref[0, ci]
        h_ref[dirn, ci] = _chunk_out_t(qt2, k2, vt2, i_row, b_row, masks_t[dirn], m_old, st)
        return _state_step(k2, vt2, i_row, b_row, btot, m_old, m_new, st)

    def body(s, states):
        return advance(s, 0, states[0]), advance(nc - 1 - s, 1, states[1])

    lax.fori_loop(0, nc, body, tuple(state), unroll=2)

    def norm_body(ci, _):
        ht2 = h_ref[0, ci] + h_ref[1, ci]
        for j in range(HEADS_PER_STEP):
            ht = ht2[:, j * t:(j + 1) * t]
            hn = ht * lax.rsqrt(jnp.mean(ht * ht, axis=0, keepdims=True) + EPS)
            out_ref[0, ci, :, head[j]] = hn.T.astype(BF16)
        return 0

    lax.fori_loop(0, nc, norm_body, 0, unroll=4)


def _mlstm(qt, k, vt, gate_rows, kc, vtc, gate_rows_c):
    b, l, _ = k.shape
    lc = kc.shape[1]
    nc, ncc = l // CHUNK, lc // CHUNK
    pairs = M_HEADS // HEADS_PER_STEP
    width = HEADS_PER_STEP * M_HEAD_DIM
    chunked = lambda a, n: a.reshape(b, n, CHUNK, M_WIDTH)
    by_chunk = lambda g, n: g.reshape(b, pairs * GATE_ROWS, n, CHUNK).transpose(0, 2, 1, 3)
    once = pl.Buffered(1)
    seq = lambda n: pl.BlockSpec((1, n, CHUNK, width), lambda bi, h: (bi, 0, 0, h), pipeline_mode=once)
    seq_t = lambda n: pl.BlockSpec((1, n, width, CHUNK), lambda bi, h: (bi, 0, h, 0), pipeline_mode=once)
    rows = lambda n: pl.BlockSpec((1, n, pairs * GATE_ROWS, CHUNK), lambda bi, h: (bi, 0, 0, 0),
                                  pipeline_mode=once)
    table = pltpu.VMEM((nc, N_CHAINS, CHUNK), F32)
    out = pl.pallas_call(
        _mlstm_kernel,
        grid=(b, pairs),
        in_specs=[seq_t(nc), seq(nc), seq_t(nc), rows(nc), seq(ncc), seq_t(ncc), rows(ncc)],
        out_specs=pl.BlockSpec((1, nc, CHUNK, width), lambda bi, h: (bi, 0, 0, h)),
        out_shape=jax.ShapeDtypeStruct((b, nc, CHUNK, M_WIDTH), BF16),
        scratch_shapes=[pltpu.VMEM((2, nc, M_HEAD_DIM, HEADS_PER_STEP * CHUNK), F32),
                        table, table, table, table],
        compiler_params=_params(("arbitrary", "arbitrary")),
        name="mlstm",
    )(qt, chunked(k, nc), vt, by_chunk(gate_rows, nc), chunked(kc, ncc), vtc, by_chunk(gate_rows_c, ncc))
    return out.reshape(b, l, M_WIDTH)


def _merge_kernel(x_ref, hn_ref, og_ref, sh_ref, sc_ref, g1_ref, sh2_ref, sc2_ref, nw_ref, wpin_ref,
                  wmerge_ref, pmat_ref, invc_ref, wpool_ref, pscale_ref, wbp_ref, hw_ref, wbm_ref, wout_ref,
                  nw2_ref, wr_ref, br_ref, ustrict_ref, lstrict_ref,
                  o_ref, xn_ref, slot_ref, wts_ref, cnt_ref):
    x = x_ref[0]
    xb = _modulated_norm(x, nw_ref[...], sc_ref[0], sh_ref[0]).astype(BF16)

    u = _dot(xb, wpin_ref[...])
    u_hi, u_lo = _split_bf16(u)
    invc = invc_ref[...]
    slab = pmat_ref.shape[1]
    ya = []
    for gi in range(len(POOL_WINDOWS)):
        cols = slice(gi * POOL_GC, (gi + 1) * POOL_GC)
        both = jnp.concatenate([u_hi[:, cols], u_lo[:, cols]], axis=1)
        sums = jnp.concatenate([_dot(pmat_ref[gi], both[s:s + slab]) for s in range(0, u.shape[0], slab)],
                               axis=0)
        pooled = (sums[:, :POOL_GC] + sums[:, POOL_GC:]) * invc[:, cols] - u[:, cols]
        ya.append(_dot(pooled.astype(BF16), wpool_ref[gi]))
    ya = jnp.concatenate(ya, axis=1) * pscale_ref[...]
    y = jax.nn.sigmoid(_dot(xb, wmerge_ref[:, 0:D_MODEL])) * _dot(ya.astype(BF16), wbp_ref[...])

    yb = hn_ref[0].astype(F32) * hw_ref[...] * og_ref[0].astype(F32)
    y = y + jax.nn.sigmoid(_dot(xb, wmerge_ref[:, D_MODEL:2 * D_MODEL])) * _dot(yb.astype(BF16), wbm_ref[...])
    x2 = x + g1_ref[0] * _dot(y.astype(BF16), wout_ref[...])
    o_ref[0] = x2

    _route_tile(x2, sh2_ref, sc2_ref, nw2_ref, wr_ref, br_ref, ustrict_ref, lstrict_ref,
                xn_ref, slot_ref, wts_ref, cnt_ref)


def _merge_route(x, hn, og, shift, scale, g1, shift2, scale2, nw, wpin, wmerge, wpool, pscale, wbp, hnorm_w,
                 wbm, wout, nw2, w_router, b_router):
    b, l, d = x.shape
    tm = ROUTE_TILE
    nl = l // tm
    nt = b * nl
    pm, inv, _, _ = _pool_constants(tm)
    tok = lambda width: pl.BlockSpec((1, tm, width), lambda bi, i: (bi, i, 0))
    vec = pl.BlockSpec((1, 1, d), lambda bi, i: (bi, 0, 0))
    tile = lambda r, c: pl.BlockSpec((1, r, c), lambda bi, i: (bi * nl + i, 0, 0))
    consts = [nw, wpin, wmerge, jnp.asarray(pm, BF16), jnp.asarray(inv), wpool, pscale, wbp,
              hnorm_w.reshape(1, M_WIDTH), wbm, wout, nw2] + _router_consts(w_router, b_router)
    return pl.pallas_call(
        _merge_kernel,
        grid=(b, nl),
        in_specs=[tok(d), tok(M_WIDTH), tok(M_WIDTH), vec, vec, vec, vec, vec]
        + [_full(c.shape) for c in consts],
        out_specs=[tok(d), tok(d), tile(8, tm), tile(8, tm), tile(N_EXPERTS, LANES)],
        out_shape=[jax.ShapeDtypeStruct((b, l, d), F32),
                   jax.ShapeDtypeStruct((b, l, d), BF16),
                   jax.ShapeDtypeStruct((nt, 8, tm), I32),
                   jax.ShapeDtypeStruct((nt, 8, tm), F32),
                   jax.ShapeDtypeStruct((nt, N_EXPERTS, LANES), F32)],
        compiler_params=_params(("arbitrary", "arbitrary")),
        name="merge_route",
    )(x, hn, og, shift, scale, g1, shift2, scale2, *consts)


def _route_tile(x, sh_ref, sc_ref, nw_ref, wr_ref, br_ref, ustrict_ref, lstrict_ref,
                xn_ref, slot_ref, wts_ref, cnt_ref):
    tm = x.shape[0]
    xm = _modulated_norm(x, nw_ref[...], sc_ref[0], sh_ref[0])
    xn_ref[0] = xm.astype(BF16)

    x_hi, x_lo = _split_bf16(xm)
    r_hi = _dot(x_hi, wr_ref[...])
    r = r_hi + pltpu.roll(r_hi, LANES - N_EXPERTS, axis=1) + _dot(x_lo, wr_ref[...])
    logits = r.T[:N_EXPERTS] + br_ref[...]
    eio = lax.broadcasted_iota(I32, logits.shape, 0).astype(F32)
    rest = logits
    onehots, vals = [], []
    for _ in range(TOP_K):
        mx = jnp.max(rest, axis=0, keepdims=True)
        idx = jnp.min(jnp.where(rest == mx, eio, float(N_EXPERTS)), axis=0, keepdims=True)
        oh = eio == idx
        onehots.append(oh)
        vals.append(mx)
        rest = jnp.where(oh, -jnp.inf, rest)
    exps = [jnp.exp(vk - vals[0]) for vk in vals]
    denom = exps[0] + exps[1] + exps[2] + exps[3]

    oh_all = jnp.zeros(logits.shape, F32)
    for oh in onehots:
        oh_all = oh_all + oh.astype(F32)
    cnt = jnp.sum(oh_all, axis=1, keepdims=True)
    n_al = jnp.ceil(cnt * (1.0 / SEG_ALIGN)) * SEG_ALIGN
    seg_off = _dot(lstrict_ref[...], jnp.broadcast_to(n_al, (N_EXPERTS, LANES)).astype(BF16))[:, 0:1]
    rank = _dot(oh_all.astype(BF16), ustrict_ref[...])
    base = seg_off + rank
    for kk in range(TOP_K):
        slot = jnp.sum(jnp.where(onehots[kk], base, 0.0), axis=0, keepdims=True)
        slot_ref[0, kk:kk + 1, :] = slot.astype(I32)
        wts_ref[0, kk:kk + 1, :] = exps[kk] / denom
    slot_ref[0, TOP_K:, :] = jnp.full((8 - TOP_K, tm), -1, I32)
    wts_ref[0, TOP_K:, :] = jnp.zeros((8 - TOP_K, tm), F32)
    cnt_ref[0] = jnp.broadcast_to(cnt, (N_EXPERTS, LANES))


def _router_consts(w_router, b_router):
    s = np.arange(ROUTE_TILE)
    ustrict = jnp.asarray((s[:, None] < s[None, :]).astype(np.float32), BF16)
    e = np.arange(N_EXPERTS)
    lstrict = jnp.asarray((e[None, :] < e[:, None]).astype(np.float32), BF16)
    w_cat = jnp.pad(jnp.concatenate(_split_bf16(w_router), axis=1), ((0, 0), (0, LANES - 2 * N_EXPERTS)))
    return [w_cat, b_router.reshape(N_EXPERTS, 1), ustrict, lstrict]


def _segment_starts(tile, n_s, off_s, dst_s, make_copy):
    def body(e, _):
        idx = tile * N_EXPERTS + e
        n = n_s[idx]

        @pl.when(n > 0)
        def _():
            make_copy(pl.multiple_of(off_s[idx], SEG_ALIGN), pl.multiple_of(dst_s[idx], SEG_ALIGN),
                      pl.multiple_of(n, SEG_ALIGN)).start()

        return 0

    lax.fori_loop(0, N_EXPERTS, body, 0)


def _segment_wait(tile, n_s, make_copy):
    total = n_s[pl.num_programs(0) * N_EXPERTS + tile]
    make_copy(0, 0, pl.multiple_of(total, SEG_ALIGN)).wait()


def _zero_fill(tail_s, xs_hbm, zero_ref, sem, action):
    def make_copy(dst, sz):
        return pltpu.make_async_copy(zero_ref.at[pl.ds(0, sz)], xs_hbm.at[pl.ds(dst, sz)], sem)

    def tail_body(e, _):
        n = tail_s[N_EXPERTS + e]

        @pl.when(n > 0)
        def _():
            action(make_copy(pl.multiple_of(tail_s[e], SEG_ALIGN), pl.multiple_of(n, SEG_ALIGN)))

        return 0

    lax.fori_loop(0, N_EXPERTS, tail_body, 0)

    def block_body(blk, _):
        action(make_copy(pl.multiple_of(blk * EXPERT_BLOCK, EXPERT_BLOCK), EXPERT_BLOCK))
        return 0

    lax.fori_loop(tail_s[2 * N_EXPERTS], tail_s[2 * N_EXPERTS + 1], block_body, 0)


ROUTE_CHUNK = 768
assert SEG_PAD % ROUTE_CHUNK == 0


def _dispatch_kernel(n_s, off_s, dst_s, tail_s, xn_ref, slot_ref, xs_hbm, g_ref, zero_ref, sems):
    tile = pl.program_id(0)
    last = pl.num_programs(0) - 1
    cur = tile % 2

    def copier(of_tile):
        par = of_tile % 2
        return lambda src, dst, n: pltpu.make_async_copy(
            g_ref.at[par, pl.ds(src, n)], xs_hbm.at[pl.ds(dst, n)], sems.at[par])

    @pl.when(tile >= 2)
    def _():
        _segment_wait(tile - 2, n_s, copier(tile - 2))

    x = xn_ref[...]
    tm = x.shape[0]
    slot = slot_ref[0].astype(jnp.int16)
    rio = lax.broadcasted_iota(I32, (ROUTE_CHUNK, tm), 0).astype(jnp.int16)
    one = jnp.ones((), BF16)
    for ci in range(SEG_PAD // ROUTE_CHUNK):
        rel = slot - ci * ROUTE_CHUNK
        sel = jnp.zeros((ROUTE_CHUNK, tm), BF16)
        for kk in range(TOP_K):
            sel = jnp.where(rio == rel[kk:kk + 1, :], one, sel)
        g_ref[cur, ci * ROUTE_CHUNK:(ci + 1) * ROUTE_CHUNK, :] = _pack_pairs(_dot(sel, x))

    _segment_starts(tile, n_s, off_s, dst_s, copier(tile))

    @pl.when(tile == last)
    def _():
        @pl.when(tile >= 1)
        def _():
            _segment_wait(tile - 1, n_s, copier(tile - 1))

        _segment_wait(tile, n_s, copier(tile))
        zero_ref[...] = jnp.zeros(zero_ref.shape, U32)
        _zero_fill(tail_s, xs_hbm, zero_ref, sems.at[2], lambda cp: cp.start())
        _zero_fill(tail_s, xs_hbm, zero_ref, sems.at[2], lambda cp: cp.wait())


def _dispatch(xn2, slot, n_flat, off_flat, dst_flat, tail_flat, rows_total):
    t, d = xn2.shape
    tm = ROUTE_TILE
    nt = t // tm
    return pl.pallas_call(
        _dispatch_kernel,
        grid_spec=pltpu.PrefetchScalarGridSpec(
            num_scalar_prefetch=4,
            grid=(nt,),
            in_specs=[pl.BlockSpec((tm, d), lambda i, *_: (i, 0)),
                      pl.BlockSpec((1, 8, tm), lambda i, *_: (i, 0, 0))],
            out_specs=pl.BlockSpec(memory_space=pl.ANY),
            scratch_shapes=[pltpu.VMEM((2, SEG_PAD, PAIR_WIDTH), U32),
                            pltpu.VMEM((EXPERT_BLOCK, PAIR_WIDTH), U32),
                            pltpu.SemaphoreType.DMA((3,))]),
        out_shape=jax.ShapeDtypeStruct((rows_total, PAIR_WIDTH), U32),
        compiler_params=_params(("arbitrary",)),
        name="dispatch",
    )(n_flat, off_flat, dst_flat, tail_flat, xn2, slot)


def _combine_kernel(n_s, off_s, dst_s, out_hbm, slott_ref, wtst_ref, x2_ref, g2_ref, fw_ref, y_ref,
                    buf_ref, sel_ref, sems):
    tile = pl.program_id(0)
    cur = tile % 2

    def copier(of_tile):
        par = of_tile % 2
        return lambda seg, src, n: pltpu.make_async_copy(
            out_hbm.at[pl.ds(src, n)], buf_ref.at[par, pl.ds(seg, n)], sems.at[par])

    @pl.when(tile == 0)
    def _():
        buf_ref[...] = jnp.zeros(buf_ref.shape, U32)
        _segment_starts(tile, n_s, off_s, dst_s, copier(tile))

    @pl.when(tile + 1 < pl.num_programs(0))
    def _():
        _segment_starts(tile + 1, n_s, off_s, dst_s, copier(tile + 1))

    _segment_wait(tile, n_s, copier(tile))

    st = slott_ref[...].astype(jnp.int16)
    wt = wtst_ref[...].astype(BF16)
    tm = st.shape[0]
    lio = lax.broadcasted_iota(I32, (tm, ROUTE_CHUNK), 1).astype(jnp.int16)
    for ci in range(SEG_PAD // ROUTE_CHUNK):
        rel = st - ci * ROUTE_CHUNK
        sel = jnp.zeros((tm, ROUTE_CHUNK), BF16)
        for kk in range(TOP_K):
            sel = jnp.where(lio == rel[:, kk:kk + 1], wt[:, kk:kk + 1], sel)
        sel_ref[:, ci * ROUTE_CHUNK:(ci + 1) * ROUTE_CHUNK] = sel
    x3 = x2_ref[...] + g2_ref[0] * _dot(sel_ref[...], _unpack_pairs(buf_ref[cur]))
    y_ref[...] = x3 * lax.rsqrt(jnp.mean(x3 * x3, axis=-1, keepdims=True) + EPS) * fw_ref[...]


def _combine(out_rows, slot_t, wts_t, x2_flat, g2, final_w, n_flat, off_flat, dst_flat, tiles_per_batch):
    t, d = x2_flat.shape
    tm = ROUTE_TILE
    nt = t // tm
    return pl.pallas_call(
        _combine_kernel,
        grid_spec=pltpu.PrefetchScalarGridSpec(
            num_scalar_prefetch=3,
            grid=(nt,),
            in_specs=[pl.BlockSpec(memory_space=pl.ANY),
                      pl.BlockSpec((tm, 8), lambda i, *_: (i, 0)),
                      pl.BlockSpec((tm, 8), lambda i, *_: (i, 0)),
                      pl.BlockSpec((tm, d), lambda i, *_: (i, 0)),
                      pl.BlockSpec((1, 1, d), lambda i, *_: (i // tiles_per_batch, 0, 0)),
                      pl.BlockSpec((1, d), lambda i, *_: (0, 0))],
            out_specs=pl.BlockSpec((tm, d), lambda i, *_: (i, 0)),
            scratch_shapes=[pltpu.VMEM((2, SEG_PAD, PAIR_WIDTH), U32), pltpu.VMEM((tm, SEG_PAD), BF16),
                            pltpu.SemaphoreType.DMA((2,))]),
        out_shape=jax.ShapeDtypeStruct((t, d), F32),
        compiler_params=_params(("arbitrary",)),
        name="combine",
    )(n_flat, off_flat, dst_flat, out_rows, slot_t, wts_t, x2_flat, g2, final_w.reshape(1, d))


def _expert_kernel(blk_e, nb_used, xs_ref, w1_ref, b1_ref, w2_ref, b2_ref, o_ref, w1b_ref, w2b_ref):
    i = pl.program_id(0)
    used = i < nb_used[0]

    @pl.when(jnp.logical_not(used))
    def _():
        o_ref[...] = jnp.zeros(o_ref.shape, U32)

    @pl.when(used & ((i == 0) | (blk_e[i] != blk_e[jnp.maximum(i - 1, 0)])))
    def _():
        w1b_ref[...] = w1_ref[0].astype(BF16)
        w2b_ref[...] = w2_ref[0].astype(BF16)

    @pl.when(used)
    def _():
        gu = _dot(_unpack_pairs(xs_ref[...]), w1b_ref[...]) + b1_ref[0]
        gate = jnp.minimum(gu[:, :D_FF], SWIGLU_LIMIT)
        up = jnp.clip(gu[:, D_FF:], -SWIGLU_LIMIT, SWIGLU_LIMIT)
        act = (up + 1.0) * gate * jax.nn.sigmoid(SWIGLU_ALPHA * gate)
        o = _dot(act.astype(BF16), w2b_ref[...]) + b2_ref[0]
        o_ref[...] = _pack_pairs(o.astype(BF16).astype(F32))


def _expert(xs, blk_e, nb_used, w1, b1, w2, b2):
    rows = xs.shape[0]
    nb = rows // EXPERT_BLOCK
    row_blk = lambda i, be, nu: (jnp.maximum(jnp.minimum(i, nu[0] - 1), 0), 0)
    per_e = lambda i, be, nu: (be[i], 0, 0)
    return pl.pallas_call(
        _expert_kernel,
        grid_spec=pltpu.PrefetchScalarGridSpec(
            num_scalar_prefetch=2,
            grid=(nb,),
            in_specs=[pl.BlockSpec((EXPERT_BLOCK, PAIR_WIDTH), row_blk),
                      pl.BlockSpec((1, D_MODEL, 2 * D_FF), per_e),
                      pl.BlockSpec((1, 1, 2 * D_FF), per_e),
                      pl.BlockSpec((1, D_FF, D_MODEL), per_e),
                      pl.BlockSpec((1, 1, D_MODEL), per_e)],
            out_specs=pl.BlockSpec((EXPERT_BLOCK, PAIR_WIDTH), lambda i, be, nu: (i, 0)),
            scratch_shapes=[pltpu.VMEM((D_MODEL, 2 * D_FF), BF16), pltpu.VMEM((D_FF, D_MODEL), BF16)]),
        out_shape=jax.ShapeDtypeStruct((rows, PAIR_WIDTH), U32),
        compiler_params=_params(("arbitrary",)),
        name="expert",
    )(blk_e, nb_used, xs, w1, b1.reshape(N_EXPERTS, 1, 2 * D_FF), w2, b2.reshape(N_EXPERTS, 1, D_MODEL))


def _routing_tables(cnt, nb):
    cnt = cnt.astype(I32)
    n_al = (cnt + SEG_ALIGN - 1) // SEG_ALIGN * SEG_ALIGN
    seg_off = jnp.cumsum(n_al, axis=1) - n_al
    rel = jnp.cumsum(n_al, axis=0) - n_al
    tot = jnp.sum(n_al, axis=0)
    blocks_e = (tot + EXPERT_BLOCK - 1) // EXPERT_BLOCK
    blk_end = jnp.cumsum(blocks_e)
    e_start = (blk_end - blocks_e) * EXPERT_BLOCK
    dst = e_start[None, :] + rel
    blk_e = jnp.minimum(jnp.sum(blk_end[None, :] <= jnp.arange(nb, dtype=I32)[:, None], axis=1),
                        N_EXPERTS - 1).astype(I32)
    nb_used = blk_end[-1:].astype(I32)
    tails = jnp.concatenate([e_start + tot, blocks_e * EXPERT_BLOCK - tot, nb_used,
                             jnp.full((1,), nb, I32)]).astype(I32)
    counts = jnp.concatenate([n_al.reshape(-1), jnp.sum(n_al, axis=1)])
    return (counts, seg_off.reshape(-1).astype(I32), dst.reshape(-1).astype(I32), blk_e, nb_used, tails)


def kernel(x, c, ctx, c_ctx, w_ada, b_ada, norm1_w, w_in, gate_b, conv_w, conv_b, w_pool, pool_scale,
           hnorm_w, w_bp, w_bm, w_out, norm2_w, w_router, b_router, w1, b1, w2, b2, final_norm_w):
    assert w_ada.shape[0] == 1, "single-layer kernel"
    b, l, d = x.shape
    lc = ctx.shape[1]
    assert d == D_MODEL and l % ROUTE_TILE == 0 and l % GRID_W == 0 and lc % CHUNK == 0
    t = b * l

    rows = (b + 1 + 7) // 8 * 8
    cc = jnp.zeros((rows, d), F32).at[:b].set(c).at[b].set(c_ctx)
    mod = _ada(cc, w_ada[0], b_ada[0])
    sh1, s1, g1, sh2, s2, g2 = [mod[:b, i * d:(i + 1) * d].reshape(b, 1, d) for i in range(6)]
    csh1, cs1 = [jnp.broadcast_to(mod[b, i * d:(i + 1) * d].reshape(1, 1, d), (b, 1, d)) for i in range(2)]

    w_in0 = w_in[0]
    off_gate = POOL_WIDTH + 4 * M_WIDTH
    wpin = w_in0[:, :POOL_WIDTH].astype(BF16)
    wqkvo = w_in0[:, POOL_WIDTH:off_gate].astype(BF16)
    order = [4 * (2 * dirn + is_f) + HEADS_PER_STEP * pair + j
             for pair in range(M_HEADS // HEADS_PER_STEP) for is_f in range(2) for dirn in range(2)
             for j in range(HEADS_PER_STEP)]
    wg_hi, wg_lo = _split_bf16(w_in0[:, off_gate:off_gate + N_GATE][:, order])
    wgate = jnp.pad(jnp.concatenate([wg_hi, wg_lo], axis=1), ((0, 0), (0, LANES - 2 * N_GATE)))
    wmerge = w_in0[:, off_gate + N_GATE:].astype(BF16)
    gbias = jnp.pad(gate_b[0][jnp.asarray(order)], (0, LANES - N_GATE)).reshape(1, LANES)
    nw1 = norm1_w[0].reshape(1, d)
    proj_consts = (nw1, gbias, wqkvo, wgate)

    qk_raw, vt, og, gate_rows = _inproj(x, sh1, s1, *proj_consts)
    qk_raw_c, vt_c, _, gate_rows_c = _inproj(ctx, csh1, cs1, *proj_consts)
    qt, k = _conv(qk_raw, conv_w[0], conv_b[0])
    _, k_c = _conv(qk_raw_c, conv_w[0], conv_b[0])
    hn = _mlstm(qt, k, vt, gate_rows, k_c, vt_c, gate_rows_c)
    x2, xn2, slot, wts, cnt = _merge_route(
        x, hn, og, sh1, s1, g1, sh2, s2, nw1, wpin, wmerge, w_pool[0].astype(BF16),
        pool_scale[0].reshape(1, POOL_WIDTH), w_bp[0].astype(BF16), hnorm_w[0], w_bm[0].astype(BF16),
        w_out[0].astype(BF16), norm2_w[0].reshape(1, d), w_router[0], b_router[0])

    nt = t // ROUTE_TILE
    nb = (t * TOP_K + nt * N_EXPERTS * (SEG_ALIGN - 1)) // EXPERT_BLOCK + N_EXPERTS
    n_flat, off_flat, dst_flat, blk_e, nb_used, tails = _routing_tables(cnt[:, :, 0], nb)
    xs = _dispatch(xn2.reshape(t, d), slot, n_flat, off_flat, dst_flat, tails, nb * EXPERT_BLOCK)
    out_rows = _expert(xs, blk_e, nb_used, w1[0], b1[0], w2[0], b2[0])
    slot_t = slot.transpose(0, 2, 1).reshape(t, 8)
    wts_t = wts.transpose(0, 2, 1).reshape(t, 8)
    y = _combine(out_rows, slot_t, wts_t, x2.reshape(t, d), g2, final_norm_w, n_flat, off_flat, dst_flat,
                 l // ROUTE_TILE)
    return y.reshape(b, l, d)
```

```python
import functools

import numpy as np
import jax
import jax.numpy as jnp
from jax import lax
from jax.experimental import pallas as pl
from jax.experimental.pallas import tpu as pltpu

F32 = jnp.float32
BF16 = jnp.bfloat16
I32 = jnp.int32
U32 = jnp.uint32
HIGHEST = lax.Precision.HIGHEST

D_MODEL = 1024
EPS = 1e-6
GRID_W = 64
POOL_WINDOWS = (2, 4, 8, 16)
POOL_WIDTH = 512
POOL_GC = 128
M_HEADS = 4
M_HEAD_DIM = 128
M_WIDTH = 512
CHUNK = 128
N_GATE = 16
N_EXPERTS = 32
TOP_K = 4
D_FF = 1024
SWIGLU_LIMIT = 7.0
SWIGLU_ALPHA = 1.702

LANES = 128
PROJ_TILE = 512
POOL_SLAB = 256
ROUTE_TILE = 512
BF16_ROWS = 16
SEG_ALIGN = 8
SEG_PAD = TOP_K * ROUTE_TILE + N_EXPERTS * SEG_ALIGN
PAIR_WIDTH = D_MODEL // 2
EXPERT_BLOCK = 1024
VMEM_LIMIT = 56 * 1024 * 1024


def _dot(a, b, precision=None):
    return jnp.dot(a, b, preferred_element_type=F32, precision=precision)


def _params(semantics):
    return pltpu.CompilerParams(dimension_semantics=semantics, vmem_limit_bytes=VMEM_LIMIT)


def _full(shape):
    nd = len(shape)
    return pl.BlockSpec(shape, lambda *_: (0,) * nd, pipeline_mode=pl.Buffered(1))


def _pack_pairs(v):
    half = v.shape[1] // 2
    lo = lax.bitcast_convert_type(v[:, :half], U32) >> 16
    hi = lax.bitcast_convert_type(v[:, half:], U32) & jnp.uint32(0xFFFF0000)
    return lo | hi


def _unpack_pairs(u):
    lo = lax.bitcast_convert_type(u << 16, F32)
    hi = lax.bitcast_convert_type(u & jnp.uint32(0xFFFF0000), F32)
    return jnp.concatenate([lo, hi], axis=1).astype(BF16)


def _split_bf16(a):
    hi = a.astype(BF16)
    return hi, (a - hi.astype(F32)).astype(BF16)


def _ada_kernel(c_ref, w_ref, b_ref, o_ref):
    c = c_ref[...]
    s = c * jax.nn.sigmoid(c)
    o_ref[...] = _dot(s, w_ref[...], HIGHEST) + b_ref[...]


def _ada(cc, w_ada, b_ada):
    rows, d = cc.shape
    n = w_ada.shape[1]
    tn = 1024
    return pl.pallas_call(
        _ada_kernel,
        grid=(n // tn,),
        in_specs=[pl.BlockSpec((rows, d), lambda j: (0, 0)),
                  pl.BlockSpec((d, tn), lambda j: (0, j)),
                  pl.BlockSpec((1, tn), lambda j: (0, j))],
        out_specs=pl.BlockSpec((rows, tn), lambda j: (0, j)),
        out_shape=jax.ShapeDtypeStruct((rows, n), F32),
        compiler_params=_params(("arbitrary",)),
        name="ada",
    )(cc, w_ada, b_ada.reshape(1, n))


def _pool_constants(tm):
    pos = np.arange(tm) % GRID_W
    row = np.arange(tm) // GRID_W
    slab = min(tm, POOL_SLAB)
    pm = np.zeros((len(POOL_WINDOWS), slab, slab), np.float32)
    inv = np.zeros((tm, POOL_WIDTH), np.float32)
    for g, win in enumerate(POOL_WINDOWS):
        lo = np.clip(pos - win // 2, 0, GRID_W)
        hi = np.clip(pos + win // 2, 0, GRID_W)
        same = row[:, None] == row[None, :]
        full = (same & (pos[None, :] >= lo[:, None]) & (pos[None, :] < hi[:, None])).astype(np.float32)
        pm[g] = full[:slab, :slab]
        inv[:, g * POOL_GC:(g + 1) * POOL_GC] = (1.0 / (hi - lo).astype(np.float32))[:, None]
    t = np.arange(tm)
    same_chunk = (t[:, None] // CHUNK) == (t[None, :] // CHUNK)
    tl = (same_chunk & (t[None, :] <= t[:, None])).astype(np.float32)
    tu = (same_chunk & (t[None, :] >= t[:, None])).astype(np.float32)
    return pm, inv, tl, tu


def _modulated_norm(x, nw, scale, shift):
    xn = x * lax.rsqrt(jnp.mean(x * x, axis=-1, keepdims=True) + EPS) * nw
    return xn * (1.0 + scale) + shift


F32_ROWS = 8


def _inproj_kernel(x_ref, xprev_ref, xnext_ref, sh_ref, sc_ref, nw_ref, gbias_ref, wqkvo_ref, wgate_ref,
                   tl_ref, tu_ref, cw_ref, cb_ref, qt_ref, k_ref, vt_ref, og_ref, row_ref):
    i = pl.program_id(1)
    last = pl.num_programs(1) - 1
    xm = _modulated_norm(x_ref[0], nw_ref[...], sc_ref[0], sh_ref[0])
    xb = xm.astype(BF16)
    tm = xm.shape[0]

    xlo = (xm - xb.astype(F32)).astype(BF16)
    wg = wgate_ref[...]
    r_hi = _dot(xb, wg)
    g = r_hi + pltpu.roll(r_hi, LANES - N_GATE, axis=1) + _dot(xlo, wg) + gbias_ref[...]
    lf = jnp.minimum(g, 0.0) - jnp.log1p(jnp.exp(-jnp.abs(g)))
    lane = lax.broadcasted_iota(I32, g.shape, 1)
    lf_hi = lf.astype(BF16).astype(F32)
    lf2 = jnp.where(lane < N_GATE, lf_hi, pltpu.roll(lf - lf_hi, N_GATE, axis=1)).astype(BF16)
    r_pre = _dot(tl_ref[...], lf2)
    r_suf = _dot(tu_ref[...], lf2)
    b_pre = r_pre + pltpu.roll(r_pre, LANES - N_GATE, axis=1)
    b_suf = r_suf + pltpu.roll(r_suf, LANES - N_GATE, axis=1)
    kind = (lane & 7) >> 1
    col = jnp.where(kind == 2, b_pre, jnp.where(kind == 3, b_suf, g))
    row_ref[0] = col.T[:N_GATE]

    wqk = wqkvo_ref[:, 0:2 * M_WIDTH]
    qk = _dot(xb, wqk)
    halo = jnp.concatenate([xprev_ref[0], xnext_ref[0]], axis=0)
    qk_halo = _dot(_modulated_norm(halo, nw_ref[...], sc_ref[0], sh_ref[0]).astype(BF16), wqk)
    prev_row = jnp.where(i > 0, qk_halo[F32_ROWS - 1:F32_ROWS], 0.0)
    next_row = jnp.where(i < last, qk_halo[F32_ROWS:F32_ROWS + 1], 0.0)
    rio = lax.broadcasted_iota(I32, qk.shape, 0)
    qk_prev = jnp.where(rio == 0, prev_row, pltpu.roll(qk, 1, axis=0))
    qk_next = jnp.where(rio == tm - 1, next_row, pltpu.roll(qk, tm - 1, axis=0))
    cw = cw_ref[...]
    acc = cb_ref[...] + qk_prev * cw[0:1] + qk * cw[1:2] + qk_next * cw[2:3]
    y = acc * jax.nn.sigmoid(acc)
    q = y[:, :M_WIDTH] * (M_HEAD_DIM ** -0.5)
    for ci in range(tm // CHUNK):
        qt_ref[0, ci] = q[ci * CHUNK:(ci + 1) * CHUNK].T.astype(BF16)
    k_ref[0] = y[:, M_WIDTH:].astype(BF16)

    v = _dot(xb, wqkvo_ref[:, 2 * M_WIDTH:3 * M_WIDTH])
    for ci in range(v.shape[0] // CHUNK):
        vt_ref[0, ci] = v[ci * CHUNK:(ci + 1) * CHUNK].T.astype(BF16)
    og_ref[0] = jax.nn.sigmoid(_dot(xb, wqkvo_ref[:, 3 * M_WIDTH:4 * M_WIDTH])).astype(BF16)


def _chunk_t_out(b, l, tm):
    return (pl.BlockSpec((1, tm // CHUNK, M_WIDTH, CHUNK), lambda bi, i: (bi, i, 0, 0)),
            jax.ShapeDtypeStruct((b, l // CHUNK, M_WIDTH, CHUNK), BF16))


def _inproj(x, shift, scale, nw, gbias, wqkvo, wgate, conv_w, conv_b):
    b, l, d = x.shape
    tm = min(PROJ_TILE, l)
    per = tm // F32_ROWS
    nblk = l // F32_ROWS
    _, _, tl, tu = _pool_constants(tm)
    tok = lambda width: pl.BlockSpec((1, tm, width), lambda bi, i: (bi, i, 0))
    vec = pl.BlockSpec((1, 1, d), lambda bi, i: (bi, 0, 0))
    halo_before = pl.BlockSpec((1, F32_ROWS, d), lambda bi, i: (bi, jnp.maximum(i * per - 1, 0), 0))
    halo_after = pl.BlockSpec((1, F32_ROWS, d), lambda bi, i: (bi, jnp.minimum((i + 1) * per, nblk - 1), 0))
    consts = [nw, gbias, wqkvo, wgate, jnp.asarray(tl, BF16), jnp.asarray(tu, BF16), conv_w,
              conv_b.reshape(1, 2 * M_WIDTH)]
    tok_out = lambda width: (tok(width), jax.ShapeDtypeStruct((b, l, width), BF16))
    gate_out = (pl.BlockSpec((1, N_GATE, tm), lambda bi, i: (bi, 0, i)),
                jax.ShapeDtypeStruct((b, N_GATE, l), F32))
    outs = [_chunk_t_out(b, l, tm), tok_out(M_WIDTH), _chunk_t_out(b, l, tm), tok_out(M_WIDTH), gate_out]
    return pl.pallas_call(
        _inproj_kernel,
        grid=(b, l // tm),
        in_specs=[tok(d), halo_before, halo_after, vec, vec] + [_full(c.shape) for c in consts],
        out_specs=[spec for spec, _ in outs],
        out_shape=[shape for _, shape in outs],
        compiler_params=_params(("arbitrary", "arbitrary")),
        name="inproj",
    )(x, x, x, shift, scale, *consts)


HEADS_PER_STEP = 2
N_CHAINS = 2 * HEADS_PER_STEP
GATE_ROWS = 2 * N_CHAINS


STATE_ROWS = M_HEAD_DIM + BF16_ROWS


def _lane_pair(rows2):
    return jnp.concatenate([rows2[0:1], rows2[1:2]], axis=1)


def _block_diag(a, b):
    za, zb = jnp.zeros((a.shape[0], b.shape[1]), a.dtype), jnp.zeros((b.shape[0], a.shape[1]), a.dtype)
    return jnp.concatenate([jnp.concatenate([a, za], axis=1), jnp.concatenate([zb, b], axis=1)], axis=0)


def _side_by_side(stacked):
    d = stacked.shape[0] // 2
    return jnp.concatenate([stacked[:d], stacked[d:]], axis=1)


def _state_step(k2, vt2, i_row, b_row, btot, m_old, m_new, state):
    t = k2.shape[0]
    a = jnp.exp(btot - b_row + i_row - m_new)
    cd = jnp.exp(btot + m_old - m_new)
    ones = (lax.broadcasted_iota(I32, (STATE_ROWS - M_HEAD_DIM, 2 * t), 0) == 0).astype(F32)
    vta = (jnp.concatenate([_side_by_side(vt2).astype(F32), ones], axis=0) * a).astype(BF16)
    return cd * state + _dot(vta, _block_diag(k2[:, :M_HEAD_DIM], k2[:, M_HEAD_DIM:]))


def _chunk_out_t(qt2, k2, vt2, i_row, b_row, mask_t, m_old, state):
    t = k2.shape[0]
    g = i_row - b_row
    g_src = jnp.concatenate([jnp.broadcast_to(g[:, :t], (t, t)).T, jnp.broadcast_to(g[:, t:], (t, t)).T],
                            axis=1)
    logw = jnp.where(mask_t, b_row + g_src, -jnp.inf)
    m_inter = b_row + m_old
    m_t = jnp.maximum(jnp.max(logw, axis=0, keepdims=True), m_inter)
    q_bd = _block_diag(qt2[:M_HEAD_DIM], qt2[M_HEAD_DIM:])
    both = _dot(jnp.concatenate([k2, state.astype(BF16)], axis=0), q_bd)
    s = both[:t] * jnp.exp(logw - m_t)
    inter = both[t:]
    decay = jnp.exp(m_inter - m_t)
    s16 = s.astype(BF16)
    num = _dot(_side_by_side(vt2), _block_diag(s16[:, :t], s16[:, t:])) + decay * inter[:M_HEAD_DIM]
    den = jnp.sum(s, axis=0, keepdims=True) + decay * inter[M_HEAD_DIM:M_HEAD_DIM + 1]
    return num * (1.0 / jnp.maximum(jnp.abs(den), jnp.exp(-m_t)))


def _mlstm_kernel(qt_ref, k_ref, vt_ref, row_ref, kc_ref, vtc_ref, rowc_ref, out_ref,
                  h_ref, bt_ref, ma_ref, mold_ref, mnew_ref):
    nc, ncc, t = k_ref.shape[1], kc_ref.shape[1], CHUNK
    src = lax.broadcasted_iota(I32, (t, t), 0)
    tgt = lax.broadcasted_iota(I32, (t, t), 1)
    causal_t = jnp.concatenate([src <= tgt] * HEADS_PER_STEP, axis=1)
    masks_t = (causal_t, jnp.concatenate([src >= tgt] * HEADS_PER_STEP, axis=1))
    grow = pl.ds(pl.multiple_of(pl.program_id(1) * GATE_ROWS, GATE_ROWS), GATE_ROWS)
    head = [slice(j * M_HEAD_DIM, (j + 1) * M_HEAD_DIM) for j in range(HEADS_PER_STEP)]
    dir_rows = [slice(HEADS_PER_STEP * dirn, HEADS_PER_STEP * (dirn + 1)) for dirn in range(2)]

    def gate_pairs(rows, dirn):
        return (_lane_pair(rows[dir_rows[dirn]]),
                _lane_pair(rows[N_CHAINS + HEADS_PER_STEP * dirn:N_CHAINS + HEADS_PER_STEP * (dirn + 1)]))

    def bcast_pair(col2):
        return _lane_pair(jnp.broadcast_to(col2, (HEADS_PER_STEP, t)))

    state = [jnp.zeros((STATE_ROWS, HEADS_PER_STEP * M_HEAD_DIM), F32)] * 2
    m_st = [jnp.zeros((HEADS_PER_STEP, 1), F32)] * 2
    for dirn in range(2):
        for ci in (range(ncc) if dirn == 0 else reversed(range(ncc))):
            rows = rowc_ref[0, ci, grow, :]
            gi2 = rows[dir_rows[dirn]]
            gb2 = rows[N_CHAINS + HEADS_PER_STEP * dirn:N_CHAINS + HEADS_PER_STEP * (dirn + 1)]
            btot = gb2[:, t - 1:t] if dirn == 0 else gb2[:, 0:1]
            m_new = jnp.maximum(btot + m_st[dirn], jnp.max(btot - gb2 + gi2, axis=1, keepdims=True))
            i_row, b_row = gate_pairs(rows, dirn)
            state[dirn] = _state_step(kc_ref[0, ci], vtc_ref[0, ci], i_row, b_row, bcast_pair(btot),
                                      bcast_pair(m_st[dirn]), bcast_pair(m_new), state[dirn])
            m_st[dirn] = m_new

    gates = row_ref[0, :, grow, :]
    gi, gb = gates[:, :N_CHAINS], gates[:, N_CHAINS:]
    is_fwd = lax.broadcasted_iota(I32, gb.shape, 1) < HEADS_PER_STEP
    bt = jnp.where(is_fwd, jnp.broadcast_to(gb[:, :, t - 1:t], gb.shape),
                   jnp.broadcast_to(gb[:, :, 0:1], gb.shape))
    bt_ref[...] = bt
    ma_ref[...] = jnp.broadcast_to(jnp.max(bt - gb + gi, axis=2, keepdims=True), gb.shape)

    def m_scan(s, carry):
        m_f, m_b = carry
        cf, cb = s, nc - 1 - s
        mold_ref[cf, 0:HEADS_PER_STEP, :] = m_f[0:HEADS_PER_STEP]
        m_f = jnp.maximum(bt_ref[cf] + m_f, ma_ref[cf])
        mnew_ref[cf, 0:HEADS_PER_STEP, :] = m_f[0:HEADS_PER_STEP]
        mold_ref[cb, HEADS_PER_STEP:, :] = m_b[HEADS_PER_STEP:]
        m_b = jnp.maximum(bt_ref[cb] + m_b, ma_ref[cb])
        mnew_ref[cb, HEADS_PER_STEP:, :] = m_b[HEADS_PER_STEP:]
        return m_f, m_b

    m0 = jnp.concatenate([jnp.broadcast_to(m, (HEADS_PER_STEP, t)) for m in m_st], axis=0)
    lax.fori_loop(0, nc, m_scan, (m0, m0))

    def advance(ci, dirn, st):
        i_row, b_row = gate_pairs(row_ref[0, ci, grow, :], dirn)
        m_old, m_new, btot = [_lane_pair(tab[ci][dir_rows[dirn]]) for tab in (mold_ref, mnew_ref, bt_ref)]
        qt2, k2, vt2 = qt_ref[0, ci], k_ref[0, ci], vt_ref[0, ci]
        h_ref[dirn, ci] = _chunk_out_t(qt2, k2, vt2, i_row, b_row, masks_t[dirn], m_old, st)
        return _state_step(k2, vt2, i_row, b_row, btot, m_old, m_new, st)

    def body(s, states):
        return advance(s, 0, states[0]), advance(nc - 1 - s, 1, states[1])

    lax.fori_loop(0, nc, body, tuple(state), unroll=2)

    def norm_body(ci, _):
        ht2 = h_ref[0, ci] + h_ref[1, ci]
        for j in range(HEADS_PER_STEP):
            ht = ht2[:, j * t:(j + 1) * t]
            hn = ht * lax.rsqrt(jnp.mean(ht * ht, axis=0, keepdims=True) + EPS)
            out_ref[0, ci, :, head[j]] = hn.T.astype(BF16)
        return 0

    lax.fori_loop(0, nc, norm_body, 0, unroll=4)


def _mlstm(qt, k, vt, gate_rows, kc, vtc, gate_rows_c):
    b, l, _ = k.shape
    lc = kc.shape[1]
    nc, ncc = l // CHUNK, lc // CHUNK
    pairs = M_HEADS // HEADS_PER_STEP
    width = HEADS_PER_STEP * M_HEAD_DIM
    chunked = lambda a, n: a.reshape(b, n, CHUNK, M_WIDTH)
    by_chunk = lambda g, n: g.reshape(b, pairs * GATE_ROWS, n, CHUNK).transpose(0, 2, 1, 3)
    once = pl.Buffered(1)
    seq = lambda n: pl.BlockSpec((1, n, CHUNK, width), lambda bi, h: (bi, 0, 0, h))
    seq_t = lambda n: pl.BlockSpec((1, n, width, CHUNK), lambda bi, h: (bi, 0, h, 0))
    rows = lambda n: pl.BlockSpec((1, n, pairs * GATE_ROWS, CHUNK), lambda bi, h: (bi, 0, 0, 0),
                                  pipeline_mode=once)
    table = pltpu.VMEM((nc, N_CHAINS, CHUNK), F32)
    out = pl.pallas_call(
        _mlstm_kernel,
        grid=(b, pairs),
        in_specs=[seq_t(nc), seq(nc), seq_t(nc), rows(nc), seq(ncc), seq_t(ncc), rows(ncc)],
        out_specs=pl.BlockSpec((1, nc, CHUNK, width), lambda bi, h: (bi, 0, 0, h)),
        out_shape=jax.ShapeDtypeStruct((b, nc, CHUNK, M_WIDTH), BF16),
        scratch_shapes=[pltpu.VMEM((2, nc, M_HEAD_DIM, HEADS_PER_STEP * CHUNK), F32),
                        table, table, table, table],
        compiler_params=_params(("arbitrary", "arbitrary")),
        name="mlstm",
    )(qt, chunked(k, nc), vt, by_chunk(gate_rows, nc), chunked(kc, ncc), vtc, by_chunk(gate_rows_c, ncc))
    return out.reshape(b, l, M_WIDTH)


def _merge_kernel(x_ref, hn_ref, og_ref, sh_ref, sc_ref, g1_ref, sh2_ref, sc2_ref, nw_ref, wpin_ref,
                  wmerge_ref, pmat_ref, invc_ref, wpool_ref, pscale_ref, wbp_ref, hw_ref, wbm_ref, wout_ref,
                  nw2_ref, wr_ref, br_ref, ustrict_ref, lstrict_ref,
                  o_ref, xn_ref, slot_ref, wts_ref, cnt_ref):
    x = x_ref[0]
    xb = _modulated_norm(x, nw_ref[...], sc_ref[0], sh_ref[0]).astype(BF16)

    u = _dot(xb, wpin_ref[...])
    u_hi, u_lo = _split_bf16(u)
    invc = invc_ref[...]
    slab = pmat_ref.shape[1]
    ya = []
    for gi in range(len(POOL_WINDOWS)):
        cols = slice(gi * POOL_GC, (gi + 1) * POOL_GC)
        both = jnp.concatenate([u_hi[:, cols], u_lo[:, cols]], axis=1)
        sums = jnp.concatenate([_dot(pmat_ref[gi], both[s:s + slab]) for s in range(0, u.shape[0], slab)],
                               axis=0)
        pooled = (sums[:, :POOL_GC] + sums[:, POOL_GC:]) * invc[:, cols] - u[:, cols]
        ya.append(_dot(pooled.astype(BF16), wpool_ref[gi]))
    ya = jnp.concatenate(ya, axis=1) * pscale_ref[...]
    y = jax.nn.sigmoid(_dot(xb, wmerge_ref[:, 0:D_MODEL])) * _dot(ya.astype(BF16), wbp_ref[...])

    yb = hn_ref[0].astype(F32) * hw_ref[...] * og_ref[0].astype(F32)
    y = y + jax.nn.sigmoid(_dot(xb, wmerge_ref[:, D_MODEL:2 * D_MODEL])) * _dot(yb.astype(BF16), wbm_ref[...])
    x2 = x + g1_ref[0] * _dot(y.astype(BF16), wout_ref[...])
    o_ref[0] = x2

    _route_tile(x2, sh2_ref, sc2_ref, nw2_ref, wr_ref, br_ref, ustrict_ref, lstrict_ref,
                xn_ref, slot_ref, wts_ref, cnt_ref)


def _merge_route(x, hn, og, shift, scale, g1, shift2, scale2, nw, wpin, wmerge, wpool, pscale, wbp, hnorm_w,
                 wbm, wout, nw2, w_router, b_router):
    b, l, d = x.shape
    tm = ROUTE_TILE
    nl = l // tm
    nt = b * nl
    pm, inv, _, _ = _pool_constants(tm)
    tok = lambda width: pl.BlockSpec((1, tm, width), lambda bi, i: (bi, i, 0))
    vec = pl.BlockSpec((1, 1, d), lambda bi, i: (bi, 0, 0))
    tile = lambda r, c: pl.BlockSpec((1, r, c), lambda bi, i: (bi * nl + i, 0, 0))
    consts = [nw, wpin, wmerge, jnp.asarray(pm, BF16), jnp.asarray(inv), wpool, pscale, wbp,
              hnorm_w.reshape(1, M_WIDTH), wbm, wout, nw2] + _router_consts(w_router, b_router)
    return pl.pallas_call(
        _merge_kernel,
        grid=(b, nl),
        in_specs=[tok(d), tok(M_WIDTH), tok(M_WIDTH), vec, vec, vec, vec, vec]
        + [_full(c.shape) for c in consts],
        out_specs=[tok(d), tok(d), tile(8, tm), tile(8, tm), tile(N_EXPERTS, LANES)],
        out_shape=[jax.ShapeDtypeStruct((b, l, d), F32),
                   jax.ShapeDtypeStruct((b, l, d), BF16),
                   jax.ShapeDtypeStruct((nt, 8, tm), I32),
                   jax.ShapeDtypeStruct((nt, 8, tm), F32),
                   jax.ShapeDtypeStruct((nt, N_EXPERTS, LANES), F32)],
        compiler_params=_params(("arbitrary", "arbitrary")),
        name="merge_route",
    )(x, hn, og, shift, scale, g1, shift2, scale2, *consts)


def _route_tile(x, sh_ref, sc_ref, nw_ref, wr_ref, br_ref, ustrict_ref, lstrict_ref,
                xn_ref, slot_ref, wts_ref, cnt_ref):
    tm = x.shape[0]
    xm = _modulated_norm(x, nw_ref[...], sc_ref[0], sh_ref[0])
    xn_ref[0] = xm.astype(BF16)

    x_hi, x_lo = _split_bf16(xm)
    r_hi = _dot(x_hi, wr_ref[...])
    r = r_hi + pltpu.roll(r_hi, LANES - N_EXPERTS, axis=1) + _dot(x_lo, wr_ref[...])
    logits = r.T[:N_EXPERTS] + br_ref[...]
    eio = lax.broadcasted_iota(I32, logits.shape, 0).astype(F32)
    rest = logits
    onehots, vals = [], []
    for _ in range(TOP_K):
        mx = jnp.max(rest, axis=0, keepdims=True)
        idx = jnp.min(jnp.where(rest == mx, eio, float(N_EXPERTS)), axis=0, keepdims=True)
        oh = eio == idx
        onehots.append(oh)
        vals.append(mx)
        rest = jnp.where(oh, -jnp.inf, rest)
    exps = [jnp.exp(vk - vals[0]) for vk in vals]
    denom = exps[0] + exps[1] + exps[2] + exps[3]

    oh_all = jnp.zeros(logits.shape, F32)
    for oh in onehots:
        oh_all = oh_all + oh.astype(F32)
    cnt = jnp.sum(oh_all, axis=1, keepdims=True)
    n_al = jnp.ceil(cnt * (1.0 / SEG_ALIGN)) * SEG_ALIGN
    seg_off = _dot(lstrict_ref[...], jnp.broadcast_to(n_al, (N_EXPERTS, LANES)).astype(BF16))[:, 0:1]
    rank = _dot(oh_all.astype(BF16), ustrict_ref[...])
    base = seg_off + rank
    for kk in range(TOP_K):
        slot = jnp.sum(jnp.where(onehots[kk], base, 0.0), axis=0, keepdims=True)
        slot_ref[0, kk:kk + 1, :] = slot.astype(I32)
        wts_ref[0, kk:kk + 1, :] = exps[kk] / denom
    slot_ref[0, TOP_K:, :] = jnp.full((8 - TOP_K, tm), -1, I32)
    wts_ref[0, TOP_K:, :] = jnp.zeros((8 - TOP_K, tm), F32)
    cnt_ref[0] = jnp.broadcast_to(cnt, (N_EXPERTS, LANES))


def _router_consts(w_router, b_router):
    s = np.arange(ROUTE_TILE)
    ustrict = jnp.asarray((s[:, None] < s[None, :]).astype(np.float32), BF16)
    e = np.arange(N_EXPERTS)
    lstrict = jnp.asarray((e[None, :] < e[:, None]).astype(np.float32), BF16)
    w_cat = jnp.pad(jnp.concatenate(_split_bf16(w_router), axis=1), ((0, 0), (0, LANES - 2 * N_EXPERTS)))
    return [w_cat, b_router.reshape(N_EXPERTS, 1), ustrict, lstrict]


def _segment_starts(tile, n_s, off_s, dst_s, make_copy):
    def body(e, _):
        idx = tile * N_EXPERTS + e
        n = n_s[idx]

        @pl.when(n > 0)
        def _():
            make_copy(pl.multiple_of(off_s[idx], SEG_ALIGN), pl.multiple_of(dst_s[idx], SEG_ALIGN),
                      pl.multiple_of(n, SEG_ALIGN)).start()

        return 0

    lax.fori_loop(0, N_EXPERTS, body, 0)


def _segment_wait(tile, n_s, make_copy):
    total = n_s[pl.num_programs(0) * N_EXPERTS + tile]
    make_copy(0, 0, pl.multiple_of(total, SEG_ALIGN)).wait()


def _zero_fill(tail_s, xs_hbm, zero_ref, sem, action):
    def make_copy(dst, sz):
        return pltpu.make_async_copy(zero_ref.at[pl.ds(0, sz)], xs_hbm.at[pl.ds(dst, sz)], sem)

    def tail_body(e, _):
        n = tail_s[N_EXPERTS + e]

        @pl.when(n > 0)
        def _():
            action(make_copy(pl.multiple_of(tail_s[e], SEG_ALIGN), pl.multiple_of(n, SEG_ALIGN)))

        return 0

    lax.fori_loop(0, N_EXPERTS, tail_body, 0)

    def block_body(blk, _):
        action(make_copy(pl.multiple_of(blk * EXPERT_BLOCK, EXPERT_BLOCK), EXPERT_BLOCK))
        return 0

    lax.fori_loop(tail_s[2 * N_EXPERTS], tail_s[2 * N_EXPERTS + 1], block_body, 0)


ROUTE_CHUNK = 768
assert SEG_PAD % ROUTE_CHUNK == 0


def _dispatch_kernel(n_s, off_s, dst_s, tail_s, xn_ref, slot_ref, xs_hbm, g_ref, zero_ref, sems):
    tile = pl.program_id(0)
    last = pl.num_programs(0) - 1
    cur = tile % 2

    def copier(of_tile):
        par = of_tile % 2
        return lambda src, dst, n: pltpu.make_async_copy(
            g_ref.at[par, pl.ds(src, n)], xs_hbm.at[pl.ds(dst, n)], sems.at[par])

    @pl.when(tile >= 2)
    def _():
        _segment_wait(tile - 2, n_s, copier(tile - 2))

    x = xn_ref[...]
    tm = x.shape[0]
    slot = slot_ref[0].astype(jnp.int16)
    rio = lax.broadcasted_iota(I32, (ROUTE_CHUNK, tm), 0).astype(jnp.int16)
    one = jnp.ones((), BF16)
    for ci in range(SEG_PAD // ROUTE_CHUNK):
        rel = slot - ci * ROUTE_CHUNK
        sel = jnp.zeros((ROUTE_CHUNK, tm), BF16)
        for kk in range(TOP_K):
            sel = jnp.where(rio == rel[kk:kk + 1, :], one, sel)
        g_ref[cur, ci * ROUTE_CHUNK:(ci + 1) * ROUTE_CHUNK, :] = _pack_pairs(_dot(sel, x))

    _segment_starts(tile, n_s, off_s, dst_s, copier(tile))

    @pl.when(tile == last)
    def _():
        @pl.when(tile >= 1)
        def _():
            _segment_wait(tile - 1, n_s, copier(tile - 1))

        _segment_wait(tile, n_s, copier(tile))
        zero_ref[...] = jnp.zeros(zero_ref.shape, U32)
        _zero_fill(tail_s, xs_hbm, zero_ref, sems.at[2], lambda cp: cp.start())
        _zero_fill(tail_s, xs_hbm, zero_ref, sems.at[2], lambda cp: cp.wait())


def _dispatch(xn2, slot, n_flat, off_flat, dst_flat, tail_flat, rows_total):
    t, d = xn2.shape
    tm = ROUTE_TILE
    nt = t // tm
    return pl.pallas_call(
        _dispatch_kernel,
        grid_spec=pltpu.PrefetchScalarGridSpec(
            num_scalar_prefetch=4,
            grid=(nt,),
            in_specs=[pl.BlockSpec((tm, d), lambda i, *_: (i, 0)),
                      pl.BlockSpec((1, 8, tm), lambda i, *_: (i, 0, 0))],
            out_specs=pl.BlockSpec(memory_space=pl.ANY),
            scratch_shapes=[pltpu.VMEM((2, SEG_PAD, PAIR_WIDTH), U32),
                            pltpu.VMEM((EXPERT_BLOCK, PAIR_WIDTH), U32),
                            pltpu.SemaphoreType.DMA((3,))]),
        out_shape=jax.ShapeDtypeStruct((rows_total, PAIR_WIDTH), U32),
        compiler_params=_params(("arbitrary",)),
        name="dispatch",
    )(n_flat, off_flat, dst_flat, tail_flat, xn2, slot)


def _combine_kernel(n_s, off_s, dst_s, out_hbm, slott_ref, wtst_ref, x2_ref, g2_ref, fw_ref, y_ref,
                    buf_ref, sel_ref, sems):
    tile = pl.program_id(0)
    cur = tile % 2

    def copier(of_tile):
        par = of_tile % 2
        return lambda seg, src, n: pltpu.make_async_copy(
            out_hbm.at[pl.ds(src, n)], buf_ref.at[par, pl.ds(seg, n)], sems.at[par])

    @pl.when(tile == 0)
    def _():
        buf_ref[...] = jnp.zeros(buf_ref.shape, U32)
        _segment_starts(tile, n_s, off_s, dst_s, copier(tile))

    @pl.when(tile + 1 < pl.num_programs(0))
    def _():
        _segment_starts(tile + 1, n_s, off_s, dst_s, copier(tile + 1))

    _segment_wait(tile, n_s, copier(tile))

    st = slott_ref[...].astype(jnp.int16)
    wt = wtst_ref[...].astype(BF16)
    tm = st.shape[0]
    lio = lax.broadcasted_iota(I32, (tm, ROUTE_CHUNK), 1).astype(jnp.int16)
    for ci in range(SEG_PAD // ROUTE_CHUNK):
        rel = st - ci * ROUTE_CHUNK
        sel = jnp.zeros((tm, ROUTE_CHUNK), BF16)
        for kk in range(TOP_K):
            sel = jnp.where(lio == rel[:, kk:kk + 1], wt[:, kk:kk + 1], sel)
        sel_ref[:, ci * ROUTE_CHUNK:(ci + 1) * ROUTE_CHUNK] = sel
    x3 = x2_ref[...] + g2_ref[0] * _dot(sel_ref[...], _unpack_pairs(buf_ref[cur]))
    y_ref[...] = x3 * lax.rsqrt(jnp.mean(x3 * x3, axis=-1, keepdims=True) + EPS) * fw_ref[...]


def _combine(out_rows, slot_t, wts_t, x2_flat, g2, final_w, n_flat, off_flat, dst_flat, tiles_per_batch):
    t, d = x2_flat.shape
    tm = ROUTE_TILE
    nt = t // tm
    return pl.pallas_call(
        _combine_kernel,
        grid_spec=pltpu.PrefetchScalarGridSpec(
            num_scalar_prefetch=3,
            grid=(nt,),
            in_specs=[pl.BlockSpec(memory_space=pl.ANY),
                      pl.BlockSpec((tm, 8), lambda i, *_: (i, 0)),
                      pl.BlockSpec((tm, 8), lambda i, *_: (i, 0)),
                      pl.BlockSpec((tm, d), lambda i, *_: (i, 0)),
                      pl.BlockSpec((1, 1, d), lambda i, *_: (i // tiles_per_batch, 0, 0)),
                      pl.BlockSpec((1, d), lambda i, *_: (0, 0))],
            out_specs=pl.BlockSpec((tm, d), lambda i, *_: (i, 0)),
            scratch_shapes=[pltpu.VMEM((2, SEG_PAD, PAIR_WIDTH), U32), pltpu.VMEM((tm, SEG_PAD), BF16),
                            pltpu.SemaphoreType.DMA((2,))]),
        out_shape=jax.ShapeDtypeStruct((t, d), F32),
        compiler_params=_params(("arbitrary",)),
        name="combine",
    )(n_flat, off_flat, dst_flat, out_rows, slot_t, wts_t, x2_flat, g2, final_w.reshape(1, d))


def _expert_kernel(blk_e, nb_used, xs_ref, w1_ref, b1_ref, w2_ref, b2_ref, o_ref, w1b_ref, w2b_ref):
    i = pl.program_id(0)
    used = i < nb_used[0]

    @pl.when(jnp.logical_not(used))
    def _():
        o_ref[...] = jnp.zeros(o_ref.shape, U32)

    @pl.when(used & ((i == 0) | (blk_e[i] != blk_e[jnp.maximum(i - 1, 0)])))
    def _():
        w1b_ref[...] = w1_ref[0].astype(BF16)
        w2b_ref[...] = w2_ref[0].astype(BF16)

    @pl.when(used)
    def _():
        gu = _dot(_unpack_pairs(xs_ref[...]), w1b_ref[...]) + b1_ref[0]
        gate = jnp.minimum(gu[:, :D_FF], SWIGLU_LIMIT)
        up = jnp.clip(gu[:, D_FF:], -SWIGLU_LIMIT, SWIGLU_LIMIT)
        act = (up + 1.0) * gate * jax.nn.sigmoid(SWIGLU_ALPHA * gate)
        o = _dot(act.astype(BF16), w2b_ref[...]) + b2_ref[0]
        o_ref[...] = _pack_pairs(o.astype(BF16).astype(F32))


def _expert(xs, blk_e, nb_used, w1, b1, w2, b2):
    rows = xs.shape[0]
    nb = rows // EXPERT_BLOCK
    row_blk = lambda i, be, nu: (jnp.maximum(jnp.minimum(i, nu[0] - 1), 0), 0)
    per_e = lambda i, be, nu: (be[i], 0, 0)
    return pl.pallas_call(
        _expert_kernel,
        grid_spec=pltpu.PrefetchScalarGridSpec(
            num_scalar_prefetch=2,
            grid=(nb,),
            in_specs=[pl.BlockSpec((EXPERT_BLOCK, PAIR_WIDTH), row_blk),
                      pl.BlockSpec((1, D_MODEL, 2 * D_FF), per_e),
                      pl.BlockSpec((1, 1, 2 * D_FF), per_e),
                      pl.BlockSpec((1, D_FF, D_MODEL), per_e),
                      pl.BlockSpec((1, 1, D_MODEL), per_e)],
            out_specs=pl.BlockSpec((EXPERT_BLOCK, PAIR_WIDTH), lambda i, be, nu: (i, 0)),
            scratch_shapes=[pltpu.VMEM((D_MODEL, 2 * D_FF), BF16), pltpu.VMEM((D_FF, D_MODEL), BF16)]),
        out_shape=jax.ShapeDtypeStruct((rows, PAIR_WIDTH), U32),
        compiler_params=_params(("arbitrary",)),
        name="expert",
    )(blk_e, nb_used, xs, w1, b1.reshape(N_EXPERTS, 1, 2 * D_FF), w2, b2.reshape(N_EXPERTS, 1, D_MODEL))


def _routing_tables(cnt, nb):
    cnt = cnt.astype(I32)
    n_al = (cnt + SEG_ALIGN - 1) // SEG_ALIGN * SEG_ALIGN
    seg_off = jnp.cumsum(n_al, axis=1) - n_al
    rel = jnp.cumsum(n_al, axis=0) - n_al
    tot = jnp.sum(n_al, axis=0)
    blocks_e = (tot + EXPERT_BLOCK - 1) // EXPERT_BLOCK
    blk_end = jnp.cumsum(blocks_e)
    e_start = (blk_end - blocks_e) * EXPERT_BLOCK
    dst = e_start[None, :] + rel
    blk_e = jnp.minimum(jnp.sum(blk_end[None, :] <= jnp.arange(nb, dtype=I32)[:, None], axis=1),
                        N_EXPERTS - 1).astype(I32)
    nb_used = blk_end[-1:].astype(I32)
    tails = jnp.concatenate([e_start + tot, blocks_e * EXPERT_BLOCK - tot, nb_used,
                             jnp.full((1,), nb, I32)]).astype(I32)
    counts = jnp.concatenate([n_al.reshape(-1), jnp.sum(n_al, axis=1)])
    return (counts, seg_off.reshape(-1).astype(I32), dst.reshape(-1).astype(I32), blk_e, nb_used, tails)


def kernel(x, c, ctx, c_ctx, w_ada, b_ada, norm1_w, w_in, gate_b, conv_w, conv_b, w_pool, pool_scale,
           hnorm_w, w_bp, w_bm, w_out, norm2_w, w_router, b_router, w1, b1, w2, b2, final_norm_w):
    assert w_ada.shape[0] == 1, "single-layer kernel"
    b, l, d = x.shape
    lc = ctx.shape[1]
    assert d == D_MODEL and l % ROUTE_TILE == 0 and l % GRID_W == 0 and lc % CHUNK == 0
    t = b * l

    rows = (b + 1 + 7) // 8 * 8
    cc = jnp.zeros((rows, d), F32).at[:b].set(c).at[b].set(c_ctx)
    mod = _ada(cc, w_ada[0], b_ada[0])
    sh1, s1, g1, sh2, s2, g2 = [mod[:b, i * d:(i + 1) * d].reshape(b, 1, d) for i in range(6)]
    csh1, cs1 = [jnp.broadcast_to(mod[b, i * d:(i + 1) * d].reshape(1, 1, d), (b, 1, d)) for i in range(2)]

    w_in0 = w_in[0]
    off_gate = POOL_WIDTH + 4 * M_WIDTH
    wpin = w_in0[:, :POOL_WIDTH].astype(BF16)
    wqkvo = w_in0[:, POOL_WIDTH:off_gate].astype(BF16)
    order = [4 * (2 * dirn + is_f) + HEADS_PER_STEP * pair + j
             for pair in range(M_HEADS // HEADS_PER_STEP) for is_f in range(2) for dirn in range(2)
             for j in range(HEADS_PER_STEP)]
    wg_hi, wg_lo = _split_bf16(w_in0[:, off_gate:off_gate + N_GATE][:, order])
    wgate = jnp.pad(jnp.concatenate([wg_hi, wg_lo], axis=1), ((0, 0), (0, LANES - 2 * N_GATE)))
    wmerge = w_in0[:, off_gate + N_GATE:].astype(BF16)
    gbias = jnp.pad(gate_b[0][jnp.asarray(order)], (0, LANES - N_GATE)).reshape(1, LANES)
    nw1 = norm1_w[0].reshape(1, d)
    proj_consts = (nw1, gbias, wqkvo, wgate, conv_w[0], conv_b[0])

    qt, k, vt, og, gate_rows = _inproj(x, sh1, s1, *proj_consts)
    _, k_c, vt_c, _, gate_rows_c = _inproj(ctx, csh1, cs1, *proj_consts)
    hn = _mlstm(qt, k, vt, gate_rows, k_c, vt_c, gate_rows_c)
    x2, xn2, slot, wts, cnt = _merge_route(
        x, hn, og, sh1, s1, g1, sh2, s2, nw1, wpin, wmerge, w_pool[0].astype(BF16),
        pool_scale[0].reshape(1, POOL_WIDTH), w_bp[0].astype(BF16), hnorm_w[0], w_bm[0].astype(BF16),
        w_out[0].astype(BF16), norm2_w[0].reshape(1, d), w_router[0], b_router[0])

    nt = t // ROUTE_TILE
    nb = (t * TOP_K + nt * N_EXPERTS * (SEG_ALIGN - 1)) // EXPERT_BLOCK + N_EXPERTS
    n_flat, off_flat, dst_flat, blk_e, nb_used, tails = _routing_tables(cnt[:, :, 0], nb)
    xs = _dispatch(xn2.reshape(t, d), slot, n_flat, off_flat, dst_flat, tails, nb * EXPERT_BLOCK)
    out_rows = _expert(xs, blk_e, nb_used, w1[0], b1[0], w2[0], b2[0])
    slot_t = slot.transpose(0, 2, 1).reshape(t, 8)
    wts_t = wts.transpose(0, 2, 1).reshape(t, 8)
    y = _combine(out_rows, slot_t, wts_t, x2.reshape(t, d), g2, final_norm_w, n_flat, off_flat, dst_flat,
                 l // ROUTE_TILE)
    return y.reshape(b, l, d)
```

```python
import numpy as np
import jax
import jax.numpy as jnp
from jax import lax
from jax.experimental import pallas as pl
from jax.experimental.pallas import tpu as pltpu

F32 = jnp.float32
BF16 = jnp.bfloat16
I32 = jnp.int32
U32 = jnp.uint32
HIGHEST = lax.Precision.HIGHEST

D_MODEL = 1024
EPS = 1e-6
GRID_W = 64
POOL_WINDOWS = (2, 4, 8, 16)
POOL_WIDTH = 512
POOL_GC = 128
M_HEADS = 4
M_HEAD_DIM = 128
M_WIDTH = 512
CHUNK = 128
N_GATE = 16
N_EXPERTS = 32
TOP_K = 4
D_FF = 1024
SWIGLU_LIMIT = 7.0
SWIGLU_ALPHA = 1.702

LANES = 128
PROJ_TILE = 512
POOL_SLAB = 256
ROUTE_TILE = 512
BF16_ROWS = 16
SEG_ALIGN = 8
SEG_PAD = TOP_K * ROUTE_TILE + N_EXPERTS * SEG_ALIGN
PAIR_WIDTH = D_MODEL // 2
EXPERT_BLOCK = 1024
VMEM_LIMIT = 56 * 1024 * 1024


def _dot(a, b, precision=None):
    return jnp.dot(a, b, preferred_element_type=F32, precision=precision)


def _params(semantics):
    return pltpu.CompilerParams(dimension_semantics=semantics, vmem_limit_bytes=VMEM_LIMIT)


def _full(shape):
    nd = len(shape)
    return pl.BlockSpec(shape, lambda *_: (0,) * nd, pipeline_mode=pl.Buffered(1))


def _pack_pairs(v):
    half = v.shape[1] // 2
    lo = lax.bitcast_convert_type(v[:, :half], U32) >> 16
    hi = lax.bitcast_convert_type(v[:, half:], U32) & jnp.uint32(0xFFFF0000)
    return lo | hi


def _unpack_pairs(u):
    lo = lax.bitcast_convert_type(u << 16, F32)
    hi = lax.bitcast_convert_type(u & jnp.uint32(0xFFFF0000), F32)
    return jnp.concatenate([lo, hi], axis=1).astype(BF16)


def _split_bf16(a):
    hi = a.astype(BF16)
    return hi, (a - hi.astype(F32)).astype(BF16)


def _ada_kernel(c_ref, w_ref, b_ref, o_ref):
    c = c_ref[...]
    s = c * jax.nn.sigmoid(c)
    o_ref[...] = _dot(s, w_ref[...], HIGHEST) + b_ref[...]


def _ada(cc, w_ada, b_ada):
    rows, d = cc.shape
    n = w_ada.shape[1]
    tn = 1024
    return pl.pallas_call(
        _ada_kernel,
        grid=(n // tn,),
        in_specs=[pl.BlockSpec((rows, d), lambda j: (0, 0)),
                  pl.BlockSpec((d, tn), lambda j: (0, j)),
                  pl.BlockSpec((1, tn), lambda j: (0, j))],
        out_specs=pl.BlockSpec((rows, tn), lambda j: (0, j)),
        out_shape=jax.ShapeDtypeStruct((rows, n), F32),
        compiler_params=_params(("arbitrary",)),
        name="ada",
    )(cc, w_ada, b_ada.reshape(1, n))


def _pool_constants(tm):
    pos = np.arange(tm) % GRID_W
    row = np.arange(tm) // GRID_W
    slab = min(tm, POOL_SLAB)
    pm = np.zeros((len(POOL_WINDOWS), slab, slab), np.float32)
    inv = np.zeros((tm, POOL_WIDTH), np.float32)
    for g, win in enumerate(POOL_WINDOWS):
        lo = np.clip(pos - win // 2, 0, GRID_W)
        hi = np.clip(pos + win // 2, 0, GRID_W)
        same = row[:, None] == row[None, :]
        full = (same & (pos[None, :] >= lo[:, None]) & (pos[None, :] < hi[:, None])).astype(np.float32)
        pm[g] = full[:slab, :slab]
        inv[:, g * POOL_GC:(g + 1) * POOL_GC] = (1.0 / (hi - lo).astype(np.float32))[:, None]
    t = np.arange(tm)
    same_chunk = (t[:, None] // CHUNK) == (t[None, :] // CHUNK)
    tl = (same_chunk & (t[None, :] <= t[:, None])).astype(np.float32)
    tu = (same_chunk & (t[None, :] >= t[:, None])).astype(np.float32)
    return pm, inv, tl, tu


def _modulated_norm(x, nw, scale, shift):
    xn = x * lax.rsqrt(jnp.mean(x * x, axis=-1, keepdims=True) + EPS) * nw
    return xn * (1.0 + scale) + shift


F32_ROWS = 8


def _inproj_kernel(x_ref, xprev_ref, xnext_ref, sh_ref, sc_ref, nw_ref, gbias_ref, wqkvo_ref, wgate_ref,
                   tl_ref, tu_ref, cw_ref, cb_ref, qt_ref, k_ref, vt_ref, og_ref, row_ref):
    i = pl.program_id(1)
    last = pl.num_programs(1) - 1
    xm = _modulated_norm(x_ref[0], nw_ref[...], sc_ref[0], sh_ref[0])
    xb = xm.astype(BF16)
    tm = xm.shape[0]

    xlo = (xm - xb.astype(F32)).astype(BF16)
    wg = wgate_ref[...]
    r_hi = _dot(xb, wg)
    g = r_hi + pltpu.roll(r_hi, LANES - N_GATE, axis=1) + _dot(xlo, wg) + gbias_ref[...]
    lf = jnp.minimum(g, 0.0) - jnp.log1p(jnp.exp(-jnp.abs(g)))
    lane = lax.broadcasted_iota(I32, g.shape, 1)
    lf_hi = lf.astype(BF16).astype(F32)
    lf2 = jnp.where(lane < N_GATE, lf_hi, pltpu.roll(lf - lf_hi, N_GATE, axis=1)).astype(BF16)
    r_pre = _dot(tl_ref[...], lf2)
    r_suf = _dot(tu_ref[...], lf2)
    b_pre = r_pre + pltpu.roll(r_pre, LANES - N_GATE, axis=1)
    b_suf = r_suf + pltpu.roll(r_suf, LANES - N_GATE, axis=1)
    kind = (lane & 7) >> 1
    col = jnp.where(kind == 2, b_pre, jnp.where(kind == 3, b_suf, g))
    row_ref[0] = col.T[:N_GATE]

    wqk = wqkvo_ref[:, 0:2 * M_WIDTH]
    qk = _dot(xb, wqk)
    halo = jnp.concatenate([xprev_ref[0], xnext_ref[0]], axis=0)
    qk_halo = _dot(_modulated_norm(halo, nw_ref[...], sc_ref[0], sh_ref[0]).astype(BF16), wqk)
    prev_row = jnp.where(i > 0, qk_halo[F32_ROWS - 1:F32_ROWS], 0.0)
    next_row = jnp.where(i < last, qk_halo[F32_ROWS:F32_ROWS + 1], 0.0)
    rio = lax.broadcasted_iota(I32, qk.shape, 0)
    qk_prev = jnp.where(rio == 0, prev_row, pltpu.roll(qk, 1, axis=0))
    qk_next = jnp.where(rio == tm - 1, next_row, pltpu.roll(qk, tm - 1, axis=0))
    cw = cw_ref[...]
    acc = cb_ref[...] + qk_prev * cw[0:1] + qk * cw[1:2] + qk_next * cw[2:3]
    y = acc * jax.nn.sigmoid(acc)
    q = y[:, :M_WIDTH] * (M_HEAD_DIM ** -0.5)
    for ci in range(tm // CHUNK):
        qt_ref[0, ci] = q[ci * CHUNK:(ci + 1) * CHUNK].T.astype(BF16)
    k_ref[0] = y[:, M_WIDTH:].astype(BF16)

    v = _dot(xb, wqkvo_ref[:, 2 * M_WIDTH:3 * M_WIDTH])
    for ci in range(v.shape[0] // CHUNK):
        vt_ref[0, ci] = v[ci * CHUNK:(ci + 1) * CHUNK].T.astype(BF16)
    og_ref[0] = jax.nn.sigmoid(_dot(xb, wqkvo_ref[:, 3 * M_WIDTH:4 * M_WIDTH])).astype(BF16)


def _chunk_t_out(b, l, tm):
    return (pl.BlockSpec((1, tm // CHUNK, M_WIDTH, CHUNK), lambda bi, i: (bi, i, 0, 0)),
            jax.ShapeDtypeStruct((b, l // CHUNK, M_WIDTH, CHUNK), BF16))


def _inproj(x, shift, scale, nw, gbias, wqkvo, wgate, conv_w, conv_b):
    b, l, d = x.shape
    tm = min(PROJ_TILE, l)
    per = tm // F32_ROWS
    nblk = l // F32_ROWS
    _, _, tl, tu = _pool_constants(tm)
    tok = lambda width: pl.BlockSpec((1, tm, width), lambda bi, i: (bi, i, 0))
    vec = pl.BlockSpec((1, 1, d), lambda bi, i: (bi, 0, 0))
    halo_before = pl.BlockSpec((1, F32_ROWS, d), lambda bi, i: (bi, jnp.maximum(i * per - 1, 0), 0))
    halo_after = pl.BlockSpec((1, F32_ROWS, d), lambda bi, i: (bi, jnp.minimum((i + 1) * per, nblk - 1), 0))
    consts = [nw, gbias, wqkvo, wgate, jnp.asarray(tl, BF16), jnp.asarray(tu, BF16), conv_w,
              conv_b.reshape(1, 2 * M_WIDTH)]
    tok_out = lambda width: (tok(width), jax.ShapeDtypeStruct((b, l, width), BF16))
    gate_out = (pl.BlockSpec((1, N_GATE, tm), lambda bi, i: (bi, 0, i)),
                jax.ShapeDtypeStruct((b, N_GATE, l), F32))
    outs = [_chunk_t_out(b, l, tm), tok_out(M_WIDTH), _chunk_t_out(b, l, tm), tok_out(M_WIDTH), gate_out]
    return pl.pallas_call(
        _inproj_kernel,
        grid=(b, l // tm),
        in_specs=[tok(d), halo_before, halo_after, vec, vec] + [_full(c.shape) for c in consts],
        out_specs=[spec for spec, _ in outs],
        out_shape=[shape for _, shape in outs],
        compiler_params=_params(("arbitrary", "arbitrary")),
        name="inproj",
    )(x, x, x, shift, scale, *consts)


HEADS_PER_STEP = 2
N_CHAINS = 2 * HEADS_PER_STEP
GATE_ROWS = 2 * N_CHAINS


STATE_ROWS = M_HEAD_DIM + BF16_ROWS


def _lane_pair(rows2):
    return jnp.concatenate([rows2[0:1], rows2[1:2]], axis=1)


def _block_diag(a, b):
    za, zb = jnp.zeros((a.shape[0], b.shape[1]), a.dtype), jnp.zeros((b.shape[0], a.shape[1]), a.dtype)
    return jnp.concatenate([jnp.concatenate([a, za], axis=1), jnp.concatenate([zb, b], axis=1)], axis=0)


def _side_by_side(stacked):
    d = stacked.shape[0] // 2
    return jnp.concatenate([stacked[:d], stacked[d:]], axis=1)


def _state_step(k2, vt2, i_row, b_row, btot, m_old, m_new, state):
    t = k2.shape[0]
    a = jnp.exp(btot - b_row + i_row - m_new)
    cd = jnp.exp(btot + m_old - m_new)
    ones = (lax.broadcasted_iota(I32, (STATE_ROWS - M_HEAD_DIM, 2 * t), 0) == 0).astype(F32)
    vta = (jnp.concatenate([_side_by_side(vt2).astype(F32), ones], axis=0) * a).astype(BF16)
    return cd * state + _dot(vta, _block_diag(k2[:, :M_HEAD_DIM], k2[:, M_HEAD_DIM:]))


def _chunk_out_t(qt2, k2, vt2, i_row, b_row, mask_t, m_old, state):
    t = k2.shape[0]
    g = i_row - b_row
    g_src = jnp.concatenate([jnp.broadcast_to(g[:, :t], (t, t)).T, jnp.broadcast_to(g[:, t:], (t, t)).T],
                            axis=1)
    logw = jnp.where(mask_t, b_row + g_src, -jnp.inf)
    m_inter = b_row + m_old
    m_t = jnp.maximum(jnp.max(logw, axis=0, keepdims=True), m_inter)
    q_bd = _block_diag(qt2[:M_HEAD_DIM], qt2[M_HEAD_DIM:])
    both = _dot(jnp.concatenate([k2, state.astype(BF16)], axis=0), q_bd)
    s = both[:t] * jnp.exp(logw - m_t)
    inter = both[t:]
    decay = jnp.exp(m_inter - m_t)
    s16 = s.astype(BF16)
    num = _dot(_side_by_side(vt2), _block_diag(s16[:, :t], s16[:, t:])) + decay * inter[:M_HEAD_DIM]
    den = jnp.sum(s, axis=0, keepdims=True) + decay * inter[M_HEAD_DIM:M_HEAD_DIM + 1]
    return num * (1.0 / jnp.maximum(jnp.abs(den), jnp.exp(-m_t)))


def _mlstm_kernel(qt_ref, k_ref, vt_ref, row_ref, kc_ref, vtc_ref, rowc_ref, out_ref,
                  h_ref, bt_ref, ma_ref, mold_ref, mnew_ref):
    nc, ncc, t = k_ref.shape[1], kc_ref.shape[1], CHUNK
    src = lax.broadcasted_iota(I32, (t, t), 0)
    tgt = lax.broadcasted_iota(I32, (t, t), 1)
    causal_t = jnp.concatenate([src <= tgt] * HEADS_PER_STEP, axis=1)
    masks_t = (causal_t, jnp.concatenate([src >= tgt] * HEADS_PER_STEP, axis=1))
    grow = pl.ds(pl.multiple_of(pl.program_id(1) * GATE_ROWS, GATE_ROWS), GATE_ROWS)
    head = [slice(j * M_HEAD_DIM, (j + 1) * M_HEAD_DIM) for j in range(HEADS_PER_STEP)]
    dir_rows = [slice(HEADS_PER_STEP * dirn, HEADS_PER_STEP * (dirn + 1)) for dirn in range(2)]

    def gate_pairs(rows, dirn):
        return (_lane_pair(rows[dir_rows[dirn]]),
                _lane_pair(rows[N_CHAINS + HEADS_PER_STEP * dirn:N_CHAINS + HEADS_PER_STEP * (dirn + 1)]))

    def bcast_pair(col2):
        return _lane_pair(jnp.broadcast_to(col2, (HEADS_PER_STEP, t)))

    state = [jnp.zeros((STATE_ROWS, HEADS_PER_STEP * M_HEAD_DIM), F32)] * 2
    m_st = [jnp.zeros((HEADS_PER_STEP, 1), F32)] * 2
    for dirn in range(2):
        for ci in (range(ncc) if dirn == 0 else reversed(range(ncc))):
            rows = rowc_ref[0, ci, grow, :]
            gi2 = rows[dir_rows[dirn]]
            gb2 = rows[N_CHAINS + HEADS_PER_STEP * dirn:N_CHAINS + HEADS_PER_STEP * (dirn + 1)]
            btot = gb2[:, t - 1:t] if dirn == 0 else gb2[:, 0:1]
            m_new = jnp.maximum(btot + m_st[dirn], jnp.max(btot - gb2 + gi2, axis=1, keepdims=True))
            i_row, b_row = gate_pairs(rows, dirn)
            state[dirn] = _state_step(kc_ref[0, ci], vtc_ref[0, ci], i_row, b_row, bcast_pair(btot),
                                      bcast_pair(m_st[dirn]), bcast_pair(m_new), state[dirn])
            m_st[dirn] = m_new

    gates = row_ref[0, :, grow, :]
    gi, gb = gates[:, :N_CHAINS], gates[:, N_CHAINS:]
    is_fwd = lax.broadcasted_iota(I32, gb.shape, 1) < HEADS_PER_STEP
    bt = jnp.where(is_fwd, jnp.broadcast_to(gb[:, :, t - 1:t], gb.shape),
                   jnp.broadcast_to(gb[:, :, 0:1], gb.shape))
    bt_ref[...] = bt
    ma_ref[...] = jnp.broadcast_to(jnp.max(bt - gb + gi, axis=2, keepdims=True), gb.shape)

    def m_scan(s, carry):
        m_f, m_b = carry
        cf, cb = s, nc - 1 - s
        mold_ref[cf, 0:HEADS_PER_STEP, :] = m_f[0:HEADS_PER_STEP]
        m_f = jnp.maximum(bt_ref[cf] + m_f, ma_ref[cf])
        mnew_ref[cf, 0:HEADS_PER_STEP, :] = m_f[0:HEADS_PER_STEP]
        mold_ref[cb, HEADS_PER_STEP:, :] = m_b[HEADS_PER_STEP:]
        m_b = jnp.maximum(bt_ref[cb] + m_b, ma_ref[cb])
        mnew_ref[cb, HEADS_PER_STEP:, :] = m_b[HEADS_PER_STEP:]
        return m_f, m_b

    m0 = jnp.concatenate([jnp.broadcast_to(m, (HEADS_PER_STEP, t)) for m in m_st], axis=0)
    lax.fori_loop(0, nc, m_scan, (m0, m0))

    def advance(ci, dirn, st):
        i_row, b_row = gate_pairs(row_ref[0, ci, grow, :], dirn)
        m_old, m_new, btot = [_lane_pair(tab[ci][dir_rows[dirn]]) for tab in (mold_ref, mnew_ref, bt_ref)]
        qt2, k2, vt2 = qt_ref[0, ci], k_ref[0, ci], vt_ref[0, ci]
        h_ref[dirn, ci] = _chunk_out_t(qt2, k2, vt2, i_row, b_row, masks_t[dirn], m_old, st)
        return _state_step(k2, vt2, i_row, b_row, btot, m_old, m_new, st)

    def body(s, states):
        return advance(s, 0, states[0]), advance(nc - 1 - s, 1, states[1])

    lax.fori_loop(0, nc, body, tuple(state), unroll=8)

    def norm_body(ci, _):
        ht2 = h_ref[0, ci] + h_ref[1, ci]
        for j in range(HEADS_PER_STEP):
            ht = ht2[:, j * t:(j + 1) * t]
            hn = ht * lax.rsqrt(jnp.mean(ht * ht, axis=0, keepdims=True) + EPS)
            out_ref[0, ci, :, head[j]] = hn.T.astype(BF16)
        return 0

    lax.fori_loop(0, nc, norm_body, 0, unroll=4)


def _mlstm(qt, k, vt, gate_rows, kc, vtc, gate_rows_c):
    b, l, _ = k.shape
    lc = kc.shape[1]
    nc, ncc = l // CHUNK, lc // CHUNK
    pairs = M_HEADS // HEADS_PER_STEP
    width = HEADS_PER_STEP * M_HEAD_DIM
    chunked = lambda a, n: a.reshape(b, n, CHUNK, M_WIDTH)
    by_chunk = lambda g, n: g.reshape(b, pairs * GATE_ROWS, n, CHUNK).transpose(0, 2, 1, 3)
    once = pl.Buffered(1)
    seq = lambda n: pl.BlockSpec((1, n, CHUNK, width), lambda bi, h: (bi, 0, 0, h))
    seq_t = lambda n: pl.BlockSpec((1, n, width, CHUNK), lambda bi, h: (bi, 0, h, 0))
    rows = lambda n: pl.BlockSpec((1, n, pairs * GATE_ROWS, CHUNK), lambda bi, h: (bi, 0, 0, 0),
                                  pipeline_mode=once)
    table = pltpu.VMEM((nc, N_CHAINS, CHUNK), F32)
    out = pl.pallas_call(
        _mlstm_kernel,
        grid=(b, pairs),
        in_specs=[seq_t(nc), seq(nc), seq_t(nc), rows(nc), seq(ncc), seq_t(ncc), rows(ncc)],
        out_specs=pl.BlockSpec((1, nc, CHUNK, width), lambda bi, h: (bi, 0, 0, h)),
        out_shape=jax.ShapeDtypeStruct((b, nc, CHUNK, M_WIDTH), BF16),
        scratch_shapes=[pltpu.VMEM((2, nc, M_HEAD_DIM, HEADS_PER_STEP * CHUNK), F32),
                        table, table, table, table],
        compiler_params=_params(("arbitrary", "arbitrary")),
        name="mlstm",
    )(qt, chunked(k, nc), vt, by_chunk(gate_rows, nc), chunked(kc, ncc), vtc, by_chunk(gate_rows_c, ncc))
    return out.reshape(b, l, M_WIDTH)


def _merge_kernel(x_ref, hn_ref, og_ref, sh_ref, sc_ref, g1_ref, sh2_ref, sc2_ref, nw_ref, wpin_ref,
                  wmerge_ref, pmat_ref, invc_ref, wpool_ref, pscale_ref, wbp_ref, hw_ref, wbm_ref, wout_ref,
                  nw2_ref, wr_ref, br_ref, ustrict_ref, lstrict_ref,
                  o_ref, xn_ref, slot_ref, wts_ref, cnt_ref):
    x = x_ref[0]
    xb = _modulated_norm(x, nw_ref[...], sc_ref[0], sh_ref[0]).astype(BF16)

    u = _dot(xb, wpin_ref[...])
    u_hi, u_lo = _split_bf16(u)
    invc = invc_ref[...]
    slab = pmat_ref.shape[1]
    ya = []
    for gi in range(len(POOL_WINDOWS)):
        cols = slice(gi * POOL_GC, (gi + 1) * POOL_GC)
        both = jnp.concatenate([u_hi[:, cols], u_lo[:, cols]], axis=1)
        sums = jnp.concatenate([_dot(pmat_ref[gi], both[s:s + slab]) for s in range(0, u.shape[0], slab)],
                               axis=0)
        pooled = (sums[:, :POOL_GC] + sums[:, POOL_GC:]) * invc[:, cols] - u[:, cols]
        ya.append(_dot(pooled.astype(BF16), wpool_ref[gi]))
    ya = jnp.concatenate(ya, axis=1) * pscale_ref[...]
    y = jax.nn.sigmoid(_dot(xb, wmerge_ref[:, 0:D_MODEL])) * _dot(ya.astype(BF16), wbp_ref[...])

    yb = hn_ref[0].astype(F32) * hw_ref[...] * og_ref[0].astype(F32)
    y = y + jax.nn.sigmoid(_dot(xb, wmerge_ref[:, D_MODEL:2 * D_MODEL])) * _dot(yb.astype(BF16), wbm_ref[...])
    x2 = x + g1_ref[0] * _dot(y.astype(BF16), wout_ref[...])
    o_ref[0] = x2

    _route_tile(x2, sh2_ref, sc2_ref, nw2_ref, wr_ref, br_ref, ustrict_ref, lstrict_ref,
                xn_ref, slot_ref, wts_ref, cnt_ref)


def _merge_route(x, hn, og, shift, scale, g1, shift2, scale2, nw, wpin, wmerge, wpool, pscale, wbp, hnorm_w,
                 wbm, wout, nw2, w_router, b_router):
    b, l, d = x.shape
    tm = ROUTE_TILE
    nl = l // tm
    nt = b * nl
    pm, inv, _, _ = _pool_constants(tm)
    tok = lambda width: pl.BlockSpec((1, tm, width), lambda bi, i: (bi, i, 0))
    vec = pl.BlockSpec((1, 1, d), lambda bi, i: (bi, 0, 0))
    tile = lambda r, c: pl.BlockSpec((1, r, c), lambda bi, i: (bi * nl + i, 0, 0))
    consts = [nw, wpin, wmerge, jnp.asarray(pm, BF16), jnp.asarray(inv), wpool, pscale, wbp,
              hnorm_w.reshape(1, M_WIDTH), wbm, wout, nw2] + _router_consts(w_router, b_router)
    return pl.pallas_call(
        _merge_kernel,
        grid=(b, nl),
        in_specs=[tok(d), tok(M_WIDTH), tok(M_WIDTH), vec, vec, vec, vec, vec]
        + [_full(c.shape) for c in consts],
        out_specs=[tok(d), tok(d), tile(8, tm), tile(8, tm), tile(N_EXPERTS, LANES)],
        out_shape=[jax.ShapeDtypeStruct((b, l, d), F32),
                   jax.ShapeDtypeStruct((b, l, d), BF16),
                   jax.ShapeDtypeStruct((nt, 8, tm), I32),
                   jax.ShapeDtypeStruct((nt, 8, tm), F32),
                   jax.ShapeDtypeStruct((nt, N_EXPERTS, LANES), F32)],
        compiler_params=_params(("arbitrary", "arbitrary")),
        name="merge_route",
    )(x, hn, og, shift, scale, g1, shift2, scale2, *consts)


def _route_tile(x, sh_ref, sc_ref, nw_ref, wr_ref, br_ref, ustrict_ref, lstrict_ref,
                xn_ref, slot_ref, wts_ref, cnt_ref):
    tm = x.shape[0]
    xm = _modulated_norm(x, nw_ref[...], sc_ref[0], sh_ref[0])
    xn_ref[0] = xm.astype(BF16)

    x_hi, x_lo = _split_bf16(xm)
    r_hi = _dot(x_hi, wr_ref[...])
    r = r_hi + pltpu.roll(r_hi, LANES - N_EXPERTS, axis=1) + _dot(x_lo, wr_ref[...])
    logits = r.T[:N_EXPERTS] + br_ref[...]
    eio = lax.broadcasted_iota(I32, logits.shape, 0).astype(F32)
    rest = logits
    onehots, vals = [], []
    for _ in range(TOP_K):
        mx = jnp.max(rest, axis=0, keepdims=True)
        idx = jnp.min(jnp.where(rest == mx, eio, float(N_EXPERTS)), axis=0, keepdims=True)
        oh = eio == idx
        onehots.append(oh)
        vals.append(mx)
        rest = jnp.where(oh, -jnp.inf, rest)
    exps = [jnp.exp(vk - vals[0]) for vk in vals]
    denom = exps[0] + exps[1] + exps[2] + exps[3]

    oh_all = jnp.zeros(logits.shape, F32)
    for oh in onehots:
        oh_all = oh_all + oh.astype(F32)
    cnt = jnp.sum(oh_all, axis=1, keepdims=True)
    n_al = jnp.ceil(cnt * (1.0 / SEG_ALIGN)) * SEG_ALIGN
    seg_off = _dot(lstrict_ref[...], jnp.broadcast_to(n_al, (N_EXPERTS, LANES)).astype(BF16))[:, 0:1]
    rank = _dot(oh_all.astype(BF16), ustrict_ref[...])
    base = seg_off + rank
    for kk in range(TOP_K):
        slot = jnp.sum(jnp.where(onehots[kk], base, 0.0), axis=0, keepdims=True)
        slot_ref[0, kk:kk + 1, :] = slot.astype(I32)
        wts_ref[0, kk:kk + 1, :] = exps[kk] / denom
    slot_ref[0, TOP_K:, :] = jnp.full((8 - TOP_K, tm), -1, I32)
    wts_ref[0, TOP_K:, :] = jnp.zeros((8 - TOP_K, tm), F32)
    cnt_ref[0] = jnp.broadcast_to(cnt, (N_EXPERTS, LANES))


def _router_consts(w_router, b_router):
    s = np.arange(ROUTE_TILE)
    ustrict = jnp.asarray((s[:, None] < s[None, :]).astype(np.float32), BF16)
    e = np.arange(N_EXPERTS)
    lstrict = jnp.asarray((e[None, :] < e[:, None]).astype(np.float32), BF16)
    w_cat = jnp.pad(jnp.concatenate(_split_bf16(w_router), axis=1), ((0, 0), (0, LANES - 2 * N_EXPERTS)))
    return [w_cat, b_router.reshape(N_EXPERTS, 1), ustrict, lstrict]


def _segment_starts(tile, n_s, off_s, dst_s, make_copy):
    def body(e, _):
        idx = tile * N_EXPERTS + e
        n = n_s[idx]

        @pl.when(n > 0)
        def _():
            make_copy(pl.multiple_of(off_s[idx], SEG_ALIGN), pl.multiple_of(dst_s[idx], SEG_ALIGN),
                      pl.multiple_of(n, SEG_ALIGN)).start()

        return 0

    lax.fori_loop(0, N_EXPERTS, body, 0)


def _segment_wait(tile, n_s, make_copy):
    total = n_s[pl.num_programs(0) * N_EXPERTS + tile]
    make_copy(0, 0, pl.multiple_of(total, SEG_ALIGN)).wait()


def _zero_fill(tail_s, xs_hbm, zero_ref, sem, action):
    def make_copy(dst, sz):
        return pltpu.make_async_copy(zero_ref.at[pl.ds(0, sz)], xs_hbm.at[pl.ds(dst, sz)], sem)

    def tail_body(e, _):
        n = tail_s[N_EXPERTS + e]

        @pl.when(n > 0)
        def _():
            action(make_copy(pl.multiple_of(tail_s[e], SEG_ALIGN), pl.multiple_of(n, SEG_ALIGN)))

        return 0

    lax.fori_loop(0, N_EXPERTS, tail_body, 0)

    def block_body(blk, _):
        action(make_copy(pl.multiple_of(blk * EXPERT_BLOCK, EXPERT_BLOCK), EXPERT_BLOCK))
        return 0

    lax.fori_loop(tail_s[2 * N_EXPERTS], tail_s[2 * N_EXPERTS + 1], block_body, 0)


ROUTE_CHUNK = 768
assert SEG_PAD % ROUTE_CHUNK == 0


def _dispatch_kernel(n_s, off_s, dst_s, tail_s, xn_ref, slot_ref, xs_hbm, g_ref, zero_ref, sems):
    tile = pl.program_id(0)
    last = pl.num_programs(0) - 1
    cur = tile % 2

    def copier(of_tile):
        par = of_tile % 2
        return lambda src, dst, n: pltpu.make_async_copy(
            g_ref.at[par, pl.ds(src, n)], xs_hbm.at[pl.ds(dst, n)], sems.at[par])

    @pl.when(tile >= 2)
    def _():
        _segment_wait(tile - 2, n_s, copier(tile - 2))

    x = xn_ref[...]
    tm = x.shape[0]
    slot = slot_ref[0].astype(jnp.int16)
    rio = lax.broadcasted_iota(I32, (ROUTE_CHUNK, tm), 0).astype(jnp.int16)
    one = jnp.ones((), BF16)
    for ci in range(SEG_PAD // ROUTE_CHUNK):
        rel = slot - ci * ROUTE_CHUNK
        sel = jnp.zeros((ROUTE_CHUNK, tm), BF16)
        for kk in range(TOP_K):
            sel = jnp.where(rio == rel[kk:kk + 1, :], one, sel)
        g_ref[cur, ci * ROUTE_CHUNK:(ci + 1) * ROUTE_CHUNK, :] = _pack_pairs(_dot(sel, x))

    _segment_starts(tile, n_s, off_s, dst_s, copier(tile))

    @pl.when(tile == last)
    def _():
        @pl.when(tile >= 1)
        def _():
            _segment_wait(tile - 1, n_s, copier(tile - 1))

        _segment_wait(tile, n_s, copier(tile))
        zero_ref[...] = jnp.zeros(zero_ref.shape, U32)
        _zero_fill(tail_s, xs_hbm, zero_ref, sems.at[2], lambda cp: cp.start())
        _zero_fill(tail_s, xs_hbm, zero_ref, sems.at[2], lambda cp: cp.wait())


def _dispatch(xn2, slot, n_flat, off_flat, dst_flat, tail_flat, rows_total):
    t, d = xn2.shape
    tm = ROUTE_TILE
    nt = t // tm
    return pl.pallas_call(
        _dispatch_kernel,
        grid_spec=pltpu.PrefetchScalarGridSpec(
            num_scalar_prefetch=4,
            grid=(nt,),
            in_specs=[pl.BlockSpec((tm, d), lambda i, *_: (i, 0)),
                      pl.BlockSpec((1, 8, tm), lambda i, *_: (i, 0, 0))],
            out_specs=pl.BlockSpec(memory_space=pl.ANY),
            scratch_shapes=[pltpu.VMEM((2, SEG_PAD, PAIR_WIDTH), U32),
                            pltpu.VMEM((EXPERT_BLOCK, PAIR_WIDTH), U32),
                            pltpu.SemaphoreType.DMA((3,))]),
        out_shape=jax.ShapeDtypeStruct((rows_total, PAIR_WIDTH), U32),
        compiler_params=_params(("arbitrary",)),
        name="dispatch",
    )(n_flat, off_flat, dst_flat, tail_flat, xn2, slot)


def _combine_kernel(n_s, off_s, dst_s, out_hbm, slott_ref, wtst_ref, x2_ref, g2_ref, fw_ref, y_ref,
                    buf_ref, sel_ref, sems):
    tile = pl.program_id(0)
    cur = tile % 2

    def copier(of_tile):
        par = of_tile % 2
        return lambda seg, src, n: pltpu.make_async_copy(
            out_hbm.at[pl.ds(src, n)], buf_ref.at[par, pl.ds(seg, n)], sems.at[par])

    @pl.when(tile == 0)
    def _():
        buf_ref[...] = jnp.zeros(buf_ref.shape, U32)
        _segment_starts(tile, n_s, off_s, dst_s, copier(tile))

    @pl.when(tile + 1 < pl.num_programs(0))
    def _():
        _segment_starts(tile + 1, n_s, off_s, dst_s, copier(tile + 1))

    _segment_wait(tile, n_s, copier(tile))

    st = slott_ref[...].astype(jnp.int16)
    wt = wtst_ref[...].astype(BF16)
    tm = st.shape[0]
    lio = lax.broadcasted_iota(I32, (tm, ROUTE_CHUNK), 1).astype(jnp.int16)
    for ci in range(SEG_PAD // ROUTE_CHUNK):
        rel = st - ci * ROUTE_CHUNK
        sel = jnp.zeros((tm, ROUTE_CHUNK), BF16)
        for kk in range(TOP_K):
            sel = jnp.where(lio == rel[:, kk:kk + 1], wt[:, kk:kk + 1], sel)
        sel_ref[:, ci * ROUTE_CHUNK:(ci + 1) * ROUTE_CHUNK] = sel
    x3 = x2_ref[...] + g2_ref[0] * _dot(sel_ref[...], _unpack_pairs(buf_ref[cur]))
    y_ref[...] = x3 * lax.rsqrt(jnp.mean(x3 * x3, axis=-1, keepdims=True) + EPS) * fw_ref[...]


def _combine(out_rows, slot_t, wts_t, x2_flat, g2, final_w, n_flat, off_flat, dst_flat, tiles_per_batch):
    t, d = x2_flat.shape
    tm = ROUTE_TILE
    nt = t // tm
    return pl.pallas_call(
        _combine_kernel,
        grid_spec=pltpu.PrefetchScalarGridSpec(
            num_scalar_prefetch=3,
            grid=(nt,),
            in_specs=[pl.BlockSpec(memory_space=pl.ANY),
                      pl.BlockSpec((tm, 8), lambda i, *_: (i, 0)),
                      pl.BlockSpec((tm, 8), lambda i, *_: (i, 0)),
                      pl.BlockSpec((tm, d), lambda i, *_: (i, 0)),
                      pl.BlockSpec((1, 1, d), lambda i, *_: (i // tiles_per_batch, 0, 0)),
                      pl.BlockSpec((1, d), lambda i, *_: (0, 0))],
            out_specs=pl.BlockSpec((tm, d), lambda i, *_: (i, 0)),
            scratch_shapes=[pltpu.VMEM((2, SEG_PAD, PAIR_WIDTH), U32), pltpu.VMEM((tm, SEG_PAD), BF16),
                            pltpu.SemaphoreType.DMA((2,))]),
        out_shape=jax.ShapeDtypeStruct((t, d), F32),
        compiler_params=_params(("arbitrary",)),
        name="combine",
    )(n_flat, off_flat, dst_flat, out_rows, slot_t, wts_t, x2_flat, g2, final_w.reshape(1, d))


def _expert_kernel(blk_e, nb_used, xs_ref, w1_ref, b1_ref, w2_ref, b2_ref, o_ref, w1b_ref, w2b_ref):
    i = pl.program_id(0)
    used = i < nb_used[0]

    @pl.when(jnp.logical_not(used))
    def _():
        o_ref[...] = jnp.zeros(o_ref.shape, U32)

    @pl.when(used & ((i == 0) | (blk_e[i] != blk_e[jnp.maximum(i - 1, 0)])))
    def _():
        w1b_ref[...] = w1_ref[0].astype(BF16)
        w2b_ref[...] = w2_ref[0].astype(BF16)

    @pl.when(used)
    def _():
        gu = _dot(_unpack_pairs(xs_ref[...]), w1b_ref[...]) + b1_ref[0]
        gate = jnp.minimum(gu[:, :D_FF], SWIGLU_LIMIT)
        up = jnp.clip(gu[:, D_FF:], -SWIGLU_LIMIT, SWIGLU_LIMIT)
        act = (up + 1.0) * gate * jax.nn.sigmoid(SWIGLU_ALPHA * gate)
        o = _dot(act.astype(BF16), w2b_ref[...]) + b2_ref[0]
        o_ref[...] = _pack_pairs(o.astype(BF16).astype(F32))


def _expert(xs, blk_e, nb_used, w1, b1, w2, b2):
    rows = xs.shape[0]
    nb = rows // EXPERT_BLOCK
    row_blk = lambda i, be, nu: (jnp.maximum(jnp.minimum(i, nu[0] - 1), 0), 0)
    per_e = lambda i, be, nu: (be[i], 0, 0)
    return pl.pallas_call(
        _expert_kernel,
        grid_spec=pltpu.PrefetchScalarGridSpec(
            num_scalar_prefetch=2,
            grid=(nb,),
            in_specs=[pl.BlockSpec((EXPERT_BLOCK, PAIR_WIDTH), row_blk),
                      pl.BlockSpec((1, D_MODEL, 2 * D_FF), per_e),
                      pl.BlockSpec((1, 1, 2 * D_FF), per_e),
                      pl.BlockSpec((1, D_FF, D_MODEL), per_e),
                      pl.BlockSpec((1, 1, D_MODEL), per_e)],
            out_specs=pl.BlockSpec((EXPERT_BLOCK, PAIR_WIDTH), lambda i, be, nu: (i, 0)),
            scratch_shapes=[pltpu.VMEM((D_MODEL, 2 * D_FF), BF16), pltpu.VMEM((D_FF, D_MODEL), BF16)]),
        out_shape=jax.ShapeDtypeStruct((rows, PAIR_WIDTH), U32),
        compiler_params=_params(("arbitrary",)),
        name="expert",
    )(blk_e, nb_used, xs, w1, b1.reshape(N_EXPERTS, 1, 2 * D_FF), w2, b2.reshape(N_EXPERTS, 1, D_MODEL))


def _routing_tables(cnt, nb):
    cnt = cnt.astype(I32)
    n_al = (cnt + SEG_ALIGN - 1) // SEG_ALIGN * SEG_ALIGN
    seg_off = jnp.cumsum(n_al, axis=1) - n_al
    rel = jnp.cumsum(n_al, axis=0) - n_al
    tot = jnp.sum(n_al, axis=0)
    blocks_e = (tot + EXPERT_BLOCK - 1) // EXPERT_BLOCK
    blk_end = jnp.cumsum(blocks_e)
    e_start = (blk_end - blocks_e) * EXPERT_BLOCK
    dst = e_start[None, :] + rel
    blk_e = jnp.minimum(jnp.sum(blk_end[None, :] <= jnp.arange(nb, dtype=I32)[:, None], axis=1),
                        N_EXPERTS - 1).astype(I32)
    nb_used = blk_end[-1:].astype(I32)
    tails = jnp.concatenate([e_start + tot, blocks_e * EXPERT_BLOCK - tot, nb_used,
                             jnp.full((1,), nb, I32)]).astype(I32)
    counts = jnp.concatenate([n_al.reshape(-1), jnp.sum(n_al, axis=1)])
    return (counts, seg_off.reshape(-1).astype(I32), dst.reshape(-1).astype(I32), blk_e, nb_used, tails)


def kernel(x, c, ctx, c_ctx, w_ada, b_ada, norm1_w, w_in, gate_b, conv_w, conv_b, w_pool, pool_scale,
           hnorm_w, w_bp, w_bm, w_out, norm2_w, w_router, b_router, w1, b1, w2, b2, final_norm_w):
    assert w_ada.shape[0] == 1, "single-layer kernel"
    b, l, d = x.shape
    lc = ctx.shape[1]
    assert d == D_MODEL and l % ROUTE_TILE == 0 and l % GRID_W == 0 and lc % CHUNK == 0
    t = b * l

    rows = (b + 1 + 7) // 8 * 8
    cc = jnp.zeros((rows, d), F32).at[:b].set(c).at[b].set(c_ctx)
    mod = _ada(cc, w_ada[0], b_ada[0])
    sh1, s1, g1, sh2, s2, g2 = [mod[:b, i * d:(i + 1) * d].reshape(b, 1, d) for i in range(6)]
    csh1, cs1 = [jnp.broadcast_to(mod[b, i * d:(i + 1) * d].reshape(1, 1, d), (b, 1, d)) for i in range(2)]

    w_in0 = w_in[0]
    off_gate = POOL_WIDTH + 4 * M_WIDTH
    wpin = w_in0[:, :POOL_WIDTH].astype(BF16)
    wqkvo = w_in0[:, POOL_WIDTH:off_gate].astype(BF16)
    order = [4 * (2 * dirn + is_f) + HEADS_PER_STEP * pair + j
             for pair in range(M_HEADS // HEADS_PER_STEP) for is_f in range(2) for dirn in range(2)
             for j in range(HEADS_PER_STEP)]
    wg_hi, wg_lo = _split_bf16(w_in0[:, off_gate:off_gate + N_GATE][:, order])
    wgate = jnp.pad(jnp.concatenate([wg_hi, wg_lo], axis=1), ((0, 0), (0, LANES - 2 * N_GATE)))
    wmerge = w_in0[:, off_gate + N_GATE:].astype(BF16)
    gbias = jnp.pad(gate_b[0][jnp.asarray(order)], (0, LANES - N_GATE)).reshape(1, LANES)
    nw1 = norm1_w[0].reshape(1, d)
    proj_consts = (nw1, gbias, wqkvo, wgate, conv_w[0], conv_b[0])

    qt, k, vt, og, gate_rows = _inproj(x, sh1, s1, *proj_consts)
    _, k_c, vt_c, _, gate_rows_c = _inproj(ctx, csh1, cs1, *proj_consts)
    hn = _mlstm(qt, k, vt, gate_rows, k_c, vt_c, gate_rows_c)
    x2, xn2, slot, wts, cnt = _merge_route(
        x, hn, og, sh1, s1, g1, sh2, s2, nw1, wpin, wmerge, w_pool[0].astype(BF16),
        pool_scale[0].reshape(1, POOL_WIDTH), w_bp[0].astype(BF16), hnorm_w[0], w_bm[0].astype(BF16),
        w_out[0].astype(BF16), norm2_w[0].reshape(1, d), w_router[0], b_router[0])

    nt = t // ROUTE_TILE
    nb = (t * TOP_K + nt * N_EXPERTS * (SEG_ALIGN - 1)) // EXPERT_BLOCK + N_EXPERTS
    n_flat, off_flat, dst_flat, blk_e, nb_used, tails = _routing_tables(cnt[:, :, 0], nb)
    xs = _dispatch(xn2.reshape(t, d), slot, n_flat, off_flat, dst_flat, tails, nb * EXPERT_BLOCK)
    out_rows = _expert(xs, blk_e, nb_used, w1[0], b1[0], w2[0], b2[0])
    slot_t = slot.transpose(0, 2, 1).reshape(t, 8)
    wts_t = wts.transpose(0, 2, 1).reshape(t, 8)
    y = _combine(out_rows, slot_t, wts_t, x2.reshape(t, d), g2, final_norm_w, n_flat, off_flat, dst_flat,
                 l // ROUTE_TILE)
    return y.reshape(b, l, d)
```

```python
import numpy as np
import jax
import jax.numpy as jnp
from jax import lax
from jax.experimental import pallas as pl
from jax.experimental.pallas import tpu as pltpu

F32 = jnp.float32
BF16 = jnp.bfloat16
I32 = jnp.int32
U32 = jnp.uint32
HIGHEST = lax.Precision.HIGHEST

D_MODEL = 1024
EPS = 1e-6
GRID_W = 64
POOL_WINDOWS = (2, 4, 8, 16)
POOL_WIDTH = 512
POOL_GC = 128
M_HEADS = 4
M_HEAD_DIM = 128
M_WIDTH = 512
CHUNK = 128
N_GATE = 16
N_EXPERTS = 32
TOP_K = 4
D_FF = 1024
SWIGLU_LIMIT = 7.0
SWIGLU_ALPHA = 1.702

LANES = 128
PROJ_TILE = 512
POOL_SLAB = 256
ROUTE_TILE = 512
BF16_ROWS = 16
SEG_ALIGN = 8
SEG_PAD = TOP_K * ROUTE_TILE + N_EXPERTS * SEG_ALIGN
PAIR_WIDTH = D_MODEL // 2
EXPERT_BLOCK = 1024
VMEM_LIMIT = 56 * 1024 * 1024


def _dot(a, b, precision=None):
    return jnp.dot(a, b, preferred_element_type=F32, precision=precision)


def _params(semantics):
    return pltpu.CompilerParams(dimension_semantics=semantics, vmem_limit_bytes=VMEM_LIMIT)


def _full(shape):
    nd = len(shape)
    return pl.BlockSpec(shape, lambda *_: (0,) * nd, pipeline_mode=pl.Buffered(1))


def _pack_pairs(v):
    half = v.shape[1] // 2
    lo = lax.bitcast_convert_type(v[:, :half], U32) >> 16
    hi = lax.bitcast_convert_type(v[:, half:], U32) & jnp.uint32(0xFFFF0000)
    return lo | hi


def _unpack_pairs(u):
    lo = lax.bitcast_convert_type(u << 16, F32)
    hi = lax.bitcast_convert_type(u & jnp.uint32(0xFFFF0000), F32)
    return jnp.concatenate([lo, hi], axis=1).astype(BF16)


def _split_bf16(a):
    hi = a.astype(BF16)
    return hi, (a - hi.astype(F32)).astype(BF16)


def _ada_kernel(c_ref, w_ref, b_ref, o_ref):
    c = c_ref[...]
    s = c * jax.nn.sigmoid(c)
    o_ref[...] = _dot(s, w_ref[...], HIGHEST) + b_ref[...]


def _ada(cc, w_ada, b_ada):
    rows, d = cc.shape
    n = w_ada.shape[1]
    tn = 1024
    return pl.pallas_call(
        _ada_kernel,
        grid=(n // tn,),
        in_specs=[pl.BlockSpec((rows, d), lambda j: (0, 0)),
                  pl.BlockSpec((d, tn), lambda j: (0, j)),
                  pl.BlockSpec((1, tn), lambda j: (0, j))],
        out_specs=pl.BlockSpec((rows, tn), lambda j: (0, j)),
        out_shape=jax.ShapeDtypeStruct((rows, n), F32),
        compiler_params=_params(("arbitrary",)),
        name="ada",
    )(cc, w_ada, b_ada.reshape(1, n))


def _pool_constants(tm):
    pos = np.arange(tm) % GRID_W
    row = np.arange(tm) // GRID_W
    slab = min(tm, POOL_SLAB)
    pm = np.zeros((len(POOL_WINDOWS), slab, slab), np.float32)
    inv = np.zeros((tm, POOL_WIDTH), np.float32)
    for g, win in enumerate(POOL_WINDOWS):
        lo = np.clip(pos - win // 2, 0, GRID_W)
        hi = np.clip(pos + win // 2, 0, GRID_W)
        same = row[:, None] == row[None, :]
        full = (same & (pos[None, :] >= lo[:, None]) & (pos[None, :] < hi[:, None])).astype(np.float32)
        pm[g] = full[:slab, :slab]
        inv[:, g * POOL_GC:(g + 1) * POOL_GC] = (1.0 / (hi - lo).astype(np.float32))[:, None]
    t = np.arange(tm)
    same_chunk = (t[:, None] // CHUNK) == (t[None, :] // CHUNK)
    tl = (same_chunk & (t[None, :] <= t[:, None])).astype(np.float32)
    tu = (same_chunk & (t[None, :] >= t[:, None])).astype(np.float32)
    return pm, inv, tl, tu


def _modulated_norm(x, nw, scale, shift):
    xn = x * lax.rsqrt(jnp.mean(x * x, axis=-1, keepdims=True) + EPS) * nw
    return xn * (1.0 + scale) + shift


F32_ROWS = 8


def _inproj_kernel(x_ref, xprev_ref, xnext_ref, sh_ref, sc_ref, nw_ref, gbias_ref, wqkvo_ref, wgate_ref,
                   tl_ref, tu_ref, cw_ref, cb_ref, qt_ref, k_ref, vt_ref, og_ref, row_ref):
    i = pl.program_id(1)
    last = pl.num_programs(1) - 1
    xm = _modulated_norm(x_ref[0], nw_ref[...], sc_ref[0], sh_ref[0])
    xb = xm.astype(BF16)
    tm = xm.shape[0]

    xlo = (xm - xb.astype(F32)).astype(BF16)
    wg = wgate_ref[...]
    r_hi = _dot(xb, wg)
    g = r_hi + pltpu.roll(r_hi, LANES - N_GATE, axis=1) + _dot(xlo, wg) + gbias_ref[...]
    lf = jnp.minimum(g, 0.0) - jnp.log1p(jnp.exp(-jnp.abs(g)))
    lane = lax.broadcasted_iota(I32, g.shape, 1)
    lf_hi = lf.astype(BF16).astype(F32)
    lf2 = jnp.where(lane < N_GATE, lf_hi, pltpu.roll(lf - lf_hi, N_GATE, axis=1)).astype(BF16)
    r_pre = _dot(tl_ref[...], lf2)
    r_suf = _dot(tu_ref[...], lf2)
    b_pre = r_pre + pltpu.roll(r_pre, LANES - N_GATE, axis=1)
    b_suf = r_suf + pltpu.roll(r_suf, LANES - N_GATE, axis=1)
    kind = (lane & 7) >> 1
    col = jnp.where(kind == 2, b_pre, jnp.where(kind == 3, b_suf, g))
    rows = col.T[:N_GATE]
    for ci in range(tm // CHUNK):
        row_ref[0, ci] = rows[:, ci * CHUNK:(ci + 1) * CHUNK]

    wqk = wqkvo_ref[:, 0:2 * M_WIDTH]
    qk = _dot(xb, wqk)
    halo = jnp.concatenate([xprev_ref[0], xnext_ref[0]], axis=0)
    qk_halo = _dot(_modulated_norm(halo, nw_ref[...], sc_ref[0], sh_ref[0]).astype(BF16), wqk)
    prev_row = jnp.where(i > 0, qk_halo[F32_ROWS - 1:F32_ROWS], 0.0)
    next_row = jnp.where(i < last, qk_halo[F32_ROWS:F32_ROWS + 1], 0.0)
    rio = lax.broadcasted_iota(I32, qk.shape, 0)
    qk_prev = jnp.where(rio == 0, prev_row, pltpu.roll(qk, 1, axis=0))
    qk_next = jnp.where(rio == tm - 1, next_row, pltpu.roll(qk, tm - 1, axis=0))
    cw = cw_ref[...]
    acc = cb_ref[...] + qk_prev * cw[0:1] + qk * cw[1:2] + qk_next * cw[2:3]
    y = acc * jax.nn.sigmoid(acc)
    q = y[:, :M_WIDTH] * (M_HEAD_DIM ** -0.5)
    for ci in range(tm // CHUNK):
        qt_ref[0, ci] = q[ci * CHUNK:(ci + 1) * CHUNK].T.astype(BF16)
    k_ref[0] = y[:, M_WIDTH:].astype(BF16)

    v = _dot(xb, wqkvo_ref[:, 2 * M_WIDTH:3 * M_WIDTH])
    for ci in range(v.shape[0] // CHUNK):
        vt_ref[0, ci] = v[ci * CHUNK:(ci + 1) * CHUNK].T.astype(BF16)
    og_ref[0] = jax.nn.sigmoid(_dot(xb, wqkvo_ref[:, 3 * M_WIDTH:4 * M_WIDTH])).astype(BF16)


def _chunk_t_out(b, l, tm):
    return (pl.BlockSpec((1, tm // CHUNK, M_WIDTH, CHUNK), lambda bi, i: (bi, i, 0, 0)),
            jax.ShapeDtypeStruct((b, l // CHUNK, M_WIDTH, CHUNK), BF16))


def _inproj(x, shift, scale, nw, gbias, wqkvo, wgate, conv_w, conv_b):
    b, l, d = x.shape
    tm = min(PROJ_TILE, l)
    per = tm // F32_ROWS
    nblk = l // F32_ROWS
    _, _, tl, tu = _pool_constants(tm)
    tok = lambda width: pl.BlockSpec((1, tm, width), lambda bi, i: (bi, i, 0))
    vec = pl.BlockSpec((1, 1, d), lambda bi, i: (bi, 0, 0))
    halo_before = pl.BlockSpec((1, F32_ROWS, d), lambda bi, i: (bi, jnp.maximum(i * per - 1, 0), 0))
    halo_after = pl.BlockSpec((1, F32_ROWS, d), lambda bi, i: (bi, jnp.minimum((i + 1) * per, nblk - 1), 0))
    consts = [nw, gbias, wqkvo, wgate, jnp.asarray(tl, BF16), jnp.asarray(tu, BF16), conv_w,
              conv_b.reshape(1, 2 * M_WIDTH)]
    tok_out = lambda width: (tok(width), jax.ShapeDtypeStruct((b, l, width), BF16))
    gate_out = (pl.BlockSpec((1, tm // CHUNK, N_GATE, CHUNK), lambda bi, i: (bi, i, 0, 0)),
                jax.ShapeDtypeStruct((b, l // CHUNK, N_GATE, CHUNK), F32))
    outs = [_chunk_t_out(b, l, tm), tok_out(M_WIDTH), _chunk_t_out(b, l, tm), tok_out(M_WIDTH), gate_out]
    return pl.pallas_call(
        _inproj_kernel,
        grid=(b, l // tm),
        in_specs=[tok(d), halo_before, halo_after, vec, vec] + [_full(c.shape) for c in consts],
        out_specs=[spec for spec, _ in outs],
        out_shape=[shape for _, shape in outs],
        compiler_params=_params(("arbitrary", "arbitrary")),
        name="inproj",
    )(x, x, x, shift, scale, *consts)


HEADS_PER_STEP = 2
N_CHAINS = 2 * HEADS_PER_STEP
GATE_ROWS = 2 * N_CHAINS


STATE_ROWS = M_HEAD_DIM + BF16_ROWS


def _lane_pair(rows2):
    return jnp.concatenate([rows2[0:1], rows2[1:2]], axis=1)


def _block_diag(a, b):
    za, zb = jnp.zeros((a.shape[0], b.shape[1]), a.dtype), jnp.zeros((b.shape[0], a.shape[1]), a.dtype)
    return jnp.concatenate([jnp.concatenate([a, za], axis=1), jnp.concatenate([zb, b], axis=1)], axis=0)


def _side_by_side(stacked):
    d = stacked.shape[0] // 2
    return jnp.concatenate([stacked[:d], stacked[d:]], axis=1)


def _state_step(k2, vt2, i_row, b_row, btot, m_old, m_new, state):
    t = k2.shape[0]
    a = jnp.exp(btot - b_row + i_row - m_new)
    cd = jnp.exp(btot + m_old - m_new)
    ones = (lax.broadcasted_iota(I32, (STATE_ROWS - M_HEAD_DIM, 2 * t), 0) == 0).astype(F32)
    vta = (jnp.concatenate([_side_by_side(vt2).astype(F32), ones], axis=0) * a).astype(BF16)
    return cd * state + _dot(vta, _block_diag(k2[:, :M_HEAD_DIM], k2[:, M_HEAD_DIM:]))


def _chunk_out_t(qt2, k2, vt2, i_row, b_row, mask_t, m_old, state):
    t = k2.shape[0]
    g = i_row - b_row
    g_src = jnp.concatenate([jnp.broadcast_to(g[:, :t], (t, t)).T, jnp.broadcast_to(g[:, t:], (t, t)).T],
                            axis=1)
    logw = jnp.where(mask_t, b_row + g_src, -jnp.inf)
    m_inter = b_row + m_old
    m_t = jnp.maximum(jnp.max(logw, axis=0, keepdims=True), m_inter)
    q_bd = _block_diag(qt2[:M_HEAD_DIM], qt2[M_HEAD_DIM:])
    both = _dot(jnp.concatenate([k2, state.astype(BF16)], axis=0), q_bd)
    s = both[:t] * jnp.exp(logw - m_t)
    inter = both[t:]
    decay = jnp.exp(m_inter - m_t)
    s16 = s.astype(BF16)
    num = _dot(_side_by_side(vt2), _block_diag(s16[:, :t], s16[:, t:])) + decay * inter[:M_HEAD_DIM]
    den = jnp.sum(s, axis=0, keepdims=True) + decay * inter[M_HEAD_DIM:M_HEAD_DIM + 1]
    return num * (1.0 / jnp.maximum(jnp.abs(den), jnp.exp(-m_t)))


def _mlstm_kernel(qt_ref, k_ref, vt_ref, row_ref, kc_ref, vtc_ref, rowc_ref, out_ref,
                  h_ref, bt_ref, ma_ref, mold_ref, mnew_ref):
    nc, ncc, t = k_ref.shape[1], kc_ref.shape[1], CHUNK
    src = lax.broadcasted_iota(I32, (t, t), 0)
    tgt = lax.broadcasted_iota(I32, (t, t), 1)
    causal_t = jnp.concatenate([src <= tgt] * HEADS_PER_STEP, axis=1)
    masks_t = (causal_t, jnp.concatenate([src >= tgt] * HEADS_PER_STEP, axis=1))
    grow = pl.ds(pl.multiple_of(pl.program_id(1) * GATE_ROWS, GATE_ROWS), GATE_ROWS)
    head = [slice(j * M_HEAD_DIM, (j + 1) * M_HEAD_DIM) for j in range(HEADS_PER_STEP)]
    dir_rows = [slice(HEADS_PER_STEP * dirn, HEADS_PER_STEP * (dirn + 1)) for dirn in range(2)]

    def gate_pairs(rows, dirn):
        return (_lane_pair(rows[dir_rows[dirn]]),
                _lane_pair(rows[N_CHAINS + HEADS_PER_STEP * dirn:N_CHAINS + HEADS_PER_STEP * (dirn + 1)]))

    def bcast_pair(col2):
        return _lane_pair(jnp.broadcast_to(col2, (HEADS_PER_STEP, t)))

    state = [jnp.zeros((STATE_ROWS, HEADS_PER_STEP * M_HEAD_DIM), F32)] * 2
    m_st = [jnp.zeros((HEADS_PER_STEP, 1), F32)] * 2
    for dirn in range(2):
        for ci in (range(ncc) if dirn == 0 else reversed(range(ncc))):
            rows = rowc_ref[0, ci, grow, :]
            gi2 = rows[dir_rows[dirn]]
            gb2 = rows[N_CHAINS + HEADS_PER_STEP * dirn:N_CHAINS + HEADS_PER_STEP * (dirn + 1)]
            btot = gb2[:, t - 1:t] if dirn == 0 else gb2[:, 0:1]
            m_new = jnp.maximum(btot + m_st[dirn], jnp.max(btot - gb2 + gi2, axis=1, keepdims=True))
            i_row, b_row = gate_pairs(rows, dirn)
            state[dirn] = _state_step(kc_ref[0, ci], vtc_ref[0, ci], i_row, b_row, bcast_pair(btot),
                                      bcast_pair(m_st[dirn]), bcast_pair(m_new), state[dirn])
            m_st[dirn] = m_new

    gates = row_ref[0, :, grow, :]
    gi, gb = gates[:, :N_CHAINS], gates[:, N_CHAINS:]
    is_fwd = lax.broadcasted_iota(I32, gb.shape, 1) < HEADS_PER_STEP
    bt = jnp.where(is_fwd, jnp.broadcast_to(gb[:, :, t - 1:t], gb.shape),
                   jnp.broadcast_to(gb[:, :, 0:1], gb.shape))
    bt_ref[...] = bt
    ma_ref[...] = jnp.broadcast_to(jnp.max(bt - gb + gi, axis=2, keepdims=True), gb.shape)

    def m_scan(s, carry):
        m_f, m_b = carry
        cf, cb = s, nc - 1 - s
        mold_ref[cf, 0:HEADS_PER_STEP, :] = m_f[0:HEADS_PER_STEP]
        m_f = jnp.maximum(bt_ref[cf] + m_f, ma_ref[cf])
        mnew_ref[cf, 0:HEADS_PER_STEP, :] = m_f[0:HEADS_PER_STEP]
        mold_ref[cb, HEADS_PER_STEP:, :] = m_b[HEADS_PER_STEP:]
        m_b = jnp.maximum(bt_ref[cb] + m_b, ma_ref[cb])
        mnew_ref[cb, HEADS_PER_STEP:, :] = m_b[HEADS_PER_STEP:]
        return m_f, m_b

    m0 = jnp.concatenate([jnp.broadcast_to(m, (HEADS_PER_STEP, t)) for m in m_st], axis=0)
    lax.fori_loop(0, nc, m_scan, (m0, m0))

    def advance(ci, dirn, st):
        i_row, b_row = gate_pairs(row_ref[0, ci, grow, :], dirn)
        m_old, m_new, btot = [_lane_pair(tab[ci][dir_rows[dirn]]) for tab in (mold_ref, mnew_ref, bt_ref)]
        qt2, k2, vt2 = qt_ref[0, ci], k_ref[0, ci], vt_ref[0, ci]
        h_ref[dirn, ci] = _chunk_out_t(qt2, k2, vt2, i_row, b_row, masks_t[dirn], m_old, st)
        return _state_step(k2, vt2, i_row, b_row, btot, m_old, m_new, st)

    def body(s, states):
        return advance(s, 0, states[0]), advance(nc - 1 - s, 1, states[1])

    lax.fori_loop(0, nc, body, tuple(state), unroll=8)

    def norm_body(ci, _):
        ht2 = h_ref[0, ci] + h_ref[1, ci]
        for j in range(HEADS_PER_STEP):
            ht = ht2[:, j * t:(j + 1) * t]
            hn = ht * lax.rsqrt(jnp.mean(ht * ht, axis=0, keepdims=True) + EPS)
            out_ref[0, ci, :, head[j]] = hn.T.astype(BF16)
        return 0

    lax.fori_loop(0, nc, norm_body, 0, unroll=4)


def _mlstm(qt, k, vt, gate_rows, kc, vtc, gate_rows_c):
    b, l, _ = k.shape
    lc = kc.shape[1]
    nc, ncc = l // CHUNK, lc // CHUNK
    pairs = M_HEADS // HEADS_PER_STEP
    width = HEADS_PER_STEP * M_HEAD_DIM
    chunked = lambda a, n: a.reshape(b, n, CHUNK, M_WIDTH)
    once = pl.Buffered(1)
    seq = lambda n: pl.BlockSpec((1, n, CHUNK, width), lambda bi, h: (bi, 0, 0, h))
    seq_t = lambda n: pl.BlockSpec((1, n, width, CHUNK), lambda bi, h: (bi, 0, h, 0))
    rows = lambda n: pl.BlockSpec((1, n, pairs * GATE_ROWS, CHUNK), lambda bi, h: (bi, 0, 0, 0),
                                  pipeline_mode=once)
    table = pltpu.VMEM((nc, N_CHAINS, CHUNK), F32)
    out = pl.pallas_call(
        _mlstm_kernel,
        grid=(b, pairs),
        in_specs=[seq_t(nc), seq(nc), seq_t(nc), rows(nc), seq(ncc), seq_t(ncc), rows(ncc)],
        out_specs=pl.BlockSpec((1, nc, CHUNK, width), lambda bi, h: (bi, 0, 0, h)),
        out_shape=jax.ShapeDtypeStruct((b, nc, CHUNK, M_WIDTH), BF16),
        scratch_shapes=[pltpu.VMEM((2, nc, M_HEAD_DIM, HEADS_PER_STEP * CHUNK), F32),
                        table, table, table, table],
        compiler_params=_params(("arbitrary", "arbitrary")),
        name="mlstm",
    )(qt, chunked(k, nc), vt, gate_rows, chunked(kc, ncc), vtc, gate_rows_c)
    return out.reshape(b, l, M_WIDTH)


def _merge_kernel(x_ref, hn_ref, og_ref, sh_ref, sc_ref, g1_ref, sh2_ref, sc2_ref, nw_ref, wpin_ref,
                  wmerge_ref, pmat_ref, invc_ref, wpool_ref, pscale_ref, wbp_ref, hw_ref, wbm_ref, wout_ref,
                  nw2_ref, wr_ref, br_ref, ustrict_ref, lstrict_ref,
                  o_ref, xn_ref, slot_ref, sw_ref, cnt_ref):
    x = x_ref[0]
    xb = _modulated_norm(x, nw_ref[...], sc_ref[0], sh_ref[0]).astype(BF16)

    u = _dot(xb, wpin_ref[...])
    u_hi, u_lo = _split_bf16(u)
    invc = invc_ref[...]
    slab = pmat_ref.shape[1]
    ya = []
    for gi in range(len(POOL_WINDOWS)):
        cols = slice(gi * POOL_GC, (gi + 1) * POOL_GC)
        both = jnp.concatenate([u_hi[:, cols], u_lo[:, cols]], axis=1)
        sums = jnp.concatenate([_dot(pmat_ref[gi], both[s:s + slab]) for s in range(0, u.shape[0], slab)],
                               axis=0)
        pooled = (sums[:, :POOL_GC] + sums[:, POOL_GC:]) * invc[:, cols] - u[:, cols]
        ya.append(_dot(pooled.astype(BF16), wpool_ref[gi]))
    ya = jnp.concatenate(ya, axis=1) * pscale_ref[...]
    y = jax.nn.sigmoid(_dot(xb, wmerge_ref[:, 0:D_MODEL])) * _dot(ya.astype(BF16), wbp_ref[...])

    yb = hn_ref[0].astype(F32) * hw_ref[...] * og_ref[0].astype(F32)
    y = y + jax.nn.sigmoid(_dot(xb, wmerge_ref[:, D_MODEL:2 * D_MODEL])) * _dot(yb.astype(BF16), wbm_ref[...])
    x2 = x + g1_ref[0] * _dot(y.astype(BF16), wout_ref[...])
    o_ref[0] = x2

    _route_tile(x2, sh2_ref, sc2_ref, nw2_ref, wr_ref, br_ref, ustrict_ref, lstrict_ref,
                xn_ref, slot_ref, sw_ref, cnt_ref)


def _merge_route(x, hn, og, shift, scale, g1, shift2, scale2, nw, wpin, wmerge, wpool, pscale, wbp, hnorm_w,
                 wbm, wout, nw2, w_router, b_router):
    b, l, d = x.shape
    tm = ROUTE_TILE
    nl = l // tm
    nt = b * nl
    pm, inv, _, _ = _pool_constants(tm)
    tok = lambda width: pl.BlockSpec((1, tm, width), lambda bi, i: (bi, i, 0))
    vec = pl.BlockSpec((1, 1, d), lambda bi, i: (bi, 0, 0))
    tile = lambda r, c: pl.BlockSpec((1, r, c), lambda bi, i: (bi * nl + i, 0, 0))
    consts = [nw, wpin, wmerge, jnp.asarray(pm, BF16), jnp.asarray(inv), wpool, pscale, wbp,
              hnorm_w.reshape(1, M_WIDTH), wbm, wout, nw2] + _router_consts(w_router, b_router)
    return pl.pallas_call(
        _merge_kernel,
        grid=(b, nl),
        in_specs=[tok(d), tok(M_WIDTH), tok(M_WIDTH), vec, vec, vec, vec, vec]
        + [_full(c.shape) for c in consts],
        out_specs=[tok(d), tok(d), tile(8, tm), pl.BlockSpec((tm, 2 * TOP_K), lambda bi, i: (bi * nl + i, 0)),
                   tile(N_EXPERTS, LANES)],
        out_shape=[jax.ShapeDtypeStruct((b, l, d), F32),
                   jax.ShapeDtypeStruct((b, l, d), BF16),
                   jax.ShapeDtypeStruct((nt, 8, tm), I32),
                   jax.ShapeDtypeStruct((nt * tm, 2 * TOP_K), F32),
                   jax.ShapeDtypeStruct((nt, N_EXPERTS, LANES), F32)],
        compiler_params=_params(("arbitrary", "arbitrary")),
        name="merge_route",
    )(x, hn, og, shift, scale, g1, shift2, scale2, *consts)


def _route_tile(x, sh_ref, sc_ref, nw_ref, wr_ref, br_ref, ustrict_ref, lstrict_ref,
                xn_ref, slot_ref, sw_ref, cnt_ref):
    tm = x.shape[0]
    xm = _modulated_norm(x, nw_ref[...], sc_ref[0], sh_ref[0])
    xn_ref[0] = xm.astype(BF16)

    x_hi, x_lo = _split_bf16(xm)
    r_hi = _dot(x_hi, wr_ref[...])
    r = r_hi + pltpu.roll(r_hi, LANES - N_EXPERTS, axis=1) + _dot(x_lo, wr_ref[...])
    logits = r.T[:N_EXPERTS] + br_ref[...]
    eio = lax.broadcasted_iota(I32, logits.shape, 0).astype(F32)
    rest = logits
    onehots, vals = [], []
    for _ in range(TOP_K):
        mx = jnp.max(rest, axis=0, keepdims=True)
        idx = jnp.min(jnp.where(rest == mx, eio, float(N_EXPERTS)), axis=0, keepdims=True)
        oh = eio == idx
        onehots.append(oh)
        vals.append(mx)
        rest = jnp.where(oh, -jnp.inf, rest)
    exps = [jnp.exp(vk - vals[0]) for vk in vals]
    denom = exps[0] + exps[1] + exps[2] + exps[3]

    oh_all = jnp.zeros(logits.shape, F32)
    for oh in onehots:
        oh_all = oh_all + oh.astype(F32)
    cnt = jnp.sum(oh_all, axis=1, keepdims=True)
    n_al = jnp.ceil(cnt * (1.0 / SEG_ALIGN)) * SEG_ALIGN
    seg_off = _dot(lstrict_ref[...], jnp.broadcast_to(n_al, (N_EXPERTS, LANES)).astype(BF16))[:, 0:1]
    rank = _dot(oh_all.astype(BF16), ustrict_ref[...])
    base = seg_off + rank
    slots = [jnp.sum(jnp.where(onehots[kk], base, 0.0), axis=0, keepdims=True) for kk in range(TOP_K)]
    for kk in range(TOP_K):
        slot_ref[0, kk:kk + 1, :] = slots[kk].astype(I32)
    slot_ref[0, TOP_K:, :] = jnp.full((8 - TOP_K, tm), -1, I32)
    per_token = jnp.concatenate(slots + [e / denom for e in exps] + [jnp.zeros((LANES - 2 * TOP_K, tm), F32)],
                                axis=0)
    sw_ref[...] = per_token.T[:, :2 * TOP_K]
    cnt_ref[0] = jnp.broadcast_to(cnt, (N_EXPERTS, LANES))


def _router_consts(w_router, b_router):
    s = np.arange(ROUTE_TILE)
    ustrict = jnp.asarray((s[:, None] < s[None, :]).astype(np.float32), BF16)
    e = np.arange(N_EXPERTS)
    lstrict = jnp.asarray((e[None, :] < e[:, None]).astype(np.float32), BF16)
    w_cat = jnp.pad(jnp.concatenate(_split_bf16(w_router), axis=1), ((0, 0), (0, LANES - 2 * N_EXPERTS)))
    return [w_cat, b_router.reshape(N_EXPERTS, 1), ustrict, lstrict]


def _segment_starts(tile, n_s, off_s, dst_s, make_copy):
    def body(e, _):
        idx = tile * N_EXPERTS + e
        n = n_s[idx]

        @pl.when(n > 0)
        def _():
            make_copy(pl.multiple_of(off_s[idx], SEG_ALIGN), pl.multiple_of(dst_s[idx], SEG_ALIGN),
                      pl.multiple_of(n, SEG_ALIGN)).start()

        return 0

    lax.fori_loop(0, N_EXPERTS, body, 0)


def _segment_wait(tile, n_s, make_copy):
    total = n_s[pl.num_programs(0) * N_EXPERTS + tile]
    make_copy(0, 0, pl.multiple_of(total, SEG_ALIGN)).wait()


def _zero_fill(tail_s, xs_hbm, zero_ref, sem, action):
    def make_copy(dst, sz):
        return pltpu.make_async_copy(zero_ref.at[pl.ds(0, sz)], xs_hbm.at[pl.ds(dst, sz)], sem)

    def tail_body(e, _):
        n = tail_s[N_EXPERTS + e]

        @pl.when(n > 0)
        def _():
            action(make_copy(pl.multiple_of(tail_s[e], SEG_ALIGN), pl.multiple_of(n, SEG_ALIGN)))

        return 0

    lax.fori_loop(0, N_EXPERTS, tail_body, 0)

    def block_body(blk, _):
        action(make_copy(pl.multiple_of(blk * EXPERT_BLOCK, EXPERT_BLOCK), EXPERT_BLOCK))
        return 0

    lax.fori_loop(tail_s[2 * N_EXPERTS], tail_s[2 * N_EXPERTS + 1], block_body, 0)


ROUTE_CHUNK = 768
assert SEG_PAD % ROUTE_CHUNK == 0


def _dispatch_kernel(n_s, off_s, dst_s, tail_s, xn_ref, slot_ref, xs_hbm, g_ref, zero_ref, sems):
    tile = pl.program_id(0)
    last = pl.num_programs(0) - 1
    cur = tile % 2

    def copier(of_tile):
        par = of_tile % 2
        return lambda src, dst, n: pltpu.make_async_copy(
            g_ref.at[par, pl.ds(src, n)], xs_hbm.at[pl.ds(dst, n)], sems.at[par])

    @pl.when(tile >= 2)
    def _():
        _segment_wait(tile - 2, n_s, copier(tile - 2))

    x = xn_ref[...]
    tm = x.shape[0]
    slot = slot_ref[0].astype(jnp.int16)
    rio = lax.broadcasted_iota(I32, (ROUTE_CHUNK, tm), 0).astype(jnp.int16)
    one = jnp.ones((), BF16)
    for ci in range(SEG_PAD // ROUTE_CHUNK):
        rel = slot - ci * ROUTE_CHUNK
        sel = jnp.zeros((ROUTE_CHUNK, tm), BF16)
        for kk in range(TOP_K):
            sel = jnp.where(rio == rel[kk:kk + 1, :], one, sel)
        g_ref[cur, ci * ROUTE_CHUNK:(ci + 1) * ROUTE_CHUNK, :] = _pack_pairs(_dot(sel, x))

    _segment_starts(tile, n_s, off_s, dst_s, copier(tile))

    @pl.when(tile == last)
    def _():
        @pl.when(tile >= 1)
        def _():
            _segment_wait(tile - 1, n_s, copier(tile - 1))

        _segment_wait(tile, n_s, copier(tile))
        zero_ref[...] = jnp.zeros(zero_ref.shape, U32)
        _zero_fill(tail_s, xs_hbm, zero_ref, sems.at[2], lambda cp: cp.start())
        _zero_fill(tail_s, xs_hbm, zero_ref, sems.at[2], lambda cp: cp.wait())


def _dispatch(xn2, slot, n_flat, off_flat, dst_flat, tail_flat, rows_total):
    t, d = xn2.shape
    tm = ROUTE_TILE
    nt = t // tm
    return pl.pallas_call(
        _dispatch_kernel,
        grid_spec=pltpu.PrefetchScalarGridSpec(
            num_scalar_prefetch=4,
            grid=(nt,),
            in_specs=[pl.BlockSpec((tm, d), lambda i, *_: (i, 0)),
                      pl.BlockSpec((1, 8, tm), lambda i, *_: (i, 0, 0))],
            out_specs=pl.BlockSpec(memory_space=pl.ANY),
            scratch_shapes=[pltpu.VMEM((2, SEG_PAD, PAIR_WIDTH), U32),
                            pltpu.VMEM((EXPERT_BLOCK, PAIR_WIDTH), U32),
                            pltpu.SemaphoreType.DMA((3,))]),
        out_shape=jax.ShapeDtypeStruct((rows_total, PAIR_WIDTH), U32),
        compiler_params=_params(("arbitrary",)),
        name="dispatch",
    )(n_flat, off_flat, dst_flat, tail_flat, xn2, slot)


def _combine_kernel(n_s, off_s, dst_s, out_hbm, sw_ref, x2_ref, g2_ref, fw_ref, y_ref,
                    buf_ref, sel_ref, sems):
    tile = pl.program_id(0)
    cur = tile % 2

    def copier(of_tile):
        par = of_tile % 2
        return lambda seg, src, n: pltpu.make_async_copy(
            out_hbm.at[pl.ds(src, n)], buf_ref.at[par, pl.ds(seg, n)], sems.at[par])

    @pl.when(tile == 0)
    def _():
        buf_ref[...] = jnp.zeros(buf_ref.shape, U32)
        _segment_starts(tile, n_s, off_s, dst_s, copier(tile))

    @pl.when(tile + 1 < pl.num_programs(0))
    def _():
        _segment_starts(tile + 1, n_s, off_s, dst_s, copier(tile + 1))

    _segment_wait(tile, n_s, copier(tile))

    sw = sw_ref[...]
    st = sw[:, :TOP_K].astype(I32).astype(jnp.int16)
    wt = sw[:, TOP_K:].astype(BF16)
    tm = st.shape[0]
    lio = lax.broadcasted_iota(I32, (tm, ROUTE_CHUNK), 1).astype(jnp.int16)
    for ci in range(SEG_PAD // ROUTE_CHUNK):
        rel = st - ci * ROUTE_CHUNK
        sel = jnp.zeros((tm, ROUTE_CHUNK), BF16)
        for kk in range(TOP_K):
            sel = jnp.where(lio == rel[:, kk:kk + 1], wt[:, kk:kk + 1], sel)
        sel_ref[:, ci * ROUTE_CHUNK:(ci + 1) * ROUTE_CHUNK] = sel
    x3 = x2_ref[...] + g2_ref[0] * _dot(sel_ref[...], _unpack_pairs(buf_ref[cur]))
    y_ref[...] = x3 * lax.rsqrt(jnp.mean(x3 * x3, axis=-1, keepdims=True) + EPS) * fw_ref[...]


def _combine(out_rows, slot_wts, x2_flat, g2, final_w, n_flat, off_flat, dst_flat, tiles_per_batch):
    t, d = x2_flat.shape
    tm = ROUTE_TILE
    nt = t // tm
    return pl.pallas_call(
        _combine_kernel,
        grid_spec=pltpu.PrefetchScalarGridSpec(
            num_scalar_prefetch=3,
            grid=(nt,),
            in_specs=[pl.BlockSpec(memory_space=pl.ANY),
                      pl.BlockSpec((tm, 2 * TOP_K), lambda i, *_: (i, 0)),
                      pl.BlockSpec((tm, d), lambda i, *_: (i, 0)),
                      pl.BlockSpec((1, 1, d), lambda i, *_: (i // tiles_per_batch, 0, 0)),
                      pl.BlockSpec((1, d), lambda i, *_: (0, 0))],
            out_specs=pl.BlockSpec((tm, d), lambda i, *_: (i, 0)),
            scratch_shapes=[pltpu.VMEM((2, SEG_PAD, PAIR_WIDTH), U32), pltpu.VMEM((tm, SEG_PAD), BF16),
                            pltpu.SemaphoreType.DMA((2,))]),
        out_shape=jax.ShapeDtypeStruct((t, d), F32),
        compiler_params=_params(("arbitrary",)),
        name="combine",
    )(n_flat, off_flat, dst_flat, out_rows, slot_wts, x2_flat, g2, final_w.reshape(1, d))


def _expert_kernel(blk_e, nb_used, xs_ref, w1_ref, b1_ref, w2_ref, b2_ref, o_ref, w1b_ref, w2b_ref):
    i = pl.program_id(0)
    used = i < nb_used[0]

    @pl.when(jnp.logical_not(used))
    def _():
        o_ref[...] = jnp.zeros(o_ref.shape, U32)

    @pl.when(used & ((i == 0) | (blk_e[i] != blk_e[jnp.maximum(i - 1, 0)])))
    def _():
        w1b_ref[...] = w1_ref[0].astype(BF16)
        w2b_ref[...] = w2_ref[0].astype(BF16)

    @pl.when(used)
    def _():
        gu = _dot(_unpack_pairs(xs_ref[...]), w1b_ref[...]) + b1_ref[0]
        gate = jnp.minimum(gu[:, :D_FF], SWIGLU_LIMIT)
        up = jnp.clip(gu[:, D_FF:], -SWIGLU_LIMIT, SWIGLU_LIMIT)
        act = (up + 1.0) * gate * jax.nn.sigmoid(SWIGLU_ALPHA * gate)
        o = _dot(act.astype(BF16), w2b_ref[...]) + b2_ref[0]
        o_ref[...] = _pack_pairs(o.astype(BF16).astype(F32))


def _expert(xs, blk_e, nb_used, w1, b1, w2, b2):
    rows = xs.shape[0]
    nb = rows // EXPERT_BLOCK
    row_blk = lambda i, be, nu: (jnp.maximum(jnp.minimum(i, nu[0] - 1), 0), 0)
    per_e = lambda i, be, nu: (be[i], 0, 0)
    return pl.pallas_call(
        _expert_kernel,
        grid_spec=pltpu.PrefetchScalarGridSpec(
            num_scalar_prefetch=2,
            grid=(nb,),
            in_specs=[pl.BlockSpec((EXPERT_BLOCK, PAIR_WIDTH), row_blk),
                      pl.BlockSpec((1, D_MODEL, 2 * D_FF), per_e),
                      pl.BlockSpec((1, 1, 2 * D_FF), per_e),
                      pl.BlockSpec((1, D_FF, D_MODEL), per_e),
                      pl.BlockSpec((1, 1, D_MODEL), per_e)],
            out_specs=pl.BlockSpec((EXPERT_BLOCK, PAIR_WIDTH), lambda i, be, nu: (i, 0)),
            scratch_shapes=[pltpu.VMEM((D_MODEL, 2 * D_FF), BF16), pltpu.VMEM((D_FF, D_MODEL), BF16)]),
        out_shape=jax.ShapeDtypeStruct((rows, PAIR_WIDTH), U32),
        compiler_params=_params(("arbitrary",)),
        name="expert",
    )(blk_e, nb_used, xs, w1, b1.reshape(N_EXPERTS, 1, 2 * D_FF), w2, b2.reshape(N_EXPERTS, 1, D_MODEL))


def _routing_tables(cnt, nb):
    cnt = cnt.astype(I32)
    n_al = (cnt + SEG_ALIGN - 1) // SEG_ALIGN * SEG_ALIGN
    seg_off = jnp.cumsum(n_al, axis=1) - n_al
    rel = jnp.cumsum(n_al, axis=0) - n_al
    tot = jnp.sum(n_al, axis=0)
    blocks_e = (tot + EXPERT_BLOCK - 1) // EXPERT_BLOCK
    blk_end = jnp.cumsum(blocks_e)
    e_start = (blk_end - blocks_e) * EXPERT_BLOCK
    dst = e_start[None, :] + rel
    blk_e = jnp.minimum(jnp.sum(blk_end[None, :] <= jnp.arange(nb, dtype=I32)[:, None], axis=1),
                        N_EXPERTS - 1).astype(I32)
    nb_used = blk_end[-1:].astype(I32)
    tails = jnp.concatenate([e_start + tot, blocks_e * EXPERT_BLOCK - tot, nb_used,
                             jnp.full((1,), nb, I32)]).astype(I32)
    counts = jnp.concatenate([n_al.reshape(-1), jnp.sum(n_al, axis=1)])
    return (counts, seg_off.reshape(-1).astype(I32), dst.reshape(-1).astype(I32), blk_e, nb_used, tails)


def kernel(x, c, ctx, c_ctx, w_ada, b_ada, norm1_w, w_in, gate_b, conv_w, conv_b, w_pool, pool_scale,
           hnorm_w, w_bp, w_bm, w_out, norm2_w, w_router, b_router, w1, b1, w2, b2, final_norm_w):
    assert w_ada.shape[0] == 1, "single-layer kernel"
    b, l, d = x.shape
    lc = ctx.shape[1]
    assert d == D_MODEL and l % ROUTE_TILE == 0 and l % GRID_W == 0 and lc % CHUNK == 0
    t = b * l

    rows = (b + 1 + 7) // 8 * 8
    cc = jnp.zeros((rows, d), F32).at[:b].set(c).at[b].set(c_ctx)
    mod = _ada(cc, w_ada[0], b_ada[0])
    sh1, s1, g1, sh2, s2, g2 = [mod[:b, i * d:(i + 1) * d].reshape(b, 1, d) for i in range(6)]
    csh1, cs1 = [jnp.broadcast_to(mod[b, i * d:(i + 1) * d].reshape(1, 1, d), (b, 1, d)) for i in range(2)]

    w_in0 = w_in[0]
    off_gate = POOL_WIDTH + 4 * M_WIDTH
    wpin = w_in0[:, :POOL_WIDTH].astype(BF16)
    wqkvo = w_in0[:, POOL_WIDTH:off_gate].astype(BF16)
    order = [4 * (2 * dirn + is_f) + HEADS_PER_STEP * pair + j
             for pair in range(M_HEADS // HEADS_PER_STEP) for is_f in range(2) for dirn in range(2)
             for j in range(HEADS_PER_STEP)]
    wg_hi, wg_lo = _split_bf16(w_in0[:, off_gate:off_gate + N_GATE][:, order])
    wgate = jnp.pad(jnp.concatenate([wg_hi, wg_lo], axis=1), ((0, 0), (0, LANES - 2 * N_GATE)))
    wmerge = w_in0[:, off_gate + N_GATE:].astype(BF16)
    gbias = jnp.pad(gate_b[0][jnp.asarray(order)], (0, LANES - N_GATE)).reshape(1, LANES)
    nw1 = norm1_w[0].reshape(1, d)
    proj_consts = (nw1, gbias, wqkvo, wgate, conv_w[0], conv_b[0])

    qt, k, vt, og, gate_rows = _inproj(x, sh1, s1, *proj_consts)
    _, k_c, vt_c, _, gate_rows_c = _inproj(ctx, csh1, cs1, *proj_consts)
    hn = _mlstm(qt, k, vt, gate_rows, k_c, vt_c, gate_rows_c)
    x2, xn2, slot, slot_wts, cnt = _merge_route(
        x, hn, og, sh1, s1, g1, sh2, s2, nw1, wpin, wmerge, w_pool[0].astype(BF16),
        pool_scale[0].reshape(1, POOL_WIDTH), w_bp[0].astype(BF16), hnorm_w[0], w_bm[0].astype(BF16),
        w_out[0].astype(BF16), norm2_w[0].reshape(1, d), w_router[0], b_router[0])

    nt = t // ROUTE_TILE
    nb = (t * TOP_K + nt * N_EXPERTS * (SEG_ALIGN - 1)) // EXPERT_BLOCK + N_EXPERTS
    n_flat, off_flat, dst_flat, blk_e, nb_used, tails = _routing_tables(cnt[:, :, 0], nb)
    xs = _dispatch(xn2.reshape(t, d), slot, n_flat, off_flat, dst_flat, tails, nb * EXPERT_BLOCK)
    out_rows = _expert(xs, blk_e, nb_used, w1[0], b1[0], w2[0], b2[0])
    y = _combine(out_rows, slot_wts, x2.reshape(t, d), g2, final_norm_w, n_flat, off_flat, dst_flat,
                 l // ROUTE_TILE)
    return y.reshape(b, l, d)
```

```python
import numpy as np
import jax
import jax.numpy as jnp
from jax import lax
from jax.experimental import pallas as pl
from jax.experimental.pallas import tpu as pltpu

F32 = jnp.float32
BF16 = jnp.bfloat16
I32 = jnp.int32
U32 = jnp.uint32
HIGHEST = lax.Precision.HIGHEST

D_MODEL = 1024
EPS = 1e-6
GRID_W = 64
POOL_WINDOWS = (2, 4, 8, 16)
POOL_WIDTH = 512
POOL_GC = 128
M_HEADS = 4
M_HEAD_DIM = 128
M_WIDTH = 512
CHUNK = 128
N_GATE = 16
N_EXPERTS = 32
TOP_K = 4
D_FF = 1024
SWIGLU_LIMIT = 7.0
SWIGLU_ALPHA = 1.702

LANES = 128
PROJ_TILE = 512
POOL_SLAB = 256
ROUTE_TILE = 512
BF16_ROWS = 16
SEG_ALIGN = 8
SEG_PAD = TOP_K * ROUTE_TILE + N_EXPERTS * SEG_ALIGN
PAIR_WIDTH = D_MODEL // 2
EXPERT_BLOCK = 1024
VMEM_LIMIT = 56 * 1024 * 1024


def _dot(a, b, precision=None):
    return jnp.dot(a, b, preferred_element_type=F32, precision=precision)


def _params(semantics):
    return pltpu.CompilerParams(dimension_semantics=semantics, vmem_limit_bytes=VMEM_LIMIT)


def _full(shape):
    nd = len(shape)
    return pl.BlockSpec(shape, lambda *_: (0,) * nd, pipeline_mode=pl.Buffered(1))


def _pack_pairs(v):
    half = v.shape[1] // 2
    lo = lax.bitcast_convert_type(v[:, :half], U32) >> 16
    hi = lax.bitcast_convert_type(v[:, half:], U32) & jnp.uint32(0xFFFF0000)
    return lo | hi


def _unpack_pairs(u):
    lo = lax.bitcast_convert_type(u << 16, F32)
    hi = lax.bitcast_convert_type(u & jnp.uint32(0xFFFF0000), F32)
    return jnp.concatenate([lo, hi], axis=1).astype(BF16)


def _split_bf16(a):
    hi = a.astype(BF16)
    return hi, (a - hi.astype(F32)).astype(BF16)


def _ada_kernel(c_ref, w_ref, b_ref, o_ref):
    c = c_ref[...]
    s = c * jax.nn.sigmoid(c)
    o_ref[...] = _dot(s, w_ref[...], HIGHEST) + b_ref[...]


def _ada(cc, w_ada, b_ada):
    rows, d = cc.shape
    n = w_ada.shape[1]
    tn = 1024
    return pl.pallas_call(
        _ada_kernel,
        grid=(n // tn,),
        in_specs=[pl.BlockSpec((rows, d), lambda j: (0, 0)),
                  pl.BlockSpec((d, tn), lambda j: (0, j)),
                  pl.BlockSpec((1, tn), lambda j: (0, j))],
        out_specs=pl.BlockSpec((rows, tn), lambda j: (0, j)),
        out_shape=jax.ShapeDtypeStruct((rows, n), F32),
        compiler_params=_params(("arbitrary",)),
        name="ada",
    )(cc, w_ada, b_ada.reshape(1, n))


def _pool_constants(tm):
    pos = np.arange(tm) % GRID_W
    row = np.arange(tm) // GRID_W
    slab = min(tm, POOL_SLAB)
    pm = np.zeros((len(POOL_WINDOWS), slab, slab), np.float32)
    inv = np.zeros((tm, POOL_WIDTH), np.float32)
    for g, win in enumerate(POOL_WINDOWS):
        lo = np.clip(pos - win // 2, 0, GRID_W)
        hi = np.clip(pos + win // 2, 0, GRID_W)
        same = row[:, None] == row[None, :]
        full = (same & (pos[None, :] >= lo[:, None]) & (pos[None, :] < hi[:, None])).astype(np.float32)
        pm[g] = full[:slab, :slab]
        inv[:, g * POOL_GC:(g + 1) * POOL_GC] = (1.0 / (hi - lo).astype(np.float32))[:, None]
    t = np.arange(tm)
    same_chunk = (t[:, None] // CHUNK) == (t[None, :] // CHUNK)
    tl = (same_chunk & (t[None, :] <= t[:, None])).astype(np.float32)
    tu = (same_chunk & (t[None, :] >= t[:, None])).astype(np.float32)
    return pm, inv, tl, tu


def _modulated_norm(x, nw, scale, shift):
    xn = x * lax.rsqrt(jnp.mean(x * x, axis=-1, keepdims=True) + EPS) * nw
    return xn * (1.0 + scale) + shift


F32_ROWS = 8


def _inproj_kernel(x_ref, xprev_ref, xnext_ref, sh_ref, sc_ref, nw_ref, gbias_ref, wqkvo_ref, wgate_ref,
                   tl_ref, tu_ref, cw_ref, cb_ref, qt_ref, k_ref, vt_ref, og_ref, row_ref):
    i = pl.program_id(1)
    last = pl.num_programs(1) - 1
    xm = _modulated_norm(x_ref[0], nw_ref[...], sc_ref[0], sh_ref[0])
    xb = xm.astype(BF16)
    tm = xm.shape[0]

    xlo = (xm - xb.astype(F32)).astype(BF16)
    wg = wgate_ref[...]
    r_hi = _dot(xb, wg)
    g = r_hi + pltpu.roll(r_hi, LANES - N_GATE, axis=1) + _dot(xlo, wg) + gbias_ref[...]
    lf = jnp.minimum(g, 0.0) - jnp.log1p(jnp.exp(-jnp.abs(g)))
    lane = lax.broadcasted_iota(I32, g.shape, 1)
    lf_hi = lf.astype(BF16).astype(F32)
    lf2 = jnp.where(lane < N_GATE, lf_hi, pltpu.roll(lf - lf_hi, N_GATE, axis=1)).astype(BF16)
    r_pre = _dot(tl_ref[...], lf2)
    r_suf = _dot(tu_ref[...], lf2)
    b_pre = r_pre + pltpu.roll(r_pre, LANES - N_GATE, axis=1)
    b_suf = r_suf + pltpu.roll(r_suf, LANES - N_GATE, axis=1)
    kind = (lane & 7) >> 1
    col = jnp.where(kind == 2, b_pre, jnp.where(kind == 3, b_suf, g))
    rows = col.T[:N_GATE]
    for ci in range(tm // CHUNK):
        row_ref[0, ci] = rows[:, ci * CHUNK:(ci + 1) * CHUNK]

    wqk = wqkvo_ref[:, 0:2 * M_WIDTH]
    qk = _dot(xb, wqk)
    halo = jnp.concatenate([xprev_ref[0], xnext_ref[0]], axis=0)
    qk_halo = _dot(_modulated_norm(halo, nw_ref[...], sc_ref[0], sh_ref[0]).astype(BF16), wqk)
    prev_row = jnp.where(i > 0, qk_halo[F32_ROWS - 1:F32_ROWS], 0.0)
    next_row = jnp.where(i < last, qk_halo[F32_ROWS:F32_ROWS + 1], 0.0)
    rio = lax.broadcasted_iota(I32, qk.shape, 0)
    qk_prev = jnp.where(rio == 0, prev_row, pltpu.roll(qk, 1, axis=0))
    qk_next = jnp.where(rio == tm - 1, next_row, pltpu.roll(qk, tm - 1, axis=0))
    cw = cw_ref[...]
    acc = cb_ref[...] + qk_prev * cw[0:1] + qk * cw[1:2] + qk_next * cw[2:3]
    y = acc * jax.nn.sigmoid(acc)
    q = y[:, :M_WIDTH] * (M_HEAD_DIM ** -0.5)
    for ci in range(tm // CHUNK):
        qt_ref[0, ci] = q[ci * CHUNK:(ci + 1) * CHUNK].T.astype(BF16)
        k_ref[0, ci] = y[ci * CHUNK:(ci + 1) * CHUNK, M_WIDTH:].astype(BF16)

    v = _dot(xb, wqkvo_ref[:, 2 * M_WIDTH:3 * M_WIDTH])
    for ci in range(v.shape[0] // CHUNK):
        vt_ref[0, ci] = v[ci * CHUNK:(ci + 1) * CHUNK].T.astype(BF16)
    og_ref[0] = jax.nn.sigmoid(_dot(xb, wqkvo_ref[:, 3 * M_WIDTH:4 * M_WIDTH])).astype(BF16)


def _chunk_t_out(b, l, tm):
    return (pl.BlockSpec((1, tm // CHUNK, M_WIDTH, CHUNK), lambda bi, i: (bi, i, 0, 0)),
            jax.ShapeDtypeStruct((b, l // CHUNK, M_WIDTH, CHUNK), BF16))


def _inproj(x, shift, scale, nw, gbias, wqkvo, wgate, conv_w, conv_b):
    b, l, d = x.shape
    tm = min(PROJ_TILE, l)
    per = tm // F32_ROWS
    nblk = l // F32_ROWS
    _, _, tl, tu = _pool_constants(tm)
    tok = lambda width: pl.BlockSpec((1, tm, width), lambda bi, i: (bi, i, 0))
    vec = pl.BlockSpec((1, 1, d), lambda bi, i: (bi, 0, 0))
    halo_before = pl.BlockSpec((1, F32_ROWS, d), lambda bi, i: (bi, jnp.maximum(i * per - 1, 0), 0))
    halo_after = pl.BlockSpec((1, F32_ROWS, d), lambda bi, i: (bi, jnp.minimum((i + 1) * per, nblk - 1), 0))
    consts = [nw, gbias, wqkvo, wgate, jnp.asarray(tl, BF16), jnp.asarray(tu, BF16), conv_w,
              conv_b.reshape(1, 2 * M_WIDTH)]
    tok_out = lambda width: (tok(width), jax.ShapeDtypeStruct((b, l, width), BF16))
    gate_out = (pl.BlockSpec((1, tm // CHUNK, N_GATE, CHUNK), lambda bi, i: (bi, i, 0, 0)),
                jax.ShapeDtypeStruct((b, l // CHUNK, N_GATE, CHUNK), F32))
    k_out = (pl.BlockSpec((1, tm // CHUNK, CHUNK, M_WIDTH), lambda bi, i: (bi, i, 0, 0)),
             jax.ShapeDtypeStruct((b, l // CHUNK, CHUNK, M_WIDTH), BF16))
    outs = [_chunk_t_out(b, l, tm), k_out, _chunk_t_out(b, l, tm), tok_out(M_WIDTH), gate_out]
    return pl.pallas_call(
        _inproj_kernel,
        grid=(b, l // tm),
        in_specs=[tok(d), halo_before, halo_after, vec, vec] + [_full(c.shape) for c in consts],
        out_specs=[spec for spec, _ in outs],
        out_shape=[shape for _, shape in outs],
        compiler_params=_params(("arbitrary", "arbitrary")),
        name="inproj",
    )(x, x, x, shift, scale, *consts)


HEADS_PER_STEP = 2
N_CHAINS = 2 * HEADS_PER_STEP
GATE_ROWS = 2 * N_CHAINS


STATE_ROWS = M_HEAD_DIM + BF16_ROWS


def _lane_pair(rows2):
    return jnp.concatenate([rows2[0:1], rows2[1:2]], axis=1)


def _block_diag(a, b):
    za, zb = jnp.zeros((a.shape[0], b.shape[1]), a.dtype), jnp.zeros((b.shape[0], a.shape[1]), a.dtype)
    return jnp.concatenate([jnp.concatenate([a, za], axis=1), jnp.concatenate([zb, b], axis=1)], axis=0)


def _side_by_side(stacked):
    d = stacked.shape[0] // 2
    return jnp.concatenate([stacked[:d], stacked[d:]], axis=1)


def _state_step(k2, vt2, i_row, b_row, btot, m_old, m_new, state):
    t = k2.shape[0]
    a = jnp.exp(btot - b_row + i_row - m_new)
    cd = jnp.exp(btot + m_old - m_new)
    ones = (lax.broadcasted_iota(I32, (STATE_ROWS - M_HEAD_DIM, 2 * t), 0) == 0).astype(F32)
    vta = (jnp.concatenate([_side_by_side(vt2).astype(F32), ones], axis=0) * a).astype(BF16)
    return cd * state + _dot(vta, _block_diag(k2[:, :M_HEAD_DIM], k2[:, M_HEAD_DIM:]))


def _chunk_out_t(qt2, k2, vt2, i_row, b_row, mask_t, m_old, state):
    t = k2.shape[0]
    g = i_row - b_row
    g_src = jnp.concatenate([jnp.broadcast_to(g[:, :t], (t, t)).T, jnp.broadcast_to(g[:, t:], (t, t)).T],
                            axis=1)
    logw = jnp.where(mask_t, b_row + g_src, -jnp.inf)
    m_inter = b_row + m_old
    m_t = jnp.maximum(jnp.max(logw, axis=0, keepdims=True), m_inter)
    q_bd = _block_diag(qt2[:M_HEAD_DIM], qt2[M_HEAD_DIM:])
    both = _dot(jnp.concatenate([k2, state.astype(BF16)], axis=0), q_bd)
    s = both[:t] * jnp.exp(logw - m_t)
    inter = both[t:]
    decay = jnp.exp(m_inter - m_t)
    s16 = s.astype(BF16)
    num = _dot(_side_by_side(vt2), _block_diag(s16[:, :t], s16[:, t:])) + decay * inter[:M_HEAD_DIM]
    den = jnp.sum(s, axis=0, keepdims=True) + decay * inter[M_HEAD_DIM:M_HEAD_DIM + 1]
    return num * (1.0 / jnp.maximum(jnp.abs(den), jnp.exp(-m_t)))


def _mlstm_kernel(qt_ref, k_ref, vt_ref, row_ref, kc_ref, vtc_ref, rowc_ref, out_ref,
                  h_ref, bt_ref, ma_ref, mold_ref, mnew_ref):
    nc, ncc, t = k_ref.shape[1], kc_ref.shape[1], CHUNK
    src = lax.broadcasted_iota(I32, (t, t), 0)
    tgt = lax.broadcasted_iota(I32, (t, t), 1)
    causal_t = jnp.concatenate([src <= tgt] * HEADS_PER_STEP, axis=1)
    masks_t = (causal_t, jnp.concatenate([src >= tgt] * HEADS_PER_STEP, axis=1))
    grow = pl.ds(pl.multiple_of(pl.program_id(1) * GATE_ROWS, GATE_ROWS), GATE_ROWS)
    head = [slice(j * M_HEAD_DIM, (j + 1) * M_HEAD_DIM) for j in range(HEADS_PER_STEP)]
    dir_rows = [slice(HEADS_PER_STEP * dirn, HEADS_PER_STEP * (dirn + 1)) for dirn in range(2)]

    def gate_pairs(rows, dirn):
        return (_lane_pair(rows[dir_rows[dirn]]),
                _lane_pair(rows[N_CHAINS + HEADS_PER_STEP * dirn:N_CHAINS + HEADS_PER_STEP * (dirn + 1)]))

    def bcast_pair(col2):
        return _lane_pair(jnp.broadcast_to(col2, (HEADS_PER_STEP, t)))

    state = [jnp.zeros((STATE_ROWS, HEADS_PER_STEP * M_HEAD_DIM), F32)] * 2
    m_st = [jnp.zeros((HEADS_PER_STEP, 1), F32)] * 2
    for dirn in range(2):
        for ci in (range(ncc) if dirn == 0 else reversed(range(ncc))):
            rows = rowc_ref[0, ci, grow, :]
            gi2 = rows[dir_rows[dirn]]
            gb2 = rows[N_CHAINS + HEADS_PER_STEP * dirn:N_CHAINS + HEADS_PER_STEP * (dirn + 1)]
            btot = gb2[:, t - 1:t] if dirn == 0 else gb2[:, 0:1]
            m_new = jnp.maximum(btot + m_st[dirn], jnp.max(btot - gb2 + gi2, axis=1, keepdims=True))
            i_row, b_row = gate_pairs(rows, dirn)
            state[dirn] = _state_step(kc_ref[0, ci], vtc_ref[0, ci], i_row, b_row, bcast_pair(btot),
                                      bcast_pair(m_st[dirn]), bcast_pair(m_new), state[dirn])
            m_st[dirn] = m_new

    gates = row_ref[0, :, grow, :]
    gi, gb = gates[:, :N_CHAINS], gates[:, N_CHAINS:]
    is_fwd = lax.broadcasted_iota(I32, gb.shape, 1) < HEADS_PER_STEP
    bt = jnp.where(is_fwd, jnp.broadcast_to(gb[:, :, t - 1:t], gb.shape),
                   jnp.broadcast_to(gb[:, :, 0:1], gb.shape))
    bt_ref[...] = bt
    ma_ref[...] = jnp.broadcast_to(jnp.max(bt - gb + gi, axis=2, keepdims=True), gb.shape)

    def m_scan(s, carry):
        m_f, m_b = carry
        cf, cb = s, nc - 1 - s
        mold_ref[cf, 0:HEADS_PER_STEP, :] = m_f[0:HEADS_PER_STEP]
        m_f = jnp.maximum(bt_ref[cf] + m_f, ma_ref[cf])
        mnew_ref[cf, 0:HEADS_PER_STEP, :] = m_f[0:HEADS_PER_STEP]
        mold_ref[cb, HEADS_PER_STEP:, :] = m_b[HEADS_PER_STEP:]
        m_b = jnp.maximum(bt_ref[cb] + m_b, ma_ref[cb])
        mnew_ref[cb, HEADS_PER_STEP:, :] = m_b[HEADS_PER_STEP:]
        return m_f, m_b

    m0 = jnp.concatenate([jnp.broadcast_to(m, (HEADS_PER_STEP, t)) for m in m_st], axis=0)
    lax.fori_loop(0, nc, m_scan, (m0, m0))

    def advance(ci, dirn, st):
        i_row, b_row = gate_pairs(row_ref[0, ci, grow, :], dirn)
        m_old, m_new, btot = [_lane_pair(tab[ci][dir_rows[dirn]]) for tab in (mold_ref, mnew_ref, bt_ref)]
        qt2, k2, vt2 = qt_ref[0, ci], k_ref[0, ci], vt_ref[0, ci]
        h_ref[dirn, ci] = _chunk_out_t(qt2, k2, vt2, i_row, b_row, masks_t[dirn], m_old, st)
        return _state_step(k2, vt2, i_row, b_row, btot, m_old, m_new, st)

    def body(s, states):
        return advance(s, 0, states[0]), advance(nc - 1 - s, 1, states[1])

    lax.fori_loop(0, nc, body, tuple(state), unroll=8)

    def norm_body(ci, _):
        ht2 = h_ref[0, ci] + h_ref[1, ci]
        for j in range(HEADS_PER_STEP):
            ht = ht2[:, j * t:(j + 1) * t]
            hn = ht * lax.rsqrt(jnp.mean(ht * ht, axis=0, keepdims=True) + EPS)
            out_ref[0, ci, :, head[j]] = hn.T.astype(BF16)
        return 0

    lax.fori_loop(0, nc, norm_body, 0, unroll=4)


def _mlstm(qt, k, vt, gate_rows, kc, vtc, gate_rows_c):
    b, nc = k.shape[:2]
    ncc = kc.shape[1]
    pairs = M_HEADS // HEADS_PER_STEP
    width = HEADS_PER_STEP * M_HEAD_DIM
    once = pl.Buffered(1)
    seq = lambda n: pl.BlockSpec((1, n, CHUNK, width), lambda bi, h: (bi, 0, 0, h))
    seq_t = lambda n: pl.BlockSpec((1, n, width, CHUNK), lambda bi, h: (bi, 0, h, 0))
    rows = lambda n: pl.BlockSpec((1, n, pairs * GATE_ROWS, CHUNK), lambda bi, h: (bi, 0, 0, 0),
                                  pipeline_mode=once)
    table = pltpu.VMEM((nc, N_CHAINS, CHUNK), F32)
    return pl.pallas_call(
        _mlstm_kernel,
        grid=(b, pairs),
        in_specs=[seq_t(nc), seq(nc), seq_t(nc), rows(nc), seq(ncc), seq_t(ncc), rows(ncc)],
        out_specs=pl.BlockSpec((1, nc, CHUNK, width), lambda bi, h: (bi, 0, 0, h)),
        out_shape=jax.ShapeDtypeStruct((b, nc, CHUNK, M_WIDTH), BF16),
        scratch_shapes=[pltpu.VMEM((2, nc, M_HEAD_DIM, HEADS_PER_STEP * CHUNK), F32),
                        table, table, table, table],
        compiler_params=_params(("arbitrary", "arbitrary")),
        name="mlstm",
    )(qt, k, vt, gate_rows, kc, vtc, gate_rows_c)


def _merge_kernel(x_ref, hn_ref, og_ref, sh_ref, sc_ref, g1_ref, sh2_ref, sc2_ref, nw_ref, wpin_ref,
                  wmerge_ref, pmat_ref, invc_ref, wpool_ref, pscale_ref, wbp_ref, hw_ref, wbm_ref, wout_ref,
                  nw2_ref, wr_ref, br_ref, ustrict_ref, lstrict_ref,
                  o_ref, xn_ref, slot_ref, sw_ref, cnt_ref):
    x = x_ref[0]
    xb = _modulated_norm(x, nw_ref[...], sc_ref[0], sh_ref[0]).astype(BF16)

    u = _dot(xb, wpin_ref[...])
    u_hi, u_lo = _split_bf16(u)
    invc = invc_ref[...]
    slab = pmat_ref.shape[1]
    ya = []
    for gi in range(len(POOL_WINDOWS)):
        cols = slice(gi * POOL_GC, (gi + 1) * POOL_GC)
        both = jnp.concatenate([u_hi[:, cols], u_lo[:, cols]], axis=1)
        sums = jnp.concatenate([_dot(pmat_ref[gi], both[s:s + slab]) for s in range(0, u.shape[0], slab)],
                               axis=0)
        pooled = (sums[:, :POOL_GC] + sums[:, POOL_GC:]) * invc[:, cols] - u[:, cols]
        ya.append(_dot(pooled.astype(BF16), wpool_ref[gi]))
    ya = jnp.concatenate(ya, axis=1) * pscale_ref[...]
    y = jax.nn.sigmoid(_dot(xb, wmerge_ref[:, 0:D_MODEL])) * _dot(ya.astype(BF16), wbp_ref[...])

    hn = jnp.concatenate([hn_ref[0, ci] for ci in range(hn_ref.shape[1])], axis=0)
    yb = hn.astype(F32) * hw_ref[...] * og_ref[0].astype(F32)
    y = y + jax.nn.sigmoid(_dot(xb, wmerge_ref[:, D_MODEL:2 * D_MODEL])) * _dot(yb.astype(BF16), wbm_ref[...])
    x2 = x + g1_ref[0] * _dot(y.astype(BF16), wout_ref[...])
    o_ref[0] = x2

    _route_tile(x2, sh2_ref, sc2_ref, nw2_ref, wr_ref, br_ref, ustrict_ref, lstrict_ref,
                xn_ref, slot_ref, sw_ref, cnt_ref)


def _merge_route(x, hn, og, shift, scale, g1, shift2, scale2, nw, wpin, wmerge, wpool, pscale, wbp, hnorm_w,
                 wbm, wout, nw2, w_router, b_router):
    b, l, d = x.shape
    tm = ROUTE_TILE
    nl = l // tm
    nt = b * nl
    pm, inv, _, _ = _pool_constants(tm)
    tok = lambda width: pl.BlockSpec((1, tm, width), lambda bi, i: (bi, i, 0))
    vec = pl.BlockSpec((1, 1, d), lambda bi, i: (bi, 0, 0))
    tile = lambda r, c: pl.BlockSpec((1, r, c), lambda bi, i: (bi * nl + i, 0, 0))
    consts = [nw, wpin, wmerge, jnp.asarray(pm, BF16), jnp.asarray(inv), wpool, pscale, wbp,
              hnorm_w.reshape(1, M_WIDTH), wbm, wout, nw2] + _router_consts(w_router, b_router)
    return pl.pallas_call(
        _merge_kernel,
        grid=(b, nl),
        in_specs=[tok(d), pl.BlockSpec((1, tm // CHUNK, CHUNK, M_WIDTH), lambda bi, i: (bi, i, 0, 0)),
                  tok(M_WIDTH), vec, vec, vec, vec, vec]
        + [_full(c.shape) for c in consts],
        out_specs=[tok(d), tok(d), tile(8, tm), pl.BlockSpec((tm, 2 * TOP_K), lambda bi, i: (bi * nl + i, 0)),
                   tile(N_EXPERTS, LANES)],
        out_shape=[jax.ShapeDtypeStruct((b, l, d), F32),
                   jax.ShapeDtypeStruct((b, l, d), BF16),
                   jax.ShapeDtypeStruct((nt, 8, tm), I32),
                   jax.ShapeDtypeStruct((nt * tm, 2 * TOP_K), F32),
                   jax.ShapeDtypeStruct((nt, N_EXPERTS, LANES), F32)],
        compiler_params=_params(("arbitrary", "arbitrary")),
        name="merge_route",
    )(x, hn, og, shift, scale, g1, shift2, scale2, *consts)


def _route_tile(x, sh_ref, sc_ref, nw_ref, wr_ref, br_ref, ustrict_ref, lstrict_ref,
                xn_ref, slot_ref, sw_ref, cnt_ref):
    tm = x.shape[0]
    xm = _modulated_norm(x, nw_ref[...], sc_ref[0], sh_ref[0])
    xn_ref[0] = xm.astype(BF16)

    x_hi, x_lo = _split_bf16(xm)
    r_hi = _dot(x_hi, wr_ref[...])
    r = r_hi + pltpu.roll(r_hi, LANES - N_EXPERTS, axis=1) + _dot(x_lo, wr_ref[...])
    logits = r.T[:N_EXPERTS] + br_ref[...]
    eio = lax.broadcasted_iota(I32, logits.shape, 0).astype(F32)
    rest = logits
    onehots, vals = [], []
    for _ in range(TOP_K):
        mx = jnp.max(rest, axis=0, keepdims=True)
        idx = jnp.min(jnp.where(rest == mx, eio, float(N_EXPERTS)), axis=0, keepdims=True)
        oh = eio == idx
        onehots.append(oh)
        vals.append(mx)
        rest = jnp.where(oh, -jnp.inf, rest)
    exps = [jnp.exp(vk - vals[0]) for vk in vals]
    denom = exps[0] + exps[1] + exps[2] + exps[3]

    oh_all = jnp.zeros(logits.shape, F32)
    for oh in onehots:
        oh_all = oh_all + oh.astype(F32)
    cnt = jnp.sum(oh_all, axis=1, keepdims=True)
    n_al = jnp.ceil(cnt * (1.0 / SEG_ALIGN)) * SEG_ALIGN
    seg_off = _dot(lstrict_ref[...], jnp.broadcast_to(n_al, (N_EXPERTS, LANES)).astype(BF16))[:, 0:1]
    rank = _dot(oh_all.astype(BF16), ustrict_ref[...])
    base = seg_off + rank
    slots = [jnp.sum(jnp.where(onehots[kk], base, 0.0), axis=0, keepdims=True) for kk in range(TOP_K)]
    for kk in range(TOP_K):
        slot_ref[0, kk:kk + 1, :] = slots[kk].astype(I32)
    slot_ref[0, TOP_K:, :] = jnp.full((8 - TOP_K, tm), -1, I32)
    per_token = jnp.concatenate(slots + [e / denom for e in exps] + [jnp.zeros((LANES - 2 * TOP_K, tm), F32)],
                                axis=0)
    sw_ref[...] = per_token.T[:, :2 * TOP_K]
    cnt_ref[0] = jnp.broadcast_to(cnt, (N_EXPERTS, LANES))


def _router_consts(w_router, b_router):
    s = np.arange(ROUTE_TILE)
    ustrict = jnp.asarray((s[:, None] < s[None, :]).astype(np.float32), BF16)
    e = np.arange(N_EXPERTS)
    lstrict = jnp.asarray((e[None, :] < e[:, None]).astype(np.float32), BF16)
    w_cat = jnp.pad(jnp.concatenate(_split_bf16(w_router), axis=1), ((0, 0), (0, LANES - 2 * N_EXPERTS)))
    return [w_cat, b_router.reshape(N_EXPERTS, 1), ustrict, lstrict]


def _segment_starts(tile, n_s, off_s, dst_s, make_copy):
    def body(e, _):
        idx = tile * N_EXPERTS + e
        n = n_s[idx]

        @pl.when(n > 0)
        def _():
            make_copy(pl.multiple_of(off_s[idx], SEG_ALIGN), pl.multiple_of(dst_s[idx], SEG_ALIGN),
                      pl.multiple_of(n, SEG_ALIGN)).start()

        return 0

    lax.fori_loop(0, N_EXPERTS, body, 0, unroll=8)


def _segment_wait(tile, n_s, make_copy):
    total = n_s[pl.num_programs(0) * N_EXPERTS + tile]
    make_copy(0, 0, pl.multiple_of(total, SEG_ALIGN)).wait()


def _zero_fill(tail_s, xs_hbm, zero_ref, sem, action):
    def make_copy(dst, sz):
        return pltpu.make_async_copy(zero_ref.at[pl.ds(0, sz)], xs_hbm.at[pl.ds(dst, sz)], sem)

    def tail_body(e, _):
        n = tail_s[N_EXPERTS + e]

        @pl.when(n > 0)
        def _():
            action(make_copy(pl.multiple_of(tail_s[e], SEG_ALIGN), pl.multiple_of(n, SEG_ALIGN)))

        return 0

    lax.fori_loop(0, N_EXPERTS, tail_body, 0)

    def block_body(blk, _):
        action(make_copy(pl.multiple_of(blk * EXPERT_BLOCK, EXPERT_BLOCK), EXPERT_BLOCK))
        return 0

    lax.fori_loop(tail_s[2 * N_EXPERTS], tail_s[2 * N_EXPERTS + 1], block_body, 0)


ROUTE_CHUNK = 768
assert SEG_PAD % ROUTE_CHUNK == 0


def _dispatch_kernel(n_s, off_s, dst_s, tail_s, xn_ref, slot_ref, xs_hbm, g_ref, zero_ref, sems):
    tile = pl.program_id(0)
    last = pl.num_programs(0) - 1
    cur = tile % 2

    def copier(of_tile):
        par = of_tile % 2
        return lambda src, dst, n: pltpu.make_async_copy(
            g_ref.at[par, pl.ds(src, n)], xs_hbm.at[pl.ds(dst, n)], sems.at[par])

    @pl.when(tile >= 2)
    def _():
        _segment_wait(tile - 2, n_s, copier(tile - 2))

    x = xn_ref[...]
    tm = x.shape[0]
    slot = slot_ref[0].astype(jnp.int16)
    rio = lax.broadcasted_iota(I32, (ROUTE_CHUNK, tm), 0).astype(jnp.int16)
    one = jnp.ones((), BF16)
    for ci in range(SEG_PAD // ROUTE_CHUNK):
        rel = slot - ci * ROUTE_CHUNK
        sel = jnp.zeros((ROUTE_CHUNK, tm), BF16)
        for kk in range(TOP_K):
            sel = jnp.where(rio == rel[kk:kk + 1, :], one, sel)
        g_ref[cur, ci * ROUTE_CHUNK:(ci + 1) * ROUTE_CHUNK, :] = _pack_pairs(_dot(sel, x))

    _segment_starts(tile, n_s, off_s, dst_s, copier(tile))

    @pl.when(tile == last)
    def _():
        @pl.when(tile >= 1)
        def _():
            _segment_wait(tile - 1, n_s, copier(tile - 1))

        _segment_wait(tile, n_s, copier(tile))
        zero_ref[...] = jnp.zeros(zero_ref.shape, U32)
        _zero_fill(tail_s, xs_hbm, zero_ref, sems.at[2], lambda cp: cp.start())
        _zero_fill(tail_s, xs_hbm, zero_ref, sems.at[2], lambda cp: cp.wait())


def _dispatch(xn2, slot, n_flat, off_flat, dst_flat, tail_flat, rows_total):
    t, d = xn2.shape
    tm = ROUTE_TILE
    nt = t // tm
    return pl.pallas_call(
        _dispatch_kernel,
        grid_spec=pltpu.PrefetchScalarGridSpec(
            num_scalar_prefetch=4,
            grid=(nt,),
            in_specs=[pl.BlockSpec((tm, d), lambda i, *_: (i, 0)),
                      pl.BlockSpec((1, 8, tm), lambda i, *_: (i, 0, 0))],
            out_specs=pl.BlockSpec(memory_space=pl.ANY),
            scratch_shapes=[pltpu.VMEM((2, SEG_PAD, PAIR_WIDTH), U32),
                            pltpu.VMEM((EXPERT_BLOCK, PAIR_WIDTH), U32),
                            pltpu.SemaphoreType.DMA((3,))]),
        out_shape=jax.ShapeDtypeStruct((rows_total, PAIR_WIDTH), U32),
        compiler_params=_params(("arbitrary",)),
        name="dispatch",
    )(n_flat, off_flat, dst_flat, tail_flat, xn2, slot)


def _combine_kernel(n_s, off_s, dst_s, out_hbm, sw_ref, x2_ref, g2_ref, fw_ref, y_ref,
                    buf_ref, sel_ref, sems):
    tile = pl.program_id(0)
    cur = tile % 2

    def copier(of_tile):
        par = of_tile % 2
        return lambda seg, src, n: pltpu.make_async_copy(
            out_hbm.at[pl.ds(src, n)], buf_ref.at[par, pl.ds(seg, n)], sems.at[par])

    @pl.when(tile == 0)
    def _():
        buf_ref[...] = jnp.zeros(buf_ref.shape, U32)
        _segment_starts(tile, n_s, off_s, dst_s, copier(tile))

    @pl.when(tile + 1 < pl.num_programs(0))
    def _():
        _segment_starts(tile + 1, n_s, off_s, dst_s, copier(tile + 1))

    _segment_wait(tile, n_s, copier(tile))

    sw = sw_ref[...]
    st = sw[:, :TOP_K].astype(I32).astype(jnp.int16)
    wt = sw[:, TOP_K:].astype(BF16)
    tm = st.shape[0]
    lio = lax.broadcasted_iota(I32, (tm, ROUTE_CHUNK), 1).astype(jnp.int16)
    for ci in range(SEG_PAD // ROUTE_CHUNK):
        rel = st - ci * ROUTE_CHUNK
        sel = jnp.zeros((tm, ROUTE_CHUNK), BF16)
        for kk in range(TOP_K):
            sel = jnp.where(lio == rel[:, kk:kk + 1], wt[:, kk:kk + 1], sel)
        sel_ref[:, ci * ROUTE_CHUNK:(ci + 1) * ROUTE_CHUNK] = sel
    x3 = x2_ref[...] + g2_ref[0] * _dot(sel_ref[...], _unpack_pairs(buf_ref[cur]))
    y_ref[...] = x3 * lax.rsqrt(jnp.mean(x3 * x3, axis=-1, keepdims=True) + EPS) * fw_ref[...]


def _combine(out_rows, slot_wts, x2_flat, g2, final_w, n_flat, off_flat, dst_flat, tiles_per_batch):
    t, d = x2_flat.shape
    tm = ROUTE_TILE
    nt = t // tm
    return pl.pallas_call(
        _combine_kernel,
        grid_spec=pltpu.PrefetchScalarGridSpec(
            num_scalar_prefetch=3,
            grid=(nt,),
            in_specs=[pl.BlockSpec(memory_space=pl.ANY),
                      pl.BlockSpec((tm, 2 * TOP_K), lambda i, *_: (i, 0)),
                      pl.BlockSpec((tm, d), lambda i, *_: (i, 0)),
                      pl.BlockSpec((1, 1, d), lambda i, *_: (i // tiles_per_batch, 0, 0)),
                      pl.BlockSpec((1, d), lambda i, *_: (0, 0))],
            out_specs=pl.BlockSpec((tm, d), lambda i, *_: (i, 0)),
            scratch_shapes=[pltpu.VMEM((2, SEG_PAD, PAIR_WIDTH), U32), pltpu.VMEM((tm, SEG_PAD), BF16),
                            pltpu.SemaphoreType.DMA((2,))]),
        out_shape=jax.ShapeDtypeStruct((t, d), F32),
        compiler_params=_params(("arbitrary",)),
        name="combine",
    )(n_flat, off_flat, dst_flat, out_rows, slot_wts, x2_flat, g2, final_w.reshape(1, d))


def _expert_kernel(blk_e, nb_used, xs_ref, w1_ref, b1_ref, w2_ref, b2_ref, o_ref, w1b_ref, w2b_ref):
    i = pl.program_id(0)
    used = i < nb_used[0]

    @pl.when(jnp.logical_not(used))
    def _():
        o_ref[...] = jnp.zeros(o_ref.shape, U32)

    @pl.when(used & ((i == 0) | (blk_e[i] != blk_e[jnp.maximum(i - 1, 0)])))
    def _():
        w1b_ref[...] = w1_ref[0].astype(BF16)
        w2b_ref[...] = w2_ref[0].astype(BF16)

    @pl.when(used)
    def _():
        gu = _dot(_unpack_pairs(xs_ref[...]), w1b_ref[...]) + b1_ref[0]
        gate = jnp.minimum(gu[:, :D_FF], SWIGLU_LIMIT)
        up = jnp.clip(gu[:, D_FF:], -SWIGLU_LIMIT, SWIGLU_LIMIT)
        act = (up + 1.0) * gate * jax.nn.sigmoid(SWIGLU_ALPHA * gate)
        o = _dot(act.astype(BF16), w2b_ref[...]) + b2_ref[0]
        o_ref[...] = _pack_pairs(o.astype(BF16).astype(F32))


def _expert(xs, blk_e, nb_used, w1, b1, w2, b2):
    rows = xs.shape[0]
    nb = rows // EXPERT_BLOCK
    row_blk = lambda i, be, nu: (jnp.maximum(jnp.minimum(i, nu[0] - 1), 0), 0)
    per_e = lambda i, be, nu: (be[i], 0, 0)
    return pl.pallas_call(
        _expert_kernel,
        grid_spec=pltpu.PrefetchScalarGridSpec(
            num_scalar_prefetch=2,
            grid=(nb,),
            in_specs=[pl.BlockSpec((EXPERT_BLOCK, PAIR_WIDTH), row_blk),
                      pl.BlockSpec((1, D_MODEL, 2 * D_FF), per_e),
                      pl.BlockSpec((1, 1, 2 * D_FF), per_e),
                      pl.BlockSpec((1, D_FF, D_MODEL), per_e),
                      pl.BlockSpec((1, 1, D_MODEL), per_e)],
            out_specs=pl.BlockSpec((EXPERT_BLOCK, PAIR_WIDTH), lambda i, be, nu: (i, 0)),
            scratch_shapes=[pltpu.VMEM((D_MODEL, 2 * D_FF), BF16), pltpu.VMEM((D_FF, D_MODEL), BF16)]),
        out_shape=jax.ShapeDtypeStruct((rows, PAIR_WIDTH), U32),
        compiler_params=_params(("arbitrary",)),
        name="expert",
    )(blk_e, nb_used, xs, w1, b1.reshape(N_EXPERTS, 1, 2 * D_FF), w2, b2.reshape(N_EXPERTS, 1, D_MODEL))


def _routing_tables(cnt, nb):
    cnt = cnt.astype(I32)
    n_al = (cnt + SEG_ALIGN - 1) // SEG_ALIGN * SEG_ALIGN
    seg_off = jnp.cumsum(n_al, axis=1) - n_al
    rel = jnp.cumsum(n_al, axis=0) - n_al
    tot = jnp.sum(n_al, axis=0)
    blocks_e = (tot + EXPERT_BLOCK - 1) // EXPERT_BLOCK
    blk_end = jnp.cumsum(blocks_e)
    e_start = (blk_end - blocks_e) * EXPERT_BLOCK
    dst = e_start[None, :] + rel
    blk_e = jnp.minimum(jnp.sum(blk_end[None, :] <= jnp.arange(nb, dtype=I32)[:, None], axis=1),
                        N_EXPERTS - 1).astype(I32)
    nb_used = blk_end[-1:].astype(I32)
    tails = jnp.concatenate([e_start + tot, blocks_e * EXPERT_BLOCK - tot, nb_used,
                             jnp.full((1,), nb, I32)]).astype(I32)
    counts = jnp.concatenate([n_al.reshape(-1), jnp.sum(n_al, axis=1)])
    return (counts, seg_off.reshape(-1).astype(I32), dst.reshape(-1).astype(I32), blk_e, nb_used, tails)


def kernel(x, c, ctx, c_ctx, w_ada, b_ada, norm1_w, w_in, gate_b, conv_w, conv_b, w_pool, pool_scale,
           hnorm_w, w_bp, w_bm, w_out, norm2_w, w_router, b_router, w1, b1, w2, b2, final_norm_w):
    assert w_ada.shape[0] == 1, "single-layer kernel"
    b, l, d = x.shape
    lc = ctx.shape[1]
    assert d == D_MODEL and l % ROUTE_TILE == 0 and l % GRID_W == 0 and lc % CHUNK == 0
    t = b * l

    rows = (b + 1 + 7) // 8 * 8
    cc = jnp.zeros((rows, d), F32).at[:b].set(c).at[b].set(c_ctx)
    mod = _ada(cc, w_ada[0], b_ada[0])
    sh1, s1, g1, sh2, s2, g2 = [mod[:b, i * d:(i + 1) * d].reshape(b, 1, d) for i in range(6)]
    csh1, cs1 = [jnp.broadcast_to(mod[b, i * d:(i + 1) * d].reshape(1, 1, d), (b, 1, d)) for i in range(2)]

    w_in0 = w_in[0]
    off_gate = POOL_WIDTH + 4 * M_WIDTH
    wpin = w_in0[:, :POOL_WIDTH].astype(BF16)
    wqkvo = w_in0[:, POOL_WIDTH:off_gate].astype(BF16)
    order = [4 * (2 * dirn + is_f) + HEADS_PER_STEP * pair + j
             for pair in range(M_HEADS // HEADS_PER_STEP) for is_f in range(2) for dirn in range(2)
             for j in range(HEADS_PER_STEP)]
    wg_hi, wg_lo = _split_bf16(w_in0[:, off_gate:off_gate + N_GATE][:, order])
    wgate = jnp.pad(jnp.concatenate([wg_hi, wg_lo], axis=1), ((0, 0), (0, LANES - 2 * N_GATE)))
    wmerge = w_in0[:, off_gate + N_GATE:].astype(BF16)
    gbias = jnp.pad(gate_b[0][jnp.asarray(order)], (0, LANES - N_GATE)).reshape(1, LANES)
    nw1 = norm1_w[0].reshape(1, d)
    proj_consts = (nw1, gbias, wqkvo, wgate, conv_w[0], conv_b[0])

    qt, k, vt, og, gate_rows = _inproj(x, sh1, s1, *proj_consts)
    _, k_c, vt_c, _, gate_rows_c = _inproj(ctx, csh1, cs1, *proj_consts)
    hn = _mlstm(qt, k, vt, gate_rows, k_c, vt_c, gate_rows_c)
    x2, xn2, slot, slot_wts, cnt = _merge_route(
        x, hn, og, sh1, s1, g1, sh2, s2, nw1, wpin, wmerge, w_pool[0].astype(BF16),
        pool_scale[0].reshape(1, POOL_WIDTH), w_bp[0].astype(BF16), hnorm_w[0], w_bm[0].astype(BF16),
        w_out[0].astype(BF16), norm2_w[0].reshape(1, d), w_router[0], b_router[0])

    nt = t // ROUTE_TILE
    nb = (t * TOP_K + nt * N_EXPERTS * (SEG_ALIGN - 1)) // EXPERT_BLOCK + N_EXPERTS
    n_flat, off_flat, dst_flat, blk_e, nb_used, tails = _routing_tables(cnt[:, :, 0], nb)
    xs = _dispatch(xn2.reshape(t, d), slot, n_flat, off_flat, dst_flat, tails, nb * EXPERT_BLOCK)
    out_rows = _expert(xs, blk_e, nb_used, w1[0], b1[0], w2[0], b2[0])
    y = _combine(out_rows, slot_wts, x2.reshape(t, d), g2, final_norm_w, n_flat, off_flat, dst_flat,
                 l // ROUTE_TILE)
    return y.reshape(b, l, d)
```

```python
import numpy as np
import jax
import jax.numpy as jnp
from jax import lax
from jax.experimental import pallas as pl
from jax.experimental.pallas import tpu as pltpu

F32 = jnp.float32
BF16 = jnp.bfloat16
I32 = jnp.int32
U32 = jnp.uint32
HIGHEST = lax.Precision.HIGHEST

D_MODEL = 1024
EPS = 1e-6
GRID_W = 64
POOL_WINDOWS = (2, 4, 8, 16)
POOL_WIDTH = 512
POOL_GC = 128
M_HEADS = 4
M_HEAD_DIM = 128
M_WIDTH = 512
CHUNK = 128
N_GATE = 16
N_EXPERTS = 32
TOP_K = 4
D_FF = 1024
SWIGLU_LIMIT = 7.0
SWIGLU_ALPHA = 1.702

LANES = 128
PROJ_TILE = 512
POOL_SLAB = 256
ROUTE_TILE = 512
BF16_ROWS = 16
SEG_ALIGN = 8
SEG_PAD = TOP_K * ROUTE_TILE + N_EXPERTS * SEG_ALIGN
PAIR_WIDTH = D_MODEL // 2
EXPERT_BLOCK = 1024
VMEM_LIMIT = 56 * 1024 * 1024


def _dot(a, b, precision=None):
    return jnp.dot(a, b, preferred_element_type=F32, precision=precision)


def _params(semantics):
    return pltpu.CompilerParams(dimension_semantics=semantics, vmem_limit_bytes=VMEM_LIMIT)


def _full(shape):
    nd = len(shape)
    return pl.BlockSpec(shape, lambda *_: (0,) * nd, pipeline_mode=pl.Buffered(1))


def _pack_pairs(v):
    half = v.shape[1] // 2
    lo = lax.bitcast_convert_type(v[:, :half], U32) >> 16
    hi = lax.bitcast_convert_type(v[:, half:], U32) & jnp.uint32(0xFFFF0000)
    return lo | hi


def _unpack_pairs(u):
    lo = lax.bitcast_convert_type(u << 16, F32)
    hi = lax.bitcast_convert_type(u & jnp.uint32(0xFFFF0000), F32)
    return jnp.concatenate([lo, hi], axis=1).astype(BF16)


def _split_bf16(a):
    hi = a.astype(BF16)
    return hi, (a - hi.astype(F32)).astype(BF16)


def _ada_kernel(c_ref, w_ref, b_ref, o_ref):
    c = c_ref[...]
    s = c * jax.nn.sigmoid(c)
    o_ref[...] = _dot(s, w_ref[...], HIGHEST) + b_ref[...]


def _ada(cc, w_ada, b_ada):
    rows, d = cc.shape
    n = w_ada.shape[1]
    tn = D_MODEL
    return pl.pallas_call(
        _ada_kernel,
        grid=(n // tn,),
        in_specs=[pl.BlockSpec((rows, d), lambda j: (0, 0)),
                  pl.BlockSpec((d, tn), lambda j: (0, j)),
                  pl.BlockSpec((1, tn), lambda j: (0, j))],
        out_specs=pl.BlockSpec((rows, tn), lambda j: (0, j)),
        out_shape=jax.ShapeDtypeStruct((rows, n), F32),
        compiler_params=_params(("arbitrary",)),
        name="ada",
    )(cc, w_ada, b_ada.reshape(1, n))


def _pool_constants(tm):
    pos = np.arange(tm) % GRID_W
    row = np.arange(tm) // GRID_W
    slab = min(tm, POOL_SLAB)
    pm = np.zeros((len(POOL_WINDOWS), slab, slab), np.float32)
    inv = np.zeros((tm, POOL_WIDTH), np.float32)
    for g, win in enumerate(POOL_WINDOWS):
        lo = np.clip(pos - win // 2, 0, GRID_W)
        hi = np.clip(pos + win // 2, 0, GRID_W)
        same = row[:, None] == row[None, :]
        full = (same & (pos[None, :] >= lo[:, None]) & (pos[None, :] < hi[:, None])).astype(np.float32)
        pm[g] = full[:slab, :slab]
        inv[:, g * POOL_GC:(g + 1) * POOL_GC] = (1.0 / (hi - lo).astype(np.float32))[:, None]
    t = np.arange(tm)
    same_chunk = (t[:, None] // CHUNK) == (t[None, :] // CHUNK)
    tl = (same_chunk & (t[None, :] <= t[:, None])).astype(np.float32)
    tu = (same_chunk & (t[None, :] >= t[:, None])).astype(np.float32)
    return pm, inv, tl, tu


def _modulated_norm(x, nw, scale, shift):
    xn = x * lax.rsqrt(jnp.mean(x * x, axis=-1, keepdims=True) + EPS) * nw
    return xn * (1.0 + scale) + shift


F32_ROWS = 8


def _inproj_kernel(x_ref, xprev_ref, xnext_ref, sh_ref, sc_ref, nw_ref, gbias_ref, wqkvo_ref, wgate_ref,
                   tl_ref, tu_ref, cw_ref, cb_ref, qt_ref, k_ref, vt_ref, og_ref, row_ref):
    i = pl.program_id(1)
    last = pl.num_programs(1) - 1
    xm = _modulated_norm(x_ref[0], nw_ref[...], sc_ref[0], sh_ref[0])
    xb = xm.astype(BF16)
    tm = xm.shape[0]

    xlo = (xm - xb.astype(F32)).astype(BF16)
    wg = wgate_ref[...]
    r_hi = _dot(xb, wg)
    g = r_hi + pltpu.roll(r_hi, LANES - N_GATE, axis=1) + _dot(xlo, wg) + gbias_ref[...]
    lf = jnp.minimum(g, 0.0) - jnp.log1p(jnp.exp(-jnp.abs(g)))
    lane = lax.broadcasted_iota(I32, g.shape, 1)
    lf_hi = lf.astype(BF16).astype(F32)
    lf2 = jnp.where(lane < N_GATE, lf_hi, pltpu.roll(lf - lf_hi, N_GATE, axis=1)).astype(BF16)
    r_pre = _dot(tl_ref[...], lf2)
    r_suf = _dot(tu_ref[...], lf2)
    b_pre = r_pre + pltpu.roll(r_pre, LANES - N_GATE, axis=1)
    b_suf = r_suf + pltpu.roll(r_suf, LANES - N_GATE, axis=1)
    kind = (lane & 7) >> 1
    col = jnp.where(kind == 2, b_pre, jnp.where(kind == 3, b_suf, g))
    rows = col.T[:N_GATE]
    for ci in range(tm // CHUNK):
        row_ref[0, ci] = rows[:, ci * CHUNK:(ci + 1) * CHUNK]

    wqk = wqkvo_ref[:, 0:2 * M_WIDTH]
    qk = _dot(xb, wqk)
    halo = jnp.concatenate([xprev_ref[0], xnext_ref[0]], axis=0)
    qk_halo = _dot(_modulated_norm(halo, nw_ref[...], sc_ref[0], sh_ref[0]).astype(BF16), wqk)
    prev_row = jnp.where(i > 0, qk_halo[F32_ROWS - 1:F32_ROWS], 0.0)
    next_row = jnp.where(i < last, qk_halo[F32_ROWS:F32_ROWS + 1], 0.0)
    rio = lax.broadcasted_iota(I32, qk.shape, 0)
    qk_prev = jnp.where(rio == 0, prev_row, pltpu.roll(qk, 1, axis=0))
    qk_next = jnp.where(rio == tm - 1, next_row, pltpu.roll(qk, tm - 1, axis=0))
    cw = cw_ref[...]
    acc = cb_ref[...] + qk_prev * cw[0:1] + qk * cw[1:2] + qk_next * cw[2:3]
    y = acc * jax.nn.sigmoid(acc)
    q = y[:, :M_WIDTH] * (M_HEAD_DIM ** -0.5)
    for ci in range(tm // CHUNK):
        qt_ref[0, ci] = q[ci * CHUNK:(ci + 1) * CHUNK].T.astype(BF16)
        k_ref[0, ci] = y[ci * CHUNK:(ci + 1) * CHUNK, M_WIDTH:].astype(BF16)

    v = _dot(xb, wqkvo_ref[:, 2 * M_WIDTH:3 * M_WIDTH])
    for ci in range(v.shape[0] // CHUNK):
        vt_ref[0, ci] = v[ci * CHUNK:(ci + 1) * CHUNK].T.astype(BF16)
    og_ref[0] = jax.nn.sigmoid(_dot(xb, wqkvo_ref[:, 3 * M_WIDTH:4 * M_WIDTH])).astype(BF16)


def _chunk_t_out(b, l, tm):
    return (pl.BlockSpec((1, tm // CHUNK, M_WIDTH, CHUNK), lambda bi, i: (bi, i, 0, 0)),
            jax.ShapeDtypeStruct((b, l // CHUNK, M_WIDTH, CHUNK), BF16))


def _inproj(x, shift, scale, nw, gbias, wqkvo, wgate, conv_w, conv_b):
    b, l, d = x.shape
    tm = min(PROJ_TILE, l)
    per = tm // F32_ROWS
    nblk = l // F32_ROWS
    _, _, tl, tu = _pool_constants(tm)
    tok = lambda width: pl.BlockSpec((1, tm, width), lambda bi, i: (bi, i, 0))
    vec = pl.BlockSpec((1, 1, d), lambda bi, i: (bi, 0, 0))
    halo_before = pl.BlockSpec((1, F32_ROWS, d), lambda bi, i: (bi, jnp.maximum(i * per - 1, 0), 0))
    halo_after = pl.BlockSpec((1, F32_ROWS, d), lambda bi, i: (bi, jnp.minimum((i + 1) * per, nblk - 1), 0))
    consts = [nw, gbias, wqkvo, wgate, jnp.asarray(tl, BF16), jnp.asarray(tu, BF16), conv_w,
              conv_b.reshape(1, 2 * M_WIDTH)]
    tok_out = lambda width: (tok(width), jax.ShapeDtypeStruct((b, l, width), BF16))
    gate_out = (pl.BlockSpec((1, tm // CHUNK, N_GATE, CHUNK), lambda bi, i: (bi, i, 0, 0)),
                jax.ShapeDtypeStruct((b, l // CHUNK, N_GATE, CHUNK), F32))
    k_out = (pl.BlockSpec((1, tm // CHUNK, CHUNK, M_WIDTH), lambda bi, i: (bi, i, 0, 0)),
             jax.ShapeDtypeStruct((b, l // CHUNK, CHUNK, M_WIDTH), BF16))
    outs = [_chunk_t_out(b, l, tm), k_out, _chunk_t_out(b, l, tm), tok_out(M_WIDTH), gate_out]
    return pl.pallas_call(
        _inproj_kernel,
        grid=(b, l // tm),
        in_specs=[tok(d), halo_before, halo_after, vec, vec] + [_full(c.shape) for c in consts],
        out_specs=[spec for spec, _ in outs],
        out_shape=[shape for _, shape in outs],
        compiler_params=_params(("arbitrary", "arbitrary")),
        name="inproj",
    )(x, x, x, shift, scale, *consts)


HEADS_PER_STEP = 2
N_CHAINS = 2 * HEADS_PER_STEP
GATE_ROWS = 2 * N_CHAINS


STATE_ROWS = M_HEAD_DIM + BF16_ROWS


def _lane_pair(rows2):
    return jnp.concatenate([rows2[0:1], rows2[1:2]], axis=1)


def _block_diag(a, b):
    za, zb = jnp.zeros((a.shape[0], b.shape[1]), a.dtype), jnp.zeros((b.shape[0], a.shape[1]), a.dtype)
    return jnp.concatenate([jnp.concatenate([a, za], axis=1), jnp.concatenate([zb, b], axis=1)], axis=0)


def _side_by_side(stacked):
    d = stacked.shape[0] // 2
    return jnp.concatenate([stacked[:d], stacked[d:]], axis=1)


def _state_step(k2, vt2, i_row, b_row, btot, m_old, m_new, state):
    t = k2.shape[0]
    a = jnp.exp(btot - b_row + i_row - m_new)
    cd = jnp.exp(btot + m_old - m_new)
    ones = (lax.broadcasted_iota(I32, (STATE_ROWS - M_HEAD_DIM, 2 * t), 0) == 0).astype(F32)
    vta = (jnp.concatenate([_side_by_side(vt2).astype(F32), ones], axis=0) * a).astype(BF16)
    return cd * state + _dot(vta, _block_diag(k2[:, :M_HEAD_DIM], k2[:, M_HEAD_DIM:]))


def _chunk_out_t(qt2, k2, vt2, i_row, b_row, mask_t, m_old, state):
    t = k2.shape[0]
    g = i_row - b_row
    g_src = jnp.concatenate([jnp.broadcast_to(g[:, :t], (t, t)).T, jnp.broadcast_to(g[:, t:], (t, t)).T],
                            axis=1)
    logw = jnp.where(mask_t, b_row + g_src, -jnp.inf)
    m_inter = b_row + m_old
    m_t = jnp.maximum(jnp.max(logw, axis=0, keepdims=True), m_inter)
    q_bd = _block_diag(qt2[:M_HEAD_DIM], qt2[M_HEAD_DIM:])
    both = _dot(jnp.concatenate([k2, state.astype(BF16)], axis=0), q_bd)
    s = both[:t] * jnp.exp(logw - m_t)
    inter = both[t:]
    decay = jnp.exp(m_inter - m_t)
    s16 = s.astype(BF16)
    num = _dot(_side_by_side(vt2), _block_diag(s16[:, :t], s16[:, t:])) + decay * inter[:M_HEAD_DIM]
    den = jnp.sum(s, axis=0, keepdims=True) + decay * inter[M_HEAD_DIM:M_HEAD_DIM + 1]
    return num * (1.0 / jnp.maximum(jnp.abs(den), jnp.exp(-m_t)))


def _mlstm_kernel(qt_ref, k_ref, vt_ref, row_ref, kc_ref, vtc_ref, rowc_ref, out_ref,
                  h_ref, bt_ref, ma_ref, mold_ref, mnew_ref):
    nc, ncc, t = k_ref.shape[1], kc_ref.shape[1], CHUNK
    src = lax.broadcasted_iota(I32, (t, t), 0)
    tgt = lax.broadcasted_iota(I32, (t, t), 1)
    causal_t = jnp.concatenate([src <= tgt] * HEADS_PER_STEP, axis=1)
    masks_t = (causal_t, jnp.concatenate([src >= tgt] * HEADS_PER_STEP, axis=1))
    grow = pl.ds(pl.multiple_of(pl.program_id(1) * GATE_ROWS, GATE_ROWS), GATE_ROWS)
    head = [slice(j * M_HEAD_DIM, (j + 1) * M_HEAD_DIM) for j in range(HEADS_PER_STEP)]
    dir_rows = [slice(HEADS_PER_STEP * dirn, HEADS_PER_STEP * (dirn + 1)) for dirn in range(2)]

    def gate_pairs(rows, dirn):
        return (_lane_pair(rows[dir_rows[dirn]]),
                _lane_pair(rows[N_CHAINS + HEADS_PER_STEP * dirn:N_CHAINS + HEADS_PER_STEP * (dirn + 1)]))

    def bcast_pair(col2):
        return _lane_pair(jnp.broadcast_to(col2, (HEADS_PER_STEP, t)))

    state = [jnp.zeros((STATE_ROWS, HEADS_PER_STEP * M_HEAD_DIM), F32)] * 2
    m_st = [jnp.zeros((HEADS_PER_STEP, 1), F32)] * 2
    for dirn in range(2):
        for ci in (range(ncc) if dirn == 0 else reversed(range(ncc))):
            rows = rowc_ref[0, ci, grow, :]
            gi2 = rows[dir_rows[dirn]]
            gb2 = rows[N_CHAINS + HEADS_PER_STEP * dirn:N_CHAINS + HEADS_PER_STEP * (dirn + 1)]
            btot = gb2[:, t - 1:t] if dirn == 0 else gb2[:, 0:1]
            m_new = jnp.maximum(btot + m_st[dirn], jnp.max(btot - gb2 + gi2, axis=1, keepdims=True))
            i_row, b_row = gate_pairs(rows, dirn)
            state[dirn] = _state_step(kc_ref[0, ci], vtc_ref[0, ci], i_row, b_row, bcast_pair(btot),
                                      bcast_pair(m_st[dirn]), bcast_pair(m_new), state[dirn])
            m_st[dirn] = m_new

    gates = row_ref[0, :, grow, :]
    gi, gb = gates[:, :N_CHAINS], gates[:, N_CHAINS:]
    is_fwd = lax.broadcasted_iota(I32, gb.shape, 1) < HEADS_PER_STEP
    bt = jnp.where(is_fwd, jnp.broadcast_to(gb[:, :, t - 1:t], gb.shape),
                   jnp.broadcast_to(gb[:, :, 0:1], gb.shape))
    bt_ref[...] = bt
    ma_ref[...] = jnp.broadcast_to(jnp.max(bt - gb + gi, axis=2, keepdims=True), gb.shape)

    def m_scan(s, carry):
        m_f, m_b = carry
        cf, cb = s, nc - 1 - s
        mold_ref[cf, 0:HEADS_PER_STEP, :] = m_f[0:HEADS_PER_STEP]
        m_f = jnp.maximum(bt_ref[cf] + m_f, ma_ref[cf])
        mnew_ref[cf, 0:HEADS_PER_STEP, :] = m_f[0:HEADS_PER_STEP]
        mold_ref[cb, HEADS_PER_STEP:, :] = m_b[HEADS_PER_STEP:]
        m_b = jnp.maximum(bt_ref[cb] + m_b, ma_ref[cb])
        mnew_ref[cb, HEADS_PER_STEP:, :] = m_b[HEADS_PER_STEP:]
        return m_f, m_b

    m0 = jnp.concatenate([jnp.broadcast_to(m, (HEADS_PER_STEP, t)) for m in m_st], axis=0)
    lax.fori_loop(0, nc, m_scan, (m0, m0))

    def advance(ci, dirn, st):
        i_row, b_row = gate_pairs(row_ref[0, ci, grow, :], dirn)
        m_old, m_new, btot = [_lane_pair(tab[ci][dir_rows[dirn]]) for tab in (mold_ref, mnew_ref, bt_ref)]
        qt2, k2, vt2 = qt_ref[0, ci], k_ref[0, ci], vt_ref[0, ci]
        h_ref[dirn, ci] = _chunk_out_t(qt2, k2, vt2, i_row, b_row, masks_t[dirn], m_old, st)
        return _state_step(k2, vt2, i_row, b_row, btot, m_old, m_new, st)

    def body(s, states):
        return advance(s, 0, states[0]), advance(nc - 1 - s, 1, states[1])

    lax.fori_loop(0, nc, body, tuple(state), unroll=8)

    def norm_body(ci, _):
        ht2 = h_ref[0, ci] + h_ref[1, ci]
        for j in range(HEADS_PER_STEP):
            ht = ht2[:, j * t:(j + 1) * t]
            hn = ht * lax.rsqrt(jnp.mean(ht * ht, axis=0, keepdims=True) + EPS)
            out_ref[0, ci, :, head[j]] = hn.T.astype(BF16)
        return 0

    lax.fori_loop(0, nc, norm_body, 0, unroll=8)


def _mlstm(qt, k, vt, gate_rows, kc, vtc, gate_rows_c):
    b, nc = k.shape[:2]
    ncc = kc.shape[1]
    pairs = M_HEADS // HEADS_PER_STEP
    width = HEADS_PER_STEP * M_HEAD_DIM
    once = pl.Buffered(1)
    seq = lambda n: pl.BlockSpec((1, n, CHUNK, width), lambda bi, h: (bi, 0, 0, h))
    seq_t = lambda n: pl.BlockSpec((1, n, width, CHUNK), lambda bi, h: (bi, 0, h, 0))
    rows = lambda n: pl.BlockSpec((1, n, pairs * GATE_ROWS, CHUNK), lambda bi, h: (bi, 0, 0, 0),
                                  pipeline_mode=once)
    table = pltpu.VMEM((nc, N_CHAINS, CHUNK), F32)
    return pl.pallas_call(
        _mlstm_kernel,
        grid=(b, pairs),
        in_specs=[seq_t(nc), seq(nc), seq_t(nc), rows(nc), seq(ncc), seq_t(ncc), rows(ncc)],
        out_specs=pl.BlockSpec((1, nc, CHUNK, width), lambda bi, h: (bi, 0, 0, h)),
        out_shape=jax.ShapeDtypeStruct((b, nc, CHUNK, M_WIDTH), BF16),
        scratch_shapes=[pltpu.VMEM((2, nc, M_HEAD_DIM, HEADS_PER_STEP * CHUNK), F32),
                        table, table, table, table],
        compiler_params=_params(("arbitrary", "arbitrary")),
        name="mlstm",
    )(qt, k, vt, gate_rows, kc, vtc, gate_rows_c)


def _merge_kernel(x_ref, hn_ref, og_ref, sh_ref, sc_ref, g1_ref, sh2_ref, sc2_ref, nw_ref, wpin_ref,
                  wmerge_ref, pmat_ref, invc_ref, wpool_ref, pscale_ref, wbp_ref, hw_ref, wbm_ref, wout_ref,
                  nw2_ref, wr_ref, br_ref, ustrict_ref, lstrict_ref,
                  o_ref, xn_ref, slot_ref, sw_ref, cnt_ref):
    x = x_ref[0]
    xb = _modulated_norm(x, nw_ref[...], sc_ref[0], sh_ref[0]).astype(BF16)

    u = _dot(xb, wpin_ref[...])
    u_hi, u_lo = _split_bf16(u)
    invc = invc_ref[...]
    slab = pmat_ref.shape[1]
    ya = []
    for gi in range(len(POOL_WINDOWS)):
        cols = slice(gi * POOL_GC, (gi + 1) * POOL_GC)
        both = jnp.concatenate([u_hi[:, cols], u_lo[:, cols]], axis=1)
        sums = jnp.concatenate([_dot(pmat_ref[gi], both[s:s + slab]) for s in range(0, u.shape[0], slab)],
                               axis=0)
        pooled = (sums[:, :POOL_GC] + sums[:, POOL_GC:]) * invc[:, cols] - u[:, cols]
        ya.append(_dot(pooled.astype(BF16), wpool_ref[gi]))
    ya = jnp.concatenate(ya, axis=1) * pscale_ref[...]
    y = jax.nn.sigmoid(_dot(xb, wmerge_ref[:, 0:D_MODEL])) * _dot(ya.astype(BF16), wbp_ref[...])

    hn = jnp.concatenate([hn_ref[0, ci] for ci in range(hn_ref.shape[1])], axis=0)
    yb = hn.astype(F32) * hw_ref[...] * og_ref[0].astype(F32)
    y = y + jax.nn.sigmoid(_dot(xb, wmerge_ref[:, D_MODEL:2 * D_MODEL])) * _dot(yb.astype(BF16), wbm_ref[...])
    x2 = x + g1_ref[0] * _dot(y.astype(BF16), wout_ref[...])
    o_ref[0] = x2

    _route_tile(x2, sh2_ref, sc2_ref, nw2_ref, wr_ref, br_ref, ustrict_ref, lstrict_ref,
                xn_ref, slot_ref, sw_ref, cnt_ref)


def _merge_route(x, hn, og, shift, scale, g1, shift2, scale2, nw, wpin, wmerge, wpool, pscale, wbp, hnorm_w,
                 wbm, wout, nw2, w_router, b_router):
    b, l, d = x.shape
    tm = ROUTE_TILE
    nl = l // tm
    nt = b * nl
    pm, inv, _, _ = _pool_constants(tm)
    tok = lambda width: pl.BlockSpec((1, tm, width), lambda bi, i: (bi, i, 0))
    vec = pl.BlockSpec((1, 1, d), lambda bi, i: (bi, 0, 0))
    tile = lambda r, c: pl.BlockSpec((1, r, c), lambda bi, i: (bi * nl + i, 0, 0))
    consts = [nw, wpin, wmerge, jnp.asarray(pm, BF16), jnp.asarray(inv), wpool, pscale, wbp,
              hnorm_w.reshape(1, M_WIDTH), wbm, wout, nw2] + _router_consts(w_router, b_router)
    return pl.pallas_call(
        _merge_kernel,
        grid=(b, nl),
        in_specs=[tok(d), pl.BlockSpec((1, tm // CHUNK, CHUNK, M_WIDTH), lambda bi, i: (bi, i, 0, 0)),
                  tok(M_WIDTH), vec, vec, vec, vec, vec]
        + [_full(c.shape) for c in consts],
        out_specs=[tok(d), tok(d), tile(8, tm), pl.BlockSpec((tm, 2 * TOP_K), lambda bi, i: (bi * nl + i, 0)),
                   tile(N_EXPERTS, LANES)],
        out_shape=[jax.ShapeDtypeStruct((b, l, d), F32),
                   jax.ShapeDtypeStruct((b, l, d), BF16),
                   jax.ShapeDtypeStruct((nt, 8, tm), I32),
                   jax.ShapeDtypeStruct((nt * tm, 2 * TOP_K), F32),
                   jax.ShapeDtypeStruct((nt, N_EXPERTS, LANES), F32)],
        compiler_params=_params(("arbitrary", "arbitrary")),
        name="merge_route",
    )(x, hn, og, shift, scale, g1, shift2, scale2, *consts)


def _route_tile(x, sh_ref, sc_ref, nw_ref, wr_ref, br_ref, ustrict_ref, lstrict_ref,
                xn_ref, slot_ref, sw_ref, cnt_ref):
    tm = x.shape[0]
    xm = _modulated_norm(x, nw_ref[...], sc_ref[0], sh_ref[0])
    xn_ref[0] = xm.astype(BF16)

    x_hi, x_lo = _split_bf16(xm)
    r_hi = _dot(x_hi, wr_ref[...])
    r = r_hi + pltpu.roll(r_hi, LANES - N_EXPERTS, axis=1) + _dot(x_lo, wr_ref[...])
    logits = r.T[:N_EXPERTS] + br_ref[...]
    eio = lax.broadcasted_iota(I32, logits.shape, 0).astype(F32)
    rest = logits
    onehots, vals = [], []
    for _ in range(TOP_K):
        mx = jnp.max(rest, axis=0, keepdims=True)
        idx = jnp.min(jnp.where(rest == mx, eio, float(N_EXPERTS)), axis=0, keepdims=True)
        oh = eio == idx
        onehots.append(oh)
        vals.append(mx)
        rest = jnp.where(oh, -jnp.inf, rest)
    exps = [jnp.exp(vk - vals[0]) for vk in vals]
    denom = exps[0] + exps[1] + exps[2] + exps[3]

    oh_all = jnp.zeros(logits.shape, F32)
    for oh in onehots:
        oh_all = oh_all + oh.astype(F32)
    cnt = jnp.sum(oh_all, axis=1, keepdims=True)
    n_al = jnp.ceil(cnt * (1.0 / SEG_ALIGN)) * SEG_ALIGN
    seg_off = _dot(lstrict_ref[...], jnp.broadcast_to(n_al, (N_EXPERTS, LANES)).astype(BF16))[:, 0:1]
    rank = _dot(oh_all.astype(BF16), ustrict_ref[...])
    base = seg_off + rank
    slots = [jnp.sum(jnp.where(onehots[kk], base, 0.0), axis=0, keepdims=True) for kk in range(TOP_K)]
    for kk in range(TOP_K):
        slot_ref[0, kk:kk + 1, :] = slots[kk].astype(I32)
    slot_ref[0, TOP_K:, :] = jnp.full((8 - TOP_K, tm), -1, I32)
    per_token = jnp.concatenate(slots + [e / denom for e in exps] + [jnp.zeros((LANES - 2 * TOP_K, tm), F32)],
                                axis=0)
    sw_ref[...] = per_token.T[:, :2 * TOP_K]
    cnt_ref[0] = jnp.broadcast_to(cnt, (N_EXPERTS, LANES))


def _router_consts(w_router, b_router):
    s = np.arange(ROUTE_TILE)
    ustrict = jnp.asarray((s[:, None] < s[None, :]).astype(np.float32), BF16)
    e = np.arange(N_EXPERTS)
    lstrict = jnp.asarray((e[None, :] < e[:, None]).astype(np.float32), BF16)
    w_cat = jnp.pad(jnp.concatenate(_split_bf16(w_router), axis=1), ((0, 0), (0, LANES - 2 * N_EXPERTS)))
    return [w_cat, b_router.reshape(N_EXPERTS, 1), ustrict, lstrict]


def _segment_starts(tile, n_s, off_s, dst_s, make_copy):
    def body(e, _):
        idx = tile * N_EXPERTS + e
        n = n_s[idx]

        @pl.when(n > 0)
        def _():
            make_copy(pl.multiple_of(off_s[idx], SEG_ALIGN), pl.multiple_of(dst_s[idx], SEG_ALIGN),
                      pl.multiple_of(n, SEG_ALIGN)).start()

        return 0

    lax.fori_loop(0, N_EXPERTS, body, 0, unroll=8)


def _segment_wait(tile, n_s, make_copy):
    total = n_s[pl.num_programs(0) * N_EXPERTS + tile]
    make_copy(0, 0, pl.multiple_of(total, SEG_ALIGN)).wait()


def _zero_fill(tail_s, xs_hbm, zero_ref, sem, action):
    def make_copy(dst, sz):
        return pltpu.make_async_copy(zero_ref.at[pl.ds(0, sz)], xs_hbm.at[pl.ds(dst, sz)], sem)

    def tail_body(e, _):
        n = tail_s[N_EXPERTS + e]

        @pl.when(n > 0)
        def _():
            action(make_copy(pl.multiple_of(tail_s[e], SEG_ALIGN), pl.multiple_of(n, SEG_ALIGN)))

        return 0

    lax.fori_loop(0, N_EXPERTS, tail_body, 0)

    def block_body(blk, _):
        action(make_copy(pl.multiple_of(blk * EXPERT_BLOCK, EXPERT_BLOCK), EXPERT_BLOCK))
        return 0

    lax.fori_loop(tail_s[2 * N_EXPERTS], tail_s[2 * N_EXPERTS + 1], block_body, 0)


ROUTE_CHUNK = 768
assert SEG_PAD % ROUTE_CHUNK == 0


def _dispatch_kernel(n_s, off_s, dst_s, tail_s, xn_ref, slot_ref, xs_hbm, g_ref, zero_ref, sems):
    tile = pl.program_id(0)
    last = pl.num_programs(0) - 1
    cur = tile % 2

    def copier(of_tile):
        par = of_tile % 2
        return lambda src, dst, n: pltpu.make_async_copy(
            g_ref.at[par, pl.ds(src, n)], xs_hbm.at[pl.ds(dst, n)], sems.at[par])

    @pl.when(tile >= 2)
    def _():
        _segment_wait(tile - 2, n_s, copier(tile - 2))

    x = xn_ref[...]
    tm = x.shape[0]
    slot = slot_ref[0].astype(jnp.int16)
    rio = lax.broadcasted_iota(I32, (ROUTE_CHUNK, tm), 0).astype(jnp.int16)
    one = jnp.ones((), BF16)
    for ci in range(SEG_PAD // ROUTE_CHUNK):
        rel = slot - ci * ROUTE_CHUNK
        sel = jnp.zeros((ROUTE_CHUNK, tm), BF16)
        for kk in range(TOP_K):
            sel = jnp.where(rio == rel[kk:kk + 1, :], one, sel)
        g_ref[cur, ci * ROUTE_CHUNK:(ci + 1) * ROUTE_CHUNK, :] = _pack_pairs(_dot(sel, x))

    _segment_starts(tile, n_s, off_s, dst_s, copier(tile))

    @pl.when(tile == last)
    def _():
        @pl.when(tile >= 1)
        def _():
            _segment_wait(tile - 1, n_s, copier(tile - 1))

        _segment_wait(tile, n_s, copier(tile))
        zero_ref[...] = jnp.zeros(zero_ref.shape, U32)
        _zero_fill(tail_s, xs_hbm, zero_ref, sems.at[2], lambda cp: cp.start())
        _zero_fill(tail_s, xs_hbm, zero_ref, sems.at[2], lambda cp: cp.wait())


def _dispatch(xn2, slot, n_flat, off_flat, dst_flat, tail_flat, rows_total):
    t, d = xn2.shape
    tm = ROUTE_TILE
    nt = t // tm
    return pl.pallas_call(
        _dispatch_kernel,
        grid_spec=pltpu.PrefetchScalarGridSpec(
            num_scalar_prefetch=4,
            grid=(nt,),
            in_specs=[pl.BlockSpec((tm, d), lambda i, *_: (i, 0)),
                      pl.BlockSpec((1, 8, tm), lambda i, *_: (i, 0, 0))],
            out_specs=pl.BlockSpec(memory_space=pl.ANY),
            scratch_shapes=[pltpu.VMEM((2, SEG_PAD, PAIR_WIDTH), U32),
                            pltpu.VMEM((EXPERT_BLOCK, PAIR_WIDTH), U32),
                            pltpu.SemaphoreType.DMA((3,))]),
        out_shape=jax.ShapeDtypeStruct((rows_total, PAIR_WIDTH), U32),
        compiler_params=_params(("arbitrary",)),
        name="dispatch",
    )(n_flat, off_flat, dst_flat, tail_flat, xn2, slot)


def _combine_kernel(n_s, off_s, dst_s, out_hbm, sw_ref, x2_ref, g2_ref, fw_ref, y_ref,
                    buf_ref, sel_ref, sems):
    tile = pl.program_id(0)
    cur = tile % 2

    def copier(of_tile):
        par = of_tile % 2
        return lambda seg, src, n: pltpu.make_async_copy(
            out_hbm.at[pl.ds(src, n)], buf_ref.at[par, pl.ds(seg, n)], sems.at[par])

    @pl.when(tile == 0)
    def _():
        buf_ref[...] = jnp.zeros(buf_ref.shape, U32)
        _segment_starts(tile, n_s, off_s, dst_s, copier(tile))

    @pl.when(tile + 1 < pl.num_programs(0))
    def _():
        _segment_starts(tile + 1, n_s, off_s, dst_s, copier(tile + 1))

    _segment_wait(tile, n_s, copier(tile))

    sw = sw_ref[...]
    st = sw[:, :TOP_K].astype(I32).astype(jnp.int16)
    wt = sw[:, TOP_K:].astype(BF16)
    tm = st.shape[0]
    lio = lax.broadcasted_iota(I32, (tm, ROUTE_CHUNK), 1).astype(jnp.int16)
    for ci in range(SEG_PAD // ROUTE_CHUNK):
        rel = st - ci * ROUTE_CHUNK
        sel = jnp.zeros((tm, ROUTE_CHUNK), BF16)
        for kk in range(TOP_K):
            sel = jnp.where(lio == rel[:, kk:kk + 1], wt[:, kk:kk + 1], sel)
        sel_ref[:, ci * ROUTE_CHUNK:(ci + 1) * ROUTE_CHUNK] = sel
    x3 = x2_ref[...] + g2_ref[0] * _dot(sel_ref[...], _unpack_pairs(buf_ref[cur]))
    y_ref[...] = x3 * lax.rsqrt(jnp.mean(x3 * x3, axis=-1, keepdims=True) + EPS) * fw_ref[...]


def _combine(out_rows, slot_wts, x2_flat, g2, final_w, n_flat, off_flat, dst_flat, tiles_per_batch):
    t, d = x2_flat.shape
    tm = ROUTE_TILE
    nt = t // tm
    return pl.pallas_call(
        _combine_kernel,
        grid_spec=pltpu.PrefetchScalarGridSpec(
            num_scalar_prefetch=3,
            grid=(nt,),
            in_specs=[pl.BlockSpec(memory_space=pl.ANY),
                      pl.BlockSpec((tm, 2 * TOP_K), lambda i, *_: (i, 0)),
                      pl.BlockSpec((tm, d), lambda i, *_: (i, 0)),
                      pl.BlockSpec((1, 1, d), lambda i, *_: (i // tiles_per_batch, 0, 0)),
                      pl.BlockSpec((1, d), lambda i, *_: (0, 0))],
            out_specs=pl.BlockSpec((tm, d), lambda i, *_: (i, 0)),
            scratch_shapes=[pltpu.VMEM((2, SEG_PAD, PAIR_WIDTH), U32), pltpu.VMEM((tm, SEG_PAD), BF16),
                            pltpu.SemaphoreType.DMA((2,))]),
        out_shape=jax.ShapeDtypeStruct((t, d), F32),
        compiler_params=_params(("arbitrary",)),
        name="combine",
    )(n_flat, off_flat, dst_flat, out_rows, slot_wts, x2_flat, g2, final_w.reshape(1, d))


def _expert_kernel(blk_e, nb_used, xs_ref, w1_ref, b1_ref, w2_ref, b2_ref, o_ref, w1b_ref, w2b_ref):
    i = pl.program_id(0)
    used = i < nb_used[0]

    @pl.when(jnp.logical_not(used))
    def _():
        o_ref[...] = jnp.zeros(o_ref.shape, U32)

    @pl.when(used & ((i == 0) | (blk_e[i] != blk_e[jnp.maximum(i - 1, 0)])))
    def _():
        w1b_ref[...] = w1_ref[0].astype(BF16)
        w2b_ref[...] = w2_ref[0].astype(BF16)

    @pl.when(used)
    def _():
        gu = _dot(_unpack_pairs(xs_ref[...]), w1b_ref[...]) + b1_ref[0]
        gate = jnp.minimum(gu[:, :D_FF], SWIGLU_LIMIT)
        up = jnp.clip(gu[:, D_FF:], -SWIGLU_LIMIT, SWIGLU_LIMIT)
        act = (up + 1.0) * gate * jax.nn.sigmoid(SWIGLU_ALPHA * gate)
        o = _dot(act.astype(BF16), w2b_ref[...]) + b2_ref[0]
        o_ref[...] = _pack_pairs(o.astype(BF16).astype(F32))


def _expert(xs, blk_e, nb_used, w1, b1, w2, b2):
    rows = xs.shape[0]
    nb = rows // EXPERT_BLOCK
    row_blk = lambda i, be, nu: (jnp.maximum(jnp.minimum(i, nu[0] - 1), 0), 0)
    per_e = lambda i, be, nu: (be[i], 0, 0)
    return pl.pallas_call(
        _expert_kernel,
        grid_spec=pltpu.PrefetchScalarGridSpec(
            num_scalar_prefetch=2,
            grid=(nb,),
            in_specs=[pl.BlockSpec((EXPERT_BLOCK, PAIR_WIDTH), row_blk),
                      pl.BlockSpec((1, D_MODEL, 2 * D_FF), per_e),
                      pl.BlockSpec((1, 1, 2 * D_FF), per_e),
                      pl.BlockSpec((1, D_FF, D_MODEL), per_e),
                      pl.BlockSpec((1, 1, D_MODEL), per_e)],
            out_specs=pl.BlockSpec((EXPERT_BLOCK, PAIR_WIDTH), lambda i, be, nu: (i, 0)),
            scratch_shapes=[pltpu.VMEM((D_MODEL, 2 * D_FF), BF16), pltpu.VMEM((D_FF, D_MODEL), BF16)]),
        out_shape=jax.ShapeDtypeStruct((rows, PAIR_WIDTH), U32),
        compiler_params=_params(("arbitrary",)),
        name="expert",
    )(blk_e, nb_used, xs, w1, b1.reshape(N_EXPERTS, 1, 2 * D_FF), w2, b2.reshape(N_EXPERTS, 1, D_MODEL))


def _routing_tables(cnt, nb):
    cnt = cnt.astype(I32)
    n_al = (cnt + SEG_ALIGN - 1) // SEG_ALIGN * SEG_ALIGN
    seg_off = jnp.cumsum(n_al, axis=1) - n_al
    rel = jnp.cumsum(n_al, axis=0) - n_al
    tot = jnp.sum(n_al, axis=0)
    blocks_e = (tot + EXPERT_BLOCK - 1) // EXPERT_BLOCK
    blk_end = jnp.cumsum(blocks_e)
    e_start = (blk_end - blocks_e) * EXPERT_BLOCK
    dst = e_start[None, :] + rel
    blk_e = jnp.minimum(jnp.sum(blk_end[None, :] <= jnp.arange(nb, dtype=I32)[:, None], axis=1),
                        N_EXPERTS - 1).astype(I32)
    nb_used = blk_end[-1:].astype(I32)
    tails = jnp.concatenate([e_start + tot, blocks_e * EXPERT_BLOCK - tot, nb_used,
                             jnp.full((1,), nb, I32)]).astype(I32)
    counts = jnp.concatenate([n_al.reshape(-1), jnp.sum(n_al, axis=1)])
    return (counts, seg_off.reshape(-1).astype(I32), dst.reshape(-1).astype(I32), blk_e, nb_used, tails)


def kernel(x, c, ctx, c_ctx, w_ada, b_ada, norm1_w, w_in, gate_b, conv_w, conv_b, w_pool, pool_scale,
           hnorm_w, w_bp, w_bm, w_out, norm2_w, w_router, b_router, w1, b1, w2, b2, final_norm_w):
    assert w_ada.shape[0] == 1, "single-layer kernel"
    b, l, d = x.shape
    lc = ctx.shape[1]
    assert d == D_MODEL and l % ROUTE_TILE == 0 and l % GRID_W == 0 and lc % CHUNK == 0
    t = b * l

    rows = (b + 1 + 7) // 8 * 8
    cc = jnp.zeros((rows, d), F32).at[:b].set(c).at[b].set(c_ctx)
    mod = _ada(cc, w_ada[0], b_ada[0])
    sh1, s1, g1, sh2, s2, g2 = [mod[:b, i * d:(i + 1) * d].reshape(b, 1, d) for i in range(6)]
    csh1, cs1 = [jnp.broadcast_to(mod[b, i * d:(i + 1) * d].reshape(1, 1, d), (b, 1, d)) for i in range(2)]

    w_in0 = w_in[0]
    off_gate = POOL_WIDTH + 4 * M_WIDTH
    wpin = w_in0[:, :POOL_WIDTH].astype(BF16)
    wqkvo = w_in0[:, POOL_WIDTH:off_gate].astype(BF16)
    order = [4 * (2 * dirn + is_f) + HEADS_PER_STEP * pair + j
             for pair in range(M_HEADS // HEADS_PER_STEP) for is_f in range(2) for dirn in range(2)
             for j in range(HEADS_PER_STEP)]
    wg_hi, wg_lo = _split_bf16(w_in0[:, off_gate:off_gate + N_GATE][:, order])
    wgate = jnp.pad(jnp.concatenate([wg_hi, wg_lo], axis=1), ((0, 0), (0, LANES - 2 * N_GATE)))
    wmerge = w_in0[:, off_gate + N_GATE:].astype(BF16)
    gbias = jnp.pad(gate_b[0][jnp.asarray(order)], (0, LANES - N_GATE)).reshape(1, LANES)
    nw1 = norm1_w[0].reshape(1, d)
    proj_consts = (nw1, gbias, wqkvo, wgate, conv_w[0], conv_b[0])

    qt, k, vt, og, gate_rows = _inproj(x, sh1, s1, *proj_consts)
    _, k_c, vt_c, _, gate_rows_c = _inproj(ctx, csh1, cs1, *proj_consts)
    hn = _mlstm(qt, k, vt, gate_rows, k_c, vt_c, gate_rows_c)
    x2, xn2, slot, slot_wts, cnt = _merge_route(
        x, hn, og, sh1, s1, g1, sh2, s2, nw1, wpin, wmerge, w_pool[0].astype(BF16),
        pool_scale[0].reshape(1, POOL_WIDTH), w_bp[0].astype(BF16), hnorm_w[0], w_bm[0].astype(BF16),
        w_out[0].astype(BF16), norm2_w[0].reshape(1, d), w_router[0], b_router[0])

    nt = t // ROUTE_TILE
    nb = (t * TOP_K + nt * N_EXPERTS * (SEG_ALIGN - 1)) // EXPERT_BLOCK + N_EXPERTS
    n_flat, off_flat, dst_flat, blk_e, nb_used, tails = _routing_tables(cnt[:, :, 0], nb)
    xs = _dispatch(xn2.reshape(t, d), slot, n_flat, off_flat, dst_flat, tails, nb * EXPERT_BLOCK)
    out_rows = _expert(xs, blk_e, nb_used, w1[0], b1[0], w2[0], b2[0])
    y = _combine(out_rows, slot_wts, x2.reshape(t, d), g2, final_norm_w, n_flat, off_flat, dst_flat,
                 l // ROUTE_TILE)
    return y.reshape(b, l, d)
```
